```python
import math
import jax, jax.numpy as jnp
from jax import lax
import numpy as np

D_MODEL = 1024
BATCH = 4
SEQ = 8192
DEPTH = 1

SSD_EXPAND = 2
SSD_D_INNER = SSD_EXPAND * D_MODEL
SSD_HEAD_DIM = 64
SSD_N_HEADS = SSD_D_INNER // SSD_HEAD_DIM
SSD_N_GROUPS = 4
SSD_D_STATE = 128
SSD_CONV_WIDTH = 4
SSD_CHUNK = 256
SSD_CONV_DIM = SSD_D_INNER + 2 * SSD_N_GROUPS * SSD_D_STATE
NSA_N_HEADS = 16
NSA_N_KV = 4
NSA_HEAD_DIM = 64
NSA_WIDTH = NSA_N_HEADS * NSA_HEAD_DIM
NSA_KV_WIDTH = NSA_N_KV * NSA_HEAD_DIM
NSA_CMP_LEN = 32
NSA_CMP_STRIDE = 16
NSA_CMP_HIDDEN = 256
NSA_SEL_LEN = 64
NSA_SEL_TOP = 16
NSA_WINDOW = 512
Q_BLOCK = 128
FORCED_SCORE = 1.0e4
N_EXPERTS = 32
TOP_K = 4
D_EXPERT = D_MODEL
SWIGLU_LIMIT = 7.0
SWIGLU_ALPHA = 1.702
MOE_BLOCK = 512
EPS = 1e-6

IN_SIZES = (SSD_D_INNER, SSD_CONV_DIM, SSD_N_HEADS, NSA_WIDTH) + (NSA_KV_WIDTH,) * 6 + (3 * NSA_N_HEADS, 2 * D_MODEL)
IN_WIDTH = sum(IN_SIZES)
IN_SPLITS = tuple(int(v) for v in np.cumsum(IN_SIZES)[:-1])

kernel_name = 'hybrid_ssd_nsa_moe_block'


def rms_norm(x, w):
    xf = x.astype(jnp.float32)
    y = xf * lax.rsqrt(jnp.mean(xf * xf, axis=-1, keepdims=True) + EPS)
    return (y * w.astype(jnp.float32)).astype(x.dtype)


def causal_dwconv(x, w, b):
    y = lax.conv_general_dilated(x, w[:, None, :], window_strides=(1,), padding=[(w.shape[0] - 1, 0)],
                                 dimension_numbers=('NWC', 'WIO', 'NWC'), feature_group_count=x.shape[-1])
    return y + b


def masked_softmax(s, mask):
    s = jnp.where(mask, s.astype(jnp.float32), -jnp.inf)
    m = jnp.max(s, axis=-1, keepdims=True)
    m = jnp.where(jnp.isfinite(m), m, 0.0)
    p = jnp.where(mask, jnp.exp(s - m), 0.0)
    return p / jnp.maximum(jnp.sum(p, axis=-1, keepdims=True), 1e-30)


def alibi_slopes(n):
    return jnp.asarray(2.0 ** (-8.0 * np.arange(1, n + 1) / n), jnp.float32)


def ssd_mixer(z, xbc, dt_raw, conv_w, conv_b, dt_bias, a_log, d_skip, norm_w):
    bsz, seq, _ = z.shape
    H, P, G, N = SSD_N_HEADS, SSD_HEAD_DIM, SSD_N_GROUPS, SSD_D_STATE
    f32 = jnp.float32
    xbc = jax.nn.silu(causal_dwconv(xbc, conv_w, conv_b))
    xs, b_in, c_in = jnp.split(xbc, [SSD_D_INNER, SSD_D_INNER + G * N], axis=-1)
    xs = xs.reshape(bsz, seq, H, P)
    b_in = b_in.reshape(bsz, seq, G, N)
    c_in = c_in.reshape(bsz, seq, G, N)
    dt = jax.nn.softplus(dt_raw.astype(f32) + dt_bias.astype(f32))
    a = -jnp.exp(a_log.astype(f32))
    chunk = math.gcd(seq, SSD_CHUNK)
    nc = seq // chunk
    rep = H // G
    causal = jnp.tril(jnp.ones((chunk, chunk), bool))[None, :, :, None]

    def to_chunks(t):
        return jnp.moveaxis(t.reshape(bsz, nc, chunk, *t.shape[2:]), 1, 0)

    def step(h, inp):
        xc, dtc, bc, cc = inp
        bh = jnp.repeat(bc.astype(f32), rep, axis=2)
        ch = jnp.repeat(cc.astype(f32), rep, axis=2)
        xf = xc.astype(f32)
        acum = jnp.cumsum(dtc * a, axis=1)
        seg = acum[:, :, None, :] - acum[:, None, :, :]
        decay = jnp.where(causal, jnp.exp(jnp.where(causal, seg, 0.0)), 0.0)
        scores = jnp.einsum('bthn,bshn->btsh', ch, bh) * decay
        y = jnp.einsum('btsh,bsh,bshp->bthp', scores, dtc, xf)
        y = y + jnp.einsum('bthn,bhpn->bthp', ch, h) * jnp.exp(acum)[..., None]
        a_end = acum[:, -1]
        w_end = jnp.exp(a_end[:, None, :] - acum) * dtc
        h = h * jnp.exp(a_end)[:, :, None, None] + jnp.einsum('bshn,bsh,bshp->bhpn', bh, w_end, xf)
        return h, y

    h0 = jnp.zeros((bsz, H, P, N), f32)
    _, y = lax.scan(step, h0, (to_chunks(xs), to_chunks(dt), to_chunks(b_in), to_chunks(c_in)))
    y = jnp.moveaxis(y, 0, 1).reshape(bsz, seq, H, P)
    y = y + d_skip.astype(f32)[:, None] * xs.astype(f32)
    y = y.reshape(bsz, seq, SSD_D_INNER) * jax.nn.silu(z.astype(f32))
    y = rms_norm(y.reshape(bsz, seq, G, SSD_D_INNER // G), norm_w.reshape(G, SSD_D_INNER // G))
    return y.reshape(bsz, seq, SSD_D_INNER).astype(z.dtype)


def nsa_mixer(q, k_c, v_c, k_s, v_s, k_w, v_w, gate_logits, q_norm_w, k_norm_w,
              cmp_pos_k, cmp_pos_v, cmp_k_w1, cmp_k_w2, cmp_v_w1, cmp_v_w2):
    bsz, seq, _ = q.shape
    H, G, Dh = NSA_N_HEADS, NSA_N_KV, NSA_HEAD_DIM
    hpg = H // G
    scale = Dh ** -0.5

    def heads(t, n):
        return jnp.moveaxis(t.reshape(bsz, seq, n, Dh), 2, 1)

    q = rms_norm(heads(q, H), q_norm_w).reshape(bsz, G, hpg, seq, Dh)
    k_s = rms_norm(heads(k_s, G), k_norm_w)
    v_s = heads(v_s, G)
    k_w = rms_norm(heads(k_w, G), k_norm_w)
    v_w = heads(v_w, G)

    n_cmp = (seq - NSA_CMP_LEN) // NSA_CMP_STRIDE + 1
    cmp_idx = np.arange(n_cmp)[:, None] * NSA_CMP_STRIDE + np.arange(NSA_CMP_LEN)[None, :]

    def compress(t, pos, w1, w2):
        blk = heads(t, G)[:, :, cmp_idx] + pos
        blk = blk.reshape(bsz, G, n_cmp, NSA_CMP_LEN * Dh)
        return jax.nn.gelu(blk @ w1) @ w2

    k_cmp = rms_norm(compress(k_c, cmp_pos_k, cmp_k_w1, cmp_k_w2), k_norm_w)
    v_cmp = compress(v_c, cmp_pos_v, cmp_v_w1, cmp_v_w2)
    cmp_end = jnp.asarray(cmp_idx[:, -1], jnp.int32)

    n_sel = seq // NSA_SEL_LEN
    top = min(NSA_SEL_TOP, n_sel)
    sel_start = np.arange(n_sel) * NSA_SEL_LEN
    c_start = cmp_idx[:, 0:1]
    overlap = np.clip(np.minimum(c_start + NSA_CMP_LEN, sel_start[None] + NSA_SEL_LEN)
                      - np.maximum(c_start, sel_start[None]), 0, None) / NSA_CMP_LEN
    overlap = jnp.asarray(overlap, jnp.float32)
    k_sel_blk = k_s.reshape(bsz, G, n_sel, NSA_SEL_LEN, Dh)
    v_sel_blk = v_s.reshape(bsz, G, n_sel, NSA_SEL_LEN, Dh)

    k_w_pad = jnp.pad(k_w, ((0, 0), (0, 0), (NSA_WINDOW, 0), (0, 0)))
    v_w_pad = jnp.pad(v_w, ((0, 0), (0, 0), (NSA_WINDOW, 0), (0, 0)))

    slopes = alibi_slopes(H).reshape(G, hpg)[:, :, None, None]
    nb = seq // Q_BLOCK
    q_blocks = jnp.moveaxis(q.reshape(bsz, G, hpg, nb, Q_BLOCK, Dh), 3, 0)
    gather = jax.vmap(jax.vmap(lambda blk, ix: blk[ix]))
    j_sel = jnp.arange(n_sel, dtype=jnp.int32)

    def block(inp):
        i, qb = inp
        q0 = i * Q_BLOCK
        t = q0 + jnp.arange(Q_BLOCK, dtype=jnp.int32)
        dist_c = t[:, None] - cmp_end[None, :]
        s_c = jnp.einsum('bghqd,bgkd->bghqk', qb, k_cmp).astype(jnp.float32) * scale
        s_c = s_c - slopes * dist_c.astype(jnp.float32)
        p_c = masked_softmax(s_c, dist_c >= 0)
        o_c = jnp.einsum('bghqk,bgkd->bghqd', p_c, v_cmp.astype(jnp.float32))
        imp = jnp.einsum('bghqk,kj->bgqj', p_c, overlap)
        cur = t // NSA_SEL_LEN
        forced = (j_sel[None] == 0) | (j_sel[None] == cur[:, None]) | (j_sel[None] == cur[:, None] - 1)
        valid = j_sel[None] * NSA_SEL_LEN <= t[:, None]
        imp = jnp.where(forced, FORCED_SCORE, jnp.where(valid, imp, -1.0))
        _, sel = lax.top_k(imp, top)
        ks = gather(k_sel_blk, sel).reshape(bsz, G, Q_BLOCK, top * NSA_SEL_LEN, Dh)
        vs = gather(v_sel_blk, sel).reshape(bsz, G, Q_BLOCK, top * NSA_SEL_LEN, Dh)
        pos_s = (sel[..., None] * NSA_SEL_LEN + jnp.arange(NSA_SEL_LEN, dtype=jnp.int32)).reshape(bsz, G, Q_BLOCK, top * NSA_SEL_LEN)
        dist_s = (t[None, None, :, None] - pos_s)[:, :, None]
        s_s = jnp.einsum('bghqd,bgqkd->bghqk', qb, ks).astype(jnp.float32) * scale
        s_s = s_s - slopes * dist_s.astype(jnp.float32)
        p_s = masked_softmax(s_s, dist_s >= 0)
        o_s = jnp.einsum('bghqk,bgqkd->bghqd', p_s, vs.astype(jnp.float32))
        kw = lax.dynamic_slice_in_dim(k_w_pad, q0, Q_BLOCK + NSA_WINDOW, axis=2)
        vw = lax.dynamic_slice_in_dim(v_w_pad, q0, Q_BLOCK + NSA_WINDOW, axis=2)
        s_pos = q0 - NSA_WINDOW + jnp.arange(Q_BLOCK + NSA_WINDOW, dtype=jnp.int32)
        dist_w = t[:, None] - s_pos[None, :]
        mask_w = (s_pos[None, :] >= 0) & (dist_w >= 0) & (dist_w < NSA_WINDOW)
        s_w = jnp.einsum('bghqd,bgkd->bghqk', qb, kw).astype(jnp.float32) * scale
        s_w = s_w - slopes * dist_w.astype(jnp.float32)
        p_w = masked_softmax(s_w, mask_w)
        o_w = jnp.einsum('bghqk,bgkd->bghqd', p_w, vw.astype(jnp.float32))
        return o_c, o_s, o_w

    o_c, o_s, o_w = lax.map(block, (jnp.arange(nb, dtype=jnp.int32), q_blocks))

    def unblock(o):
        return jnp.moveaxis(o, 0, 3).reshape(bsz, H, seq, Dh).transpose(0, 2, 1, 3)

    g = jax.nn.sigmoid(gate_logits.astype(jnp.float32)).reshape(bsz, seq, H, 3)
    o = g[..., 0:1] * unblock(o_c) + g[..., 1:2] * unblock(o_s) + g[..., 2:3] * unblock(o_w)
    return o.reshape(bsz, seq, NSA_WIDTH).astype(q.dtype)


def moe_ffn(x, router_w, router_b, w1, b1, w2, b2):
    bsz, seq, d = x.shape
    T = bsz * seq
    xf = x.reshape(T, d)
    logits = (xf @ router_w + router_b).astype(jnp.float32)
    top_val, top_idx = lax.top_k(logits, TOP_K)
    gate = jax.nn.softmax(top_val, axis=-1)
    n_assign = T * TOP_K
    flat_e = top_idx.reshape(-1)
    flat_tok = jnp.arange(n_assign, dtype=jnp.int32) // TOP_K
    order = jnp.argsort(flat_e)
    e_sorted = flat_e[order]
    counts = jnp.bincount(flat_e, length=N_EXPERTS)
    padded = (counts + MOE_BLOCK - 1) // MOE_BLOCK * MOE_BLOCK
    start = jnp.cumsum(counts) - counts
    pend = jnp.cumsum(padded)
    pstart = pend - padded
    dest = pstart[e_sorted] + (jnp.arange(n_assign, dtype=jnp.int32) - start[e_sorted])
    n_blocks = -(-(n_assign + N_EXPERTS * (MOE_BLOCK - 1)) // MOE_BLOCK)
    rows = n_blocks * MOE_BLOCK
    buf_tok = jnp.zeros((rows,), jnp.int32).at[dest].set(flat_tok[order])
    buf_gate = jnp.zeros((rows,), jnp.float32).at[dest].set(gate.reshape(-1)[order])
    b_start = jnp.arange(n_blocks, dtype=jnp.int32) * MOE_BLOCK
    block_e = jnp.minimum(jnp.sum(b_start[:, None] >= pend[None, :], axis=1), N_EXPERTS - 1)
    xs = xf[buf_tok].reshape(n_blocks, MOE_BLOCK, d)

    def expert_block(inp):
        xb, e = inp
        h = xb @ w1[e] + b1[e]
        glu = jnp.minimum(h[:, 0::2], SWIGLU_LIMIT)
        lin = jnp.clip(h[:, 1::2], -SWIGLU_LIMIT, SWIGLU_LIMIT)
        act = glu * jax.nn.sigmoid(SWIGLU_ALPHA * glu) * (lin + 1.0)
        return act @ w2[e] + b2[e]

    ys = lax.map(expert_block, (xs, block_e)).reshape(rows, d)
    y = jax.ops.segment_sum(ys.astype(jnp.float32) * buf_gate[:, None], buf_tok, num_segments=T)
    return y.reshape(bsz, seq, d).astype(x.dtype)


def setup_inputs(seed: int = 0) -> dict:
    key = jax.random.key(seed)
    ks = jax.random.split(key, 28)
    f32 = jnp.float32
    L = DEPTH

    def nrm(k, shape, scale):
        return jax.random.normal(k, shape, f32) * scale

    def gain(k, shape):
        return 1.0 + 0.02 * jax.random.normal(k, shape, f32)

    dt = jnp.exp(jax.random.uniform(ks[5], (L, SSD_N_HEADS), f32, math.log(1e-3), math.log(1e-1)))
    cmp_in = NSA_CMP_LEN * NSA_HEAD_DIM
    return {
        'x': nrm(ks[0], (BATCH, SEQ, D_MODEL), 1.0),
        'ln1_w': gain(ks[1], (L, D_MODEL)),
        'w_in': nrm(ks[2], (L, D_MODEL, IN_WIDTH), D_MODEL ** -0.5),
        'ssd_conv_w': nrm(ks[3], (L, SSD_CONV_WIDTH, SSD_CONV_DIM), SSD_CONV_WIDTH ** -0.5),
        'ssd_conv_b': nrm(ks[4], (L, SSD_CONV_DIM), 0.01),
        'ssd_dt_bias': dt + jnp.log(-jnp.expm1(-dt)),
        'ssd_a_log': jnp.log(jax.random.uniform(ks[6], (L, SSD_N_HEADS), f32, 1.0, 16.0)),
        'ssd_d': gain(ks[7], (L, SSD_N_HEADS)),
        'ssd_norm_w': gain(ks[8], (L, SSD_D_INNER)),
        'ssd_out_w': nrm(ks[9], (L, SSD_D_INNER, D_MODEL), SSD_D_INNER ** -0.5),
        'nsa_q_norm_w': gain(ks[10], (L, NSA_HEAD_DIM)),
        'nsa_k_norm_w': gain(ks[11], (L, NSA_HEAD_DIM)),
        'cmp_pos_k': nrm(ks[12], (L, NSA_CMP_LEN, NSA_HEAD_DIM), 0.1),
        'cmp_pos_v': nrm(ks[13], (L, NSA_CMP_LEN, NSA_HEAD_DIM), 0.1),
        'cmp_k_w1': nrm(ks[14], (L, cmp_in, NSA_CMP_HIDDEN), cmp_in ** -0.5),
        'cmp_k_w2': nrm(ks[15], (L, NSA_CMP_HIDDEN, NSA_HEAD_DIM), NSA_CMP_HIDDEN ** -0.5),
        'cmp_v_w1': nrm(ks[16], (L, cmp_in, NSA_CMP_HIDDEN), cmp_in ** -0.5),
        'cmp_v_w2': nrm(ks[17], (L, NSA_CMP_HIDDEN, NSA_HEAD_DIM), NSA_CMP_HIDDEN ** -0.5),
        'nsa_out_w': nrm(ks[18], (L, NSA_WIDTH, D_MODEL), NSA_WIDTH ** -0.5),
        'w_out': nrm(ks[19], (L, D_MODEL, D_MODEL), D_MODEL ** -0.5),
        'ln2_w': gain(ks[20], (L, D_MODEL)),
        'router_w': nrm(ks[21], (L, D_MODEL, N_EXPERTS), D_MODEL ** -0.5),
        'router_b': nrm(ks[22], (L, N_EXPERTS), 0.01),
        'exp_w1': nrm(ks[23], (L, N_EXPERTS, D_MODEL, 2 * D_EXPERT), D_MODEL ** -0.5),
        'exp_b1': nrm(ks[24], (L, N_EXPERTS, 2 * D_EXPERT), 0.01),
        'exp_w2': nrm(ks[25], (L, N_EXPERTS, D_EXPERT, D_MODEL), D_EXPERT ** -0.5),
        'exp_b2': nrm(ks[26], (L, N_EXPERTS, D_MODEL), 0.01),
    }


def reference(x, ln1_w, w_in, ssd_conv_w, ssd_conv_b, ssd_dt_bias, ssd_a_log, ssd_d, ssd_norm_w, ssd_out_w,
              nsa_q_norm_w, nsa_k_norm_w, cmp_pos_k, cmp_pos_v, cmp_k_w1, cmp_k_w2, cmp_v_w1, cmp_v_w2,
              nsa_out_w, w_out, ln2_w, router_w, router_b, exp_w1, exp_b1, exp_w2, exp_b2):
    for l in range(DEPTH):
        xn = rms_norm(x, ln1_w[l])
        proj = xn @ w_in[l]
        (z, xbc, dt_raw, q, k_c, v_c, k_s, v_s, k_w, v_w,
         nsa_gate_logits, merge_logits) = jnp.split(proj, IN_SPLITS, axis=-1)
        y_ssd = ssd_mixer(z, xbc, dt_raw, ssd_conv_w[l], ssd_conv_b[l], ssd_dt_bias[l], ssd_a_log[l],
                          ssd_d[l], ssd_norm_w[l]) @ ssd_out_w[l]
        y_nsa = nsa_mixer(q, k_c, v_c, k_s, v_s, k_w, v_w, nsa_gate_logits, nsa_q_norm_w[l], nsa_k_norm_w[l],
                          cmp_pos_k[l], cmp_pos_v[l], cmp_k_w1[l], cmp_k_w2[l], cmp_v_w1[l], cmp_v_w2[l]) @ nsa_out_w[l]
        g = jax.nn.sigmoid(merge_logits.astype(jnp.float32))
        g_ssd, g_nsa = jnp.split(g, 2, axis=-1)
        merged = (g_ssd * y_ssd.astype(jnp.float32) + g_nsa * y_nsa.astype(jnp.float32)).astype(x.dtype)
        x = x + merged @ w_out[l]
        x = x + moe_ffn(rms_norm(x, ln2_w[l]), router_w[l], router_b[l], exp_w1[l], exp_b1[l], exp_w2[l], exp_b2[l])
    return x
```

```python
import functools
import math

import numpy as np
import jax
import jax.numpy as jnp
from jax import lax
from jax.experimental import pallas as pl
from jax.experimental.pallas import tpu as pltpu

F32 = jnp.float32
BF16 = jnp.bfloat16
I32 = jnp.int32

D_MODEL = 1024
SSD_D_INNER = 2048
SSD_HEAD_DIM = 64
SSD_N_HEADS = 32
SSD_N_GROUPS = 4
SSD_D_STATE = 128
SSD_CHUNK = 256
SSD_HPG = SSD_N_HEADS // SSD_N_GROUPS
NSA_N_HEADS = 16
NSA_N_KV = 4
NSA_HPG = NSA_N_HEADS // NSA_N_KV
NSA_HEAD_DIM = 64
NSA_CMP_LEN = 32
NSA_CMP_STRIDE = 16
NSA_CMP_HIDDEN = 256
NSA_SEL_LEN = 64
NSA_SEL_TOP = 16
NSA_WINDOW = 512
Q_BLOCK = 128
FORCED_SCORE = 1.0e4
N_EXPERTS = 32
TOP_K = 4
SWIGLU_LIMIT = 7.0
SWIGLU_ALPHA = 1.702
EPS = 1e-6

NEG = -1.0e30
LOG2E = 1.4426950408889634
LANES = 128
K_AUG = 80
VMEM_LIMIT = 56 * 1024 * 1024

C_Z, C_XS, C_B, C_C, C_MS, C_MN, C_Q, C_KC, C_VC, C_KS, C_VS, C_KW, C_VW = (
    0, 2048, 4096, 4608, 5120, 6144, 7168, 8192, 8448, 8704, 8960, 9216, 9472)
MAIN_W = 9728
TAIL_W = 512
O_DT, O_Q, O_GATE, O_MERGE = 5120, 5152, 7712, 7760


def _cparams(sem):
    return pltpu.CompilerParams(dimension_semantics=sem, vmem_limit_bytes=VMEM_LIMIT)


def _dot(a, b):
    return jnp.dot(a, b, preferred_element_type=F32)


def _split_bf16(x):
    hi = x.astype(BF16)
    lo = (x - hi.astype(F32)).astype(BF16)
    return hi, lo


def _inproj_kernel(x_ref, lnw_ref, w_ref, wt_ref, main_ref, tail_ref, xn_ref):
    @pl.when(pl.program_id(1) == 0)
    def _():
        x = x_ref[...]
        ms = jnp.mean(x * x, axis=-1, keepdims=True)
        xn = (x * lax.rsqrt(ms + EPS) * lnw_ref[...]).astype(BF16)
        xn_ref[...] = xn
        tail_ref[...] = _dot(xn, wt_ref[...])

    main_ref[...] = _dot(xn_ref[...], w_ref[...]).astype(BF16)


def _inproj(x2, ln_w, w_main, w_tail, tm, tn):
    T = x2.shape[0]
    return pl.pallas_call(
        _inproj_kernel,
        grid=(T // tm, MAIN_W // tn),
        in_specs=[pl.BlockSpec((tm, D_MODEL), lambda i, j: (i, 0)),
                  pl.BlockSpec((1, D_MODEL), lambda i, j: (0, 0)),
                  pl.BlockSpec((D_MODEL, tn), lambda i, j: (0, j)),
                  pl.BlockSpec((D_MODEL, TAIL_W), lambda i, j: (0, 0))],
        out_specs=[pl.BlockSpec((tm, tn), lambda i, j: (i, j)),
                   pl.BlockSpec((tm, TAIL_W), lambda i, j: (i, 0))],
        out_shape=[jax.ShapeDtypeStruct((T, MAIN_W), BF16),
                   jax.ShapeDtypeStruct((T, TAIL_W), F32)],
        scratch_shapes=[pltpu.VMEM((tm, D_MODEL), BF16)],
        compiler_params=_cparams(("arbitrary", "arbitrary")),
        name="inproj",
    )(x2, ln_w, w_main, w_tail)


def _softplus(x):
    return jnp.maximum(x, 0.0) + jnp.log1p(jnp.exp(-jnp.abs(x)))


def _ssd_kernel(z_ref, xs_ref, b_ref, c_ref, dt_ref, cwx_ref, cwb_ref, cwc_ref, cbx_ref, cbb_ref, cbc_ref,
                dtb_ref, alog_ref, dexp_ref, nw_ref, y_ref, extx, extb, extc, hs_ref):
    L = xs_ref.shape[0]
    P = SSD_HEAD_DIM

    @pl.when(pl.program_id(2) == 0)
    def _():
        extx[0:8, :] = jnp.zeros((8, extx.shape[1]), F32)
        extb[0:8, :] = jnp.zeros((8, extb.shape[1]), F32)
        extc[0:8, :] = jnp.zeros((8, extc.shape[1]), F32)
        hs_ref[...] = jnp.zeros(hs_ref.shape, F32)

    def conv_act(x_ref, w_ref, bias_ref, ext):
        xf = x_ref[...].astype(F32)
        ext[8:8 + L, :] = xf
        acc = bias_ref[...] + w_ref[3:4, :] * xf
        for k in (1, 2, 3):
            acc = acc + w_ref[3 - k:4 - k, :] * ext[8 - k:8 - k + L, :]
        ext[0:8, :] = ext[L:L + 8, :]
        return acc * jax.nn.sigmoid(acc)

    xs = conv_act(xs_ref, cwx_ref, cbx_ref, extx)
    bm = conv_act(b_ref, cwb_ref, cbb_ref, extb)
    cm = conv_act(c_ref, cwc_ref, cbc_ref, extc)

    dt = _softplus(dt_ref[...] + dtb_ref[...])
    da = dt * (-jnp.exp(alog_ref[...]))
    row = lax.broadcasted_iota(I32, (L, L), 0)
    col = lax.broadcasted_iota(I32, (L, L), 1)
    tril = row >= col
    trif = jnp.where(tril, 1.0, 0.0).astype(BF16)
    da_hi, da_lo = _split_bf16(da)
    acum = _dot(trif, da_hi) + _dot(trif, da_lo)
    acum_t = acum.T
    a_end = acum[L - 1:L, :]
    ea = jnp.exp(acum)
    wend = jnp.exp(a_end - acum) * dt
    eend = jnp.exp(a_end)

    cb16 = cm.astype(BF16)
    bt16 = bm.T.astype(BF16)
    cb = _dot(cb16, bt16)

    ys = []
    for hh in range(SSD_HPG):
        seg = acum[:, hh:hh + 1] - acum_t[hh:hh + 1, :]
        decay = jnp.exp(jnp.where(tril, seg, NEG))
        g = (cb * decay).astype(BF16)
        xh = xs[:, hh * P:(hh + 1) * P]
        xdt = (xh * dt[:, hh:hh + 1]).astype(BF16)
        hprev = hs_ref[hh]
        y = _dot(g, xdt) + _dot(cb16, hprev.astype(BF16)) * ea[:, hh:hh + 1]
        xw = (xh * wend[:, hh:hh + 1]).astype(BF16)
        hs_ref[hh] = hprev * eend[:, hh:hh + 1] + _dot(bt16, xw)
        ys.append(y)
    y = jnp.concatenate(ys, axis=1)
    y = y + dexp_ref[...] * xs
    z = z_ref[...].astype(F32)
    y = y * (z * jax.nn.sigmoid(z))
    ms = jnp.mean(y * y, axis=-1, keepdims=True)
    y_ref[...] = (y * lax.rsqrt(ms + EPS) * nw_ref[...]).astype(BF16)


def _ssd(main3, tail3, conv_w, conv_b, dtb, alog, dexp, norm_w):
    B, S, _ = main3.shape
    L = math.gcd(S, SSD_CHUNK)
    nc = S // L
    G, N, GW = SSD_N_GROUPS, SSD_D_STATE, SSD_D_INNER // SSD_N_GROUPS
    xs0, b0, c0 = C_XS // GW, C_B // N, C_C // N
    cb0, cc0 = SSD_D_INNER // N, (SSD_D_INNER + G * N) // N

    def seq(w, off):
        return pl.BlockSpec((None, L, w), lambda b, g, c: (b, c, off + g))

    def par(r, w, off):
        return pl.BlockSpec((r, w), lambda b, g, c: (0, off + g))

    return pl.pallas_call(
        _ssd_kernel,
        grid=(B, G, nc),
        in_specs=[seq(GW, 0), seq(GW, xs0), seq(N, b0), seq(N, c0), seq(LANES, 0),
                  par(4, GW, 0), par(4, N, cb0), par(4, N, cc0),
                  par(1, GW, 0), par(1, N, cb0), par(1, N, cc0),
                  par(1, LANES, 0), par(1, LANES, 0), par(1, GW, 0), par(1, GW, 0)],
        out_specs=pl.BlockSpec((None, L, GW), lambda b, g, c: (b, c, g)),
        out_shape=jax.ShapeDtypeStruct((B, S, SSD_D_INNER), BF16),
        scratch_shapes=[pltpu.VMEM((L + 8, GW), F32), pltpu.VMEM((L + 8, N), F32), pltpu.VMEM((L + 8, N), F32),
                        pltpu.VMEM((SSD_HPG, N, SSD_HEAD_DIM), F32)],
        compiler_params=_cparams(("arbitrary", "arbitrary", "arbitrary")),
        name="ssd",
    )(main3, main3, main3, main3, tail3, conv_w, conv_w, conv_w, conv_b, conv_b, conv_b,
      dtb, alog, dexp, norm_w)


def _nsa_norm_kernel(q_ref, ks_ref, kw_ref, bdq_ref, bdk_ref, wq_ref, wk_ref, qo_ref, kso_ref, kwo_ref):
    def head_norm(x_ref, bd_ref, w_ref):
        x = x_ref[...].astype(F32)
        hi, lo = _split_bf16(x * x)
        ms = (_dot(hi, bd_ref[...]) + _dot(lo, bd_ref[...])) * (1.0 / NSA_HEAD_DIM)
        return (x * lax.rsqrt(ms + EPS) * w_ref[...]).astype(BF16)

    qo_ref[...] = head_norm(q_ref, bdq_ref, wq_ref)
    kso_ref[...] = head_norm(ks_ref, bdk_ref, wk_ref)
    kwo_ref[...] = head_norm(kw_ref, bdk_ref, wk_ref)


def _nsa_norm(main, bdq, bdk, wq, wk, tm):
    T = main.shape[0]
    QW, KW = NSA_N_HEADS * NSA_HEAD_DIM, NSA_N_KV * NSA_HEAD_DIM
    const = lambda shape: pl.BlockSpec(shape, lambda i: (0, 0))
    return pl.pallas_call(
        _nsa_norm_kernel,
        grid=(T // tm,),
        in_specs=[pl.BlockSpec((tm, QW), lambda i: (i, C_Q // QW)),
                  pl.BlockSpec((tm, KW), lambda i: (i, C_KS // KW)),
                  pl.BlockSpec((tm, KW), lambda i: (i, C_KW // KW)),
                  const((QW, QW)), const((KW, KW)), const((1, QW)), const((1, KW))],
        out_specs=[pl.BlockSpec((tm, QW), lambda i: (i, 0)),
                   pl.BlockSpec((tm, KW), lambda i: (i, 0)),
                   pl.BlockSpec((tm, KW), lambda i: (i, 0))],
        out_shape=[jax.ShapeDtypeStruct((T, QW), BF16), jax.ShapeDtypeStruct((T, KW), BF16),
                   jax.ShapeDtypeStruct((T, KW), BF16)],
        compiler_params=_cparams(("arbitrary",)),
        name="nsa_norm",
    )(main, main, main, bdq, bdk, wq, wk)


def _compress_kernel(u_ref, pos_ref, w1_ref, w2_ref, nw_ref, o_ref, *, normalize):
    half = u_ref.shape[1]
    nrow = u_ref.shape[0]
    u = u_ref[...]
    a = _dot(u, w1_ref[0:half, :])
    b = _dot(u, w1_ref[half:2 * half, :])
    posc = _dot(pos_ref[...], w1_ref[...])[0:1, :]
    pre = a + pltpu.roll(b, nrow - 1, 0) + posc
    act = 0.5 * pre * (1.0 + jnp.tanh(math.sqrt(2.0 / math.pi) * (pre + 0.044715 * (pre * pre * pre))))
    o = _dot(act.astype(BF16), w2_ref[...])
    if normalize:
        ms = jnp.mean(o * o, axis=-1, keepdims=True)
        o = o * lax.rsqrt(ms + EPS) * nw_ref[...]
    o_ref[...] = o


def _compress(u, pos8, w1, w2, nw, normalize):
    BG, nrow, half = u.shape
    const = lambda shape: pl.BlockSpec(shape, lambda i: (0, 0))
    return pl.pallas_call(
        functools.partial(_compress_kernel, normalize=normalize),
        grid=(BG,),
        in_specs=[pl.BlockSpec((None, nrow, half), lambda i: (i, 0, 0)),
                  const(pos8.shape), const(w1.shape), const(w2.shape), const(nw.shape)],
        out_specs=pl.BlockSpec((None, nrow, NSA_HEAD_DIM), lambda i: (i, 0, 0)),
        out_shape=jax.ShapeDtypeStruct((BG, nrow, NSA_HEAD_DIM), F32),
        compiler_params=_cparams(("arbitrary",)),
        name="nsa_compress_norm" if normalize else "nsa_compress",
    )(u, pos8, w1, w2, nw)


def _nsa_cmp_kernel(qt_ref, kc_ref, vct_ref, ovt_ref, pm_ref, glt_ref, oct_ref, sb_ref, lst_ref):
    i = pl.program_id(2)
    ncmp = kc_ref.shape[0]
    nsel = ovt_ref.shape[0]
    QB = Q_BLOCK
    W = NSA_HPG * QB

    s = _dot(kc_ref[...], qt_ref[...])
    jrow = lax.broadcasted_iota(I32, (ncmp, W), 0)
    lane = lax.broadcasted_iota(I32, (ncmp, W), 1)
    t = i * QB + (lane & (QB - 1))
    mask = (NSA_CMP_STRIDE * jrow + (NSA_CMP_LEN - 1)) <= t
    sm = jnp.where(mask, s, NEG)
    m = jnp.max(sm, axis=0, keepdims=True)
    p = jnp.where(mask, jnp.exp2(sm - m), 0.0)
    l = jnp.sum(p, axis=0, keepdims=True)
    pn = p * (1.0 / jnp.maximum(l, 1e-30))
    g0 = jax.nn.sigmoid(glt_ref[0:1, :])
    oct_ref[...] = _dot(vct_ref[...], pn.astype(BF16)) * g0

    psum = pn[:, 0:QB]
    for h in range(1, NSA_HPG):
        psum = psum + pn[:, h * QB:(h + 1) * QB]
    imp = _dot(ovt_ref[...], psum.astype(BF16))

    jf = lax.broadcasted_iota(I32, (nsel, QB), 0).astype(F32)
    tq = (i * QB + lax.broadcasted_iota(I32, (nsel, QB), 1))
    cur = (tq >> 6).astype(F32)
    forced = (jf == 0.0) | (jf == cur) | (jf == cur - 1.0)
    valid = jf <= cur
    v0 = jnp.where(forced, FORCED_SCORE, jnp.where(valid, imp, -1.0))

    def pick_one(_, carry):
        v, sel = carry
        mx = jnp.max(v, axis=0, keepdims=True)
        first = jnp.min(jnp.where(v == mx, jf, float(nsel)), axis=0, keepdims=True)
        pick = jf == first
        return jnp.where(pick, -2.0, v), jnp.where(pick, 1.0, sel)

    _, sel = lax.fori_loop(0, min(NSA_SEL_TOP, nsel), pick_one, (v0, jnp.zeros((nsel, QB), F32)))
    sel = jnp.where(valid, sel, 0.0)
    sb_ref[...] = jnp.where(sel > 0.0, 0.0, NEG)

    npair = nsel // 2
    pairsel = _dot(pm_ref[...], sel.astype(BF16))
    jp = lax.broadcasted_iota(I32, (npair, LANES), 0)
    need = (jnp.sum(pairsel, axis=1, keepdims=True) > 0.0) & (jp < i)
    needf = jnp.where(need, 1.0, 0.0)
    r2 = lax.broadcasted_iota(I32, (npair, npair), 0)
    c2 = lax.broadcasted_iota(I32, (npair, npair), 1)
    tri = jnp.where(c2 <= r2, 1.0, 0.0).astype(BF16)
    prefix = _dot(tri, needf.astype(BF16))
    slot = lax.broadcasted_iota(I32, (npair, LANES), 1).astype(F32)
    onehot = jnp.where(need & (prefix == slot + 1.0), 1.0, 0.0).astype(BF16)
    jpv = lax.broadcasted_iota(I32, (8, npair), 1).astype(F32).astype(BF16)
    lst = _dot(jpv, onehot)
    cnt = _dot(jnp.ones((8, npair), BF16), needf.astype(BF16))
    r8 = lax.broadcasted_iota(I32, (8, LANES), 0)
    lst_ref[...] = jnp.where(r8 == 0, lst, cnt).astype(I32)


def _nsa_cmp(qt, kc, vct, ovt, pm, glt):
    B, G, nb, _, W = qt.shape
    ncmp = kc.shape[2]
    nsel = ovt.shape[0]
    blk = lambda *shape: pl.BlockSpec((None, None, None) + shape, lambda b, g, i: (b, g, i) + (0,) * len(shape))
    per_bg = lambda *shape: pl.BlockSpec((None, None) + shape, lambda b, g, i: (b, g) + (0,) * len(shape))
    const = lambda shape: pl.BlockSpec(shape, lambda b, g, i: (0,) * len(shape))
    return pl.pallas_call(
        _nsa_cmp_kernel,
        grid=(B, G, nb),
        in_specs=[blk(K_AUG, W), per_bg(ncmp, K_AUG), per_bg(NSA_HEAD_DIM, ncmp),
                  const(ovt.shape), const(pm.shape), blk(8, W)],
        out_specs=[blk(NSA_HEAD_DIM, W), blk(nsel, Q_BLOCK), blk(8, LANES)],
        out_shape=[jax.ShapeDtypeStruct((B, G, nb, NSA_HEAD_DIM, W), F32),
                   jax.ShapeDtypeStruct((B, G, nb, nsel, Q_BLOCK), F32),
                   jax.ShapeDtypeStruct((B, G, nb, 8, LANES), I32)],
        compiler_params=_cparams(("arbitrary", "arbitrary", "arbitrary")),
        name="nsa_cmp_select",
    )(qt, kc, vct, ovt, pm, glt)


def _tile4(r):
    return jnp.concatenate([r] * NSA_HPG, axis=1)


def _nsa_sel_kernel(lst_ref, qt_ref, ks_ref, vst_ref, kw_ref, vwt_ref, sb_ref, glt_ref, oct_ref, o_ref):
    i = pl.program_id(2)
    QB = Q_BLOCK
    W = NSA_HPG * QB
    HB = NSA_SEL_LEN
    qt = qt_ref[...]
    count = lst_ref[1, 0]

    def flash(s, vt, carry):
        m, l, acc = carry
        mn = jnp.maximum(m, jnp.max(s, axis=0, keepdims=True))
        alpha = jnp.exp2(m - mn)
        p = jnp.exp2(s - mn)
        l = alpha * l + jnp.sum(p, axis=0, keepdims=True)
        acc = alpha * acc + _dot(vt, p.astype(BF16))
        return mn, l, acc

    def pair_scores(jp, live):
        k = ks_ref[pl.ds(pl.multiple_of(jp * QB, QB), QB), :]
        s = _dot(k, qt)
        b0 = _tile4(sb_ref[2 * jp])
        b1 = _tile4(sb_ref[2 * jp + 1])
        if live is not None:
            b0 = jnp.where(live, b0, NEG)
            b1 = jnp.where(live, b1, NEG)
        return jnp.concatenate([s[0:HB] + b0, s[HB:2 * HB] + b1], axis=0)

    def body(it, carry):
        k0 = 2 * it
        jp0 = lst_ref[0, k0]
        live1 = (k0 + 1) < count
        jp1 = jnp.where(live1, lst_ref[0, k0 + 1], jp0)
        s = jnp.concatenate([pair_scores(jp0, None), pair_scores(jp1, live1)], axis=0)
        vt = jnp.concatenate([vst_ref[jp0], vst_ref[jp1]], axis=1)
        return flash(s, vt, carry)

    init = (jnp.full((1, W), NEG, F32), jnp.zeros((1, W), F32), jnp.zeros((NSA_HEAD_DIM, W), F32))
    carry = lax.fori_loop(0, (count + 1) // 2, body, init)

    r = lax.broadcasted_iota(I32, (QB, W), 0)
    c = lax.broadcasted_iota(I32, (QB, W), 1) & (QB - 1)
    kd = ks_ref[pl.ds(pl.multiple_of(i * QB, QB), QB), :]
    sd = jnp.where(r <= c, _dot(kd, qt), NEG)
    m, l, acc = flash(sd, vst_ref[i], carry)
    o_sel = acc * (1.0 / l)

    ss, vts = [], []
    for w in range(NSA_WINDOW // QB + 1):
        pw = i - NSA_WINDOW // QB + w
        pc = jnp.maximum(pw, 0)
        kk = kw_ref[pl.ds(pl.multiple_of(pc * QB, QB), QB), :]
        s = _dot(kk, qt)
        if w == 0:
            s = jnp.where((r > c) & (pw >= 0), s, NEG)
        elif w == NSA_WINDOW // QB:
            s = jnp.where(r <= c, s, NEG)
        else:
            s = jnp.where(pw >= 0, s, NEG)
        ss.append(s)
        vts.append(vwt_ref[pc])
    s = jnp.concatenate(ss, axis=0)
    mw = jnp.max(s, axis=0, keepdims=True)
    p = jnp.exp2(s - mw)
    lw = jnp.sum(p, axis=0, keepdims=True)
    o_win = _dot(jnp.concatenate(vts, axis=1), p.astype(BF16)) * (1.0 / lw)

    g = jax.nn.sigmoid(glt_ref[...])
    o_ref[...] = (oct_ref[...] + g[1:2, :] * o_sel + g[2:3, :] * o_win).astype(BF16)


def _nsa_sel(lst, qt, ks, vst, kw, vwt, sb, glt, oct):
    B, G, nb, _, W = qt.shape
    S = ks.shape[2]
    npair = vst.shape[2]
    nsel = sb.shape[3]
    blk = lambda *shape: pl.BlockSpec((None, None, None) + shape, lambda b, g, i: (b, g, i) + (0,) * len(shape))
    per_bg = lambda *shape: pl.BlockSpec((None, None) + shape, lambda b, g, i: (b, g) + (0,) * len(shape))
    return pl.pallas_call(
        _nsa_sel_kernel,
        grid=(B, G, nb),
        in_specs=[pl.BlockSpec((None, None, None, 8, LANES), lambda b, g, i: (b, g, i, 0, 0),
                               memory_space=pltpu.SMEM),
                  blk(K_AUG, W), per_bg(S, K_AUG), per_bg(npair, NSA_HEAD_DIM, Q_BLOCK),
                  per_bg(S, K_AUG), per_bg(npair, NSA_HEAD_DIM, Q_BLOCK),
                  blk(nsel, 1, Q_BLOCK), blk(8, W), blk(NSA_HEAD_DIM, W)],
        out_specs=blk(NSA_HEAD_DIM, W),
        out_shape=jax.ShapeDtypeStruct((B, G, nb, NSA_HEAD_DIM, W), BF16),
        compiler_params=_cparams(("arbitrary", "arbitrary", "arbitrary")),
        name="nsa_select_window",
    )(lst, qt, ks, vst, kw, vwt, sb, glt, oct)


def _merge_kernel(x_ref, ys_ref, on_ref, gs_ref, gn_ref, wssd_ref, wnsa_ref, wout_ref, ln2_ref,
                  rwh_ref, rwl_ref, rb_ref, x1_ref, xn2_ref, route_ref, cnt_ref, base_ref):
    tm = x_ref.shape[0]

    @pl.when(pl.program_id(0) == 0)
    def _():
        base_ref[...] = jnp.zeros(base_ref.shape, F32)

    y_ssd = _dot(ys_ref[...], wssd_ref[...])
    y_nsa = _dot(on_ref[...], wnsa_ref[...])
    merged = (jax.nn.sigmoid(gs_ref[...].astype(F32)) * y_ssd
              + jax.nn.sigmoid(gn_ref[...].astype(F32)) * y_nsa)
    x1 = x_ref[...] + _dot(merged.astype(BF16), wout_ref[...])
    x1_ref[...] = x1
    ms = jnp.mean(x1 * x1, axis=-1, keepdims=True)
    xn2 = x1 * lax.rsqrt(ms + EPS) * ln2_ref[...]
    xn2_ref[...] = xn2

    xh, xl = _split_bf16(xn2)
    logits = _dot(xh, rwh_ref[...]) + _dot(xl, rwh_ref[...]) + _dot(xh, rwl_ref[...]) + rb_ref[...]
    lane = lax.broadcasted_iota(I32, (tm, LANES), 1)
    lanef = lane.astype(F32)
    v = logits
    onehot = jnp.zeros((tm, LANES), F32)
    vals, picks = [], []
    for _ in range(TOP_K):
        mx = jnp.max(v, axis=-1, keepdims=True)
        first = jnp.min(jnp.where(v == mx, lanef, float(LANES)), axis=-1, keepdims=True)
        pick = lanef == first
        v = jnp.where(pick, 2.0 * NEG, v)
        onehot = jnp.where(pick, 1.0, onehot)
        vals.append(mx)
        picks.append((pick, first))
    es = [jnp.exp(val - vals[0]) for val in vals]
    inv = 1.0 / (es[0] + es[1] + es[2] + es[3])

    r2 = lax.broadcasted_iota(I32, (tm, tm), 0)
    c2 = lax.broadcasted_iota(I32, (tm, tm), 1)
    stril = jnp.where(c2 < r2, 1.0, 0.0).astype(BF16)
    posmap = base_ref[...] + _dot(stril, onehot.astype(BF16))
    route = jnp.zeros((tm, LANES), F32)
    for k in range(TOP_K):
        pick, first = picks[k]
        pos = jnp.sum(jnp.where(pick, posmap, 0.0), axis=-1, keepdims=True)
        route = jnp.where(lane == k, first, route)
        route = jnp.where(lane == TOP_K + k, pos, route)
        route = jnp.where(lane == 2 * TOP_K + k, es[k] * inv, route)
    route_ref[...] = route
    base = base_ref[...] + jnp.sum(onehot, axis=0, keepdims=True)
    base_ref[...] = base
    cnt_ref[...] = jnp.broadcast_to(base, cnt_ref.shape)


def _merge(x2, yssd, onsa, main, wssd, wnsa, wout, ln2, rwh, rwl, rb, tm):
    T = x2.shape[0]
    const = lambda shape: pl.BlockSpec(shape, lambda i: (0, 0))
    row = lambda w, j=0: pl.BlockSpec((tm, w), lambda i: (i, j))
    return pl.pallas_call(
        _merge_kernel,
        grid=(T // tm,),
        in_specs=[row(D_MODEL), row(SSD_D_INNER), row(D_MODEL), row(D_MODEL, C_MS // D_MODEL),
                  row(D_MODEL, C_MN // D_MODEL),
                  const(wssd.shape), const(wnsa.shape), const(wout.shape), const(ln2.shape),
                  const(rwh.shape), const(rwl.shape), const(rb.shape)],
        out_specs=[row(D_MODEL), row(D_MODEL), row(LANES), const((8, LANES))],
        out_shape=[jax.ShapeDtypeStruct((T, D_MODEL), F32), jax.ShapeDtypeStruct((T, D_MODEL), F32),
                   jax.ShapeDtypeStruct((T, LANES), F32), jax.ShapeDtypeStruct((8, LANES), F32)],
        scratch_shapes=[pltpu.VMEM((1, LANES), F32)],
        compiler_params=_cparams(("arbitrary",)),
        name="merge_router",
    )(x2, yssd, onsa, main, main, wssd, wnsa, wout, ln2, rwh, rwl, rb)


def _expert_kernel(be_ref, nu_ref, tok_ref, xn_hbm, w1g_ref, w1l_ref, b1g_ref, b1l_ref, w2_ref, b2_ref,
                   ys_ref, xbuf, sem):
    bm = xbuf.shape[0]
    blk = pl.program_id(0)

    def row_copy(r, tok):
        return pltpu.make_async_copy(xn_hbm.at[pl.ds(tok, 1)], xbuf.at[pl.ds(r, 1)], sem)

    @pl.when(blk < nu_ref[0])
    def _():
        def issue(r, c):
            row_copy(r, tok_ref[0, r]).start()
            return c

        lax.fori_loop(0, bm, issue, 0)

        def drain(r, c):
            row_copy(r, 0).wait()
            return c

        lax.fori_loop(0, bm, drain, 0)
        x = xbuf[...].astype(BF16)
        glu = jnp.minimum(_dot(x, w1g_ref[...]) + b1g_ref[...], SWIGLU_LIMIT)
        lin = jnp.clip(_dot(x, w1l_ref[...]) + b1l_ref[...], -SWIGLU_LIMIT, SWIGLU_LIMIT)
        act = glu * jax.nn.sigmoid(SWIGLU_ALPHA * glu) * (lin + 1.0)
        ys_ref[...] = _dot(act.astype(BF16), w2_ref[...]) + b2_ref[...]

    @pl.when(blk >= nu_ref[0])
    def _():
        ys_ref[...] = jnp.zeros(ys_ref.shape, F32)


def _experts(block_e, n_used, buf_tok3, xn2, w1g, w1l, b1g, b1l, w2, b2, bm):
    n_blocks = buf_tok3.shape[0]
    DE = w2.shape[1]
    wspec = lambda r, c: pl.BlockSpec((None, r, c), lambda b, be, nu: (be[b], 0, 0))
    return pl.pallas_call(
        _expert_kernel,
        grid_spec=pltpu.PrefetchScalarGridSpec(
            num_scalar_prefetch=2,
            grid=(n_blocks,),
            in_specs=[pl.BlockSpec((None, 1, bm), lambda b, be, nu: (b, 0, 0), memory_space=pltpu.SMEM),
                      pl.BlockSpec(memory_space=pl.ANY),
                      wspec(D_MODEL, DE), wspec(D_MODEL, DE), wspec(1, DE), wspec(1, DE),
                      wspec(DE, D_MODEL), wspec(1, D_MODEL)],
            out_specs=pl.BlockSpec((bm, D_MODEL), lambda b, be, nu: (b, 0)),
            scratch_shapes=[pltpu.VMEM((bm, D_MODEL), F32), pltpu.SemaphoreType.DMA(())]),
        out_shape=jax.ShapeDtypeStruct((n_blocks * bm, D_MODEL), F32),
        compiler_params=_cparams(("arbitrary",)),
        name="moe_experts",
    )(block_e, n_used, buf_tok3, xn2, w1g, w1l, b1g, b1l, w2, b2)


def _combine_kernel(dest_ref, route_ref, x1_ref, ys_hbm, o_ref, buf, sem):
    tc = x1_ref.shape[0]
    shift = tc.bit_length() - 1

    def row_copy(r, d):
        return pltpu.make_async_copy(ys_hbm.at[pl.ds(d, 1)], buf.at[r >> shift, pl.ds(r & (tc - 1), 1)], sem)

    def issue(r, c):
        row_copy(r, dest_ref[0, r]).start()
        return c

    lax.fori_loop(0, TOP_K * tc, issue, 0)

    def drain(r, c):
        row_copy(r, 0).wait()
        return c

    lax.fori_loop(0, TOP_K * tc, drain, 0)
    route = route_ref[...]
    acc = x1_ref[...]
    for k in range(TOP_K):
        acc = acc + route[:, 2 * TOP_K + k:2 * TOP_K + k + 1] * buf[k]
    o_ref[...] = acc


def _combine(dest3, route, x1, ys, tc):
    T = x1.shape[0]
    return pl.pallas_call(
        _combine_kernel,
        grid=(T // tc,),
        in_specs=[pl.BlockSpec((None, 1, TOP_K * tc), lambda i: (i, 0, 0), memory_space=pltpu.SMEM),
                  pl.BlockSpec((tc, LANES), lambda i: (i, 0)),
                  pl.BlockSpec((tc, D_MODEL), lambda i: (i, 0)),
                  pl.BlockSpec(memory_space=pl.ANY)],
        out_specs=pl.BlockSpec((tc, D_MODEL), lambda i: (i, 0)),
        out_shape=jax.ShapeDtypeStruct((T, D_MODEL), F32),
        scratch_shapes=[pltpu.VMEM((TOP_K, tc, D_MODEL), F32), pltpu.SemaphoreType.DMA(())],
        compiler_params=_cparams(("arbitrary",)),
        name="moe_combine",
    )(dest3, route, x1, ys)


def _split_pos(pos):
    lo = pos % 256
    return lo.astype(np.float32), (pos - lo).astype(np.float32)


def _aug_keys(k, pos):
    lo, hi = _split_pos(pos)
    cols = np.stack([lo, hi, lo, hi] + [np.zeros_like(lo)] * (K_AUG - NSA_HEAD_DIM - 4), axis=-1)
    cols = jnp.broadcast_to(jnp.asarray(cols, BF16), k.shape[:-1] + (cols.shape[-1],))
    return jnp.concatenate([k, cols], axis=-1)


def _slope_rows():
    slopes = (2.0 ** (-8.0 * np.arange(1, NSA_N_HEADS + 1) / NSA_N_HEADS)).astype(np.float32)
    sl = jnp.asarray(np.repeat((slopes * np.float32(LOG2E)).reshape(NSA_N_KV, NSA_HPG), Q_BLOCK, axis=1), F32)
    hi, lo = _split_bf16(sl)
    zero = jnp.zeros_like(hi)
    return jnp.stack([hi, hi, lo, lo] + [zero] * (K_AUG - NSA_HEAD_DIM - 4), axis=1)


def _nsa(main3, tail3, q_norm_w, k_norm_w, cmp_pos_k, cmp_pos_v, cmp_k_w1, cmp_k_w2, cmp_v_w1, cmp_v_w2):
    B, S, _ = main3.shape
    T = B * S
    G, HPG, DH, QB = NSA_N_KV, NSA_HPG, NSA_HEAD_DIM, Q_BLOCK
    nb, npair, nsel, ncmp = S // QB, S // QB, S // NSA_SEL_LEN, S // NSA_CMP_STRIDE
    main = main3.reshape(T, MAIN_W)

    hd = np.arange(NSA_N_HEADS * DH) // DH
    bdq = jnp.asarray(hd[:, None] == hd[None, :], BF16)
    bdk = bdq[:G * DH, :G * DH]
    wq = (jnp.tile(q_norm_w.astype(F32), NSA_N_HEADS) * (DH ** -0.5 * LOG2E)).reshape(1, -1)
    wk = jnp.tile(k_norm_w.astype(F32), G).reshape(1, -1)
    qn, ksn, kwn = _nsa_norm(main, bdq, bdk, wq, wk, tm=min(512, T))

    def kv_heads(t):
        return t.reshape(B, S, G, DH).transpose(0, 2, 1, 3)

    def v_pairs_t(v):
        return v.reshape(B, G, npair, QB, DH).transpose(0, 1, 2, 4, 3)

    qt = qn.reshape(B, nb, QB, G, HPG, DH).transpose(0, 3, 1, 5, 4, 2).reshape(B, G, nb, DH, HPG * QB)
    srow = jnp.broadcast_to(_slope_rows()[None, :, None], (B, G, nb, K_AUG - DH, HPG * QB))
    qt = jnp.concatenate([qt, srow], axis=3)

    tok_pos = np.arange(S)
    ks = _aug_keys(kv_heads(ksn), tok_pos)
    kw = _aug_keys(kv_heads(kwn), tok_pos)
    vst = v_pairs_t(kv_heads(main[:, C_VS:C_VS + G * DH]))
    vwt = v_pairs_t(kv_heads(main[:, C_VW:C_VW + G * DH]))

    def halves(c0):
        return kv_heads(main[:, c0:c0 + G * DH]).reshape(B * G, ncmp, NSA_CMP_STRIDE * DH)

    def pos8(pos):
        return jnp.zeros((8, NSA_CMP_LEN * DH), F32).at[0].set(pos.reshape(-1)).astype(BF16)

    ones = jnp.ones((1, DH), F32)
    kc = _compress(halves(C_KC), pos8(cmp_pos_k), cmp_k_w1.astype(BF16), cmp_k_w2.astype(BF16),
                   k_norm_w.astype(F32).reshape(1, DH), True)
    vc = _compress(halves(C_VC), pos8(cmp_pos_v), cmp_v_w1.astype(BF16), cmp_v_w2.astype(BF16), ones, False)
    cmp_end = np.arange(ncmp) * NSA_CMP_STRIDE + NSA_CMP_LEN - 1
    kc = _aug_keys(kc.astype(BF16).reshape(B, G, ncmp, DH), cmp_end)
    vct = vc.astype(BF16).reshape(B, G, ncmp, DH).transpose(0, 1, 3, 2)

    c_start = (np.arange(ncmp) * NSA_CMP_STRIDE)[:, None]
    s_start = (np.arange(nsel) * NSA_SEL_LEN)[None, :]
    overlap = np.clip(np.minimum(c_start + NSA_CMP_LEN, s_start + NSA_SEL_LEN)
                      - np.maximum(c_start, s_start), 0, None) / NSA_CMP_LEN
    overlap[(S - NSA_CMP_LEN) // NSA_CMP_STRIDE + 1:] = 0.0
    ovt = jnp.asarray(overlap.T, BF16)
    pm = jnp.asarray(np.arange(nsel)[None, :] // 2 == np.arange(nsel // 2)[:, None], BF16)

    gl = tail3.reshape(B, nb, QB, G, LANES)[..., 8:8 + 3 * HPG].reshape(B, nb, QB, G, HPG, 3)
    glt = gl.transpose(0, 3, 1, 5, 4, 2).reshape(B, G, nb, 3, HPG * QB)
    glt = jnp.concatenate([glt, jnp.zeros((B, G, nb, 5, HPG * QB), F32)], axis=3)

    oct, sb, lst = _nsa_cmp(qt, kc, vct, ovt, pm, glt)
    sb = sb.reshape(B, G, nb, nsel, 1, QB)
    ot = _nsa_sel(lst, qt, ks, vst, kw, vwt, sb, glt, oct)
    o = ot.reshape(B, G, nb, DH, HPG, QB).transpose(0, 2, 5, 1, 4, 3)
    return o.reshape(T, NSA_N_HEADS * DH)


def _moe_tables(route, cnt, T, bm):
    idx = route[:, 0:TOP_K].astype(I32)
    pos = route[:, TOP_K:2 * TOP_K].astype(I32)
    counts = cnt[0, :N_EXPERTS].astype(I32)
    padded = (counts + bm - 1) // bm * bm
    pend = jnp.cumsum(padded)
    pstart = pend - padded
    dest = pstart[idx] + pos
    n_blocks = -(-(T * TOP_K + N_EXPERTS * (bm - 1)) // bm)
    buf_tok = jnp.zeros((n_blocks * bm,), I32).at[dest.reshape(-1)].set(
        jnp.repeat(jnp.arange(T, dtype=I32), TOP_K))
    b_start = jnp.arange(n_blocks, dtype=I32) * bm
    block_e = jnp.minimum(jnp.sum(b_start[:, None] >= pend[None, :], axis=1), N_EXPERTS - 1).astype(I32)
    n_used = (pend[-1] // bm).astype(I32).reshape(1)
    return dest, buf_tok.reshape(n_blocks, 1, bm), block_e, n_used


def kernel(x, ln1_w, w_in, ssd_conv_w, ssd_conv_b, ssd_dt_bias, ssd_a_log, ssd_d, ssd_norm_w, ssd_out_w,
           nsa_q_norm_w, nsa_k_norm_w, cmp_pos_k, cmp_pos_v, cmp_k_w1, cmp_k_w2, cmp_v_w1, cmp_v_w2,
           nsa_out_w, w_out, ln2_w, router_w, router_b, exp_w1, exp_b1, exp_w2, exp_b2):
    B, S, D = x.shape
    T = B * S
    depth = ln1_w.shape[0]
    x2 = x.reshape(T, D)
    G = SSD_N_GROUPS

    tail_src = np.zeros((TAIL_W,), np.int32)
    tail_on = np.zeros((TAIL_W,), bool)
    for g in range(G):
        tail_src[g * LANES:g * LANES + SSD_HPG] = O_DT + SSD_HPG * g + np.arange(SSD_HPG)
        tail_src[g * LANES + 8:g * LANES + 8 + 3 * NSA_HPG] = O_GATE + 3 * NSA_HPG * g + np.arange(3 * NSA_HPG)
        tail_on[g * LANES:g * LANES + 8 + 3 * NSA_HPG] = True
    main_src = np.concatenate([np.arange(0, O_DT), np.arange(O_MERGE, O_MERGE + 2 * D_MODEL),
                               np.arange(O_Q, O_GATE)])

    def group_lanes(v):
        out = jnp.zeros((G, LANES), F32).at[:, :SSD_HPG].set(v.astype(F32).reshape(G, SSD_HPG))
        return out.reshape(1, TAIL_W)

    tm_rows = min(512, T)
    bm = 512
    for l in range(depth):
        w_main = w_in[l][:, main_src].astype(BF16)
        w_tail = jnp.where(tail_on[None, :], w_in[l][:, tail_src], 0.0).astype(BF16)
        main, tail = _inproj(x2, ln1_w[l].reshape(1, D), w_main, w_tail, tm=tm_rows, tn=MAIN_W // 4)
        main3 = main.reshape(B, S, MAIN_W)
        tail3 = tail.reshape(B, S, TAIL_W)

        yssd = _ssd(main3, tail3, ssd_conv_w[l], ssd_conv_b[l].reshape(1, -1),
                    group_lanes(ssd_dt_bias[l]), group_lanes(ssd_a_log[l]),
                    jnp.repeat(ssd_d[l].astype(F32), SSD_HEAD_DIM).reshape(1, -1),
                    ssd_norm_w[l].reshape(1, -1))
        onsa = _nsa(main3, tail3, nsa_q_norm_w[l], nsa_k_norm_w[l], cmp_pos_k[l], cmp_pos_v[l],
                    cmp_k_w1[l], cmp_k_w2[l], cmp_v_w1[l], cmp_v_w2[l])

        rw = jnp.zeros((D, LANES), F32).at[:, :N_EXPERTS].set(router_w[l])
        rwh, rwl = _split_bf16(rw)
        rb = jnp.full((1, LANES), NEG, F32).at[0, :N_EXPERTS].set(router_b[l])
        x1, xn2, route, cnt = _merge(x2, yssd.reshape(T, -1), onsa, main,
                                     ssd_out_w[l].astype(BF16), nsa_out_w[l].astype(BF16), w_out[l].astype(BF16),
                                     ln2_w[l].reshape(1, D), rwh, rwl, rb, tm=min(256, T))

        dest, buf_tok3, block_e, n_used = _moe_tables(route, cnt, T, bm)
        ys = _experts(block_e, n_used, buf_tok3, xn2,
                      exp_w1[l][:, :, 0::2].astype(BF16), exp_w1[l][:, :, 1::2].astype(BF16),
                      exp_b1[l][:, None, 0::2], exp_b1[l][:, None, 1::2],
                      exp_w2[l].astype(BF16), exp_b2[l][:, None, :], bm)
        tc = min(256, T)
        dest3 = dest.reshape(T // tc, tc, TOP_K).transpose(0, 2, 1).reshape(T // tc, 1, TOP_K * tc)
        x2 = _combine(dest3, route, x1, ys, tc)
    return x2.reshape(B, S, D)
```

```python
import functools
import math

import numpy as np
import jax
import jax.numpy as jnp
from jax import lax
from jax.experimental import pallas as pl
from jax.experimental.pallas import tpu as pltpu

F32 = jnp.float32
BF16 = jnp.bfloat16
I32 = jnp.int32

D_MODEL = 1024
SSD_D_INNER = 2048
SSD_HEAD_DIM = 64
SSD_N_HEADS = 32
SSD_N_GROUPS = 4
SSD_D_STATE = 128
SSD_CHUNK = 256
SSD_HPG = SSD_N_HEADS // SSD_N_GROUPS
NSA_N_HEADS = 16
NSA_N_KV = 4
NSA_HPG = NSA_N_HEADS // NSA_N_KV
NSA_HEAD_DIM = 64
NSA_CMP_LEN = 32
NSA_CMP_STRIDE = 16
NSA_CMP_HIDDEN = 256
NSA_SEL_LEN = 64
NSA_SEL_TOP = 16
NSA_WINDOW = 512
Q_BLOCK = 128
FORCED_SCORE = 1.0e4
N_EXPERTS = 32
TOP_K = 4
SWIGLU_LIMIT = 7.0
SWIGLU_ALPHA = 1.702
EPS = 1e-6

NEG = -1.0e30
LOG2E = 1.4426950408889634
LANES = 128
K_AUG = 80
VMEM_LIMIT = 56 * 1024 * 1024

C_Z, C_XS, C_B, C_C, C_MS, C_MN, C_Q, C_KC, C_VC, C_KS, C_VS, C_KW, C_VW = (
    0, 2048, 4096, 4608, 5120, 6144, 7168, 8192, 8448, 8704, 8960, 9216, 9472)
MAIN_W = 9728
TAIL_W = 512
O_DT, O_Q, O_GATE, O_MERGE = 5120, 5152, 7712, 7760


def _cparams(sem):
    return pltpu.CompilerParams(dimension_semantics=sem, vmem_limit_bytes=VMEM_LIMIT)


def _dot(a, b):
    return jnp.dot(a, b, preferred_element_type=F32)


def _split_bf16(x):
    hi = x.astype(BF16)
    lo = (x - hi.astype(F32)).astype(BF16)
    return hi, lo


def _inproj_kernel(x_ref, lnw_ref, w_ref, wt_ref, main_ref, tail_ref, xn_ref):
    @pl.when(pl.program_id(1) == 0)
    def _():
        x = x_ref[...]
        ms = jnp.mean(x * x, axis=-1, keepdims=True)
        xn = (x * lax.rsqrt(ms + EPS) * lnw_ref[...]).astype(BF16)
        xn_ref[...] = xn
        tail_ref[...] = _dot(xn, wt_ref[...])

    main_ref[...] = _dot(xn_ref[...], w_ref[...]).astype(BF16)


def _inproj(x2, ln_w, w_main, w_tail, tm, tn):
    T = x2.shape[0]
    return pl.pallas_call(
        _inproj_kernel,
        grid=(T // tm, MAIN_W // tn),
        in_specs=[pl.BlockSpec((tm, D_MODEL), lambda i, j: (i, 0)),
                  pl.BlockSpec((1, D_MODEL), lambda i, j: (0, 0)),
                  pl.BlockSpec((D_MODEL, tn), lambda i, j: (0, j)),
                  pl.BlockSpec((D_MODEL, TAIL_W), lambda i, j: (0, 0))],
        out_specs=[pl.BlockSpec((tm, tn), lambda i, j: (i, j)),
                   pl.BlockSpec((tm, TAIL_W), lambda i, j: (i, 0))],
        out_shape=[jax.ShapeDtypeStruct((T, MAIN_W), BF16),
                   jax.ShapeDtypeStruct((T, TAIL_W), F32)],
        scratch_shapes=[pltpu.VMEM((tm, D_MODEL), BF16)],
        compiler_params=_cparams(("arbitrary", "arbitrary")),
        name="inproj",
    )(x2, ln_w, w_main, w_tail)


def _softplus(x):
    return jnp.maximum(x, 0.0) + jnp.log1p(jnp.exp(-jnp.abs(x)))


def _ssd_kernel(z_ref, xs_ref, b_ref, c_ref, dt_ref, cwx_ref, cwb_ref, cwc_ref, cbx_ref, cbb_ref, cbc_ref,
                dtb_ref, alog_ref, dexp_ref, nw_ref, y_ref, extx, extb, extc, hs_ref):
    L = xs_ref.shape[0]
    P = SSD_HEAD_DIM

    @pl.when(pl.program_id(2) == 0)
    def _():
        extx[0:8, :] = jnp.zeros((8, extx.shape[1]), F32)
        extb[0:8, :] = jnp.zeros((8, extb.shape[1]), F32)
        extc[0:8, :] = jnp.zeros((8, extc.shape[1]), F32)
        hs_ref[...] = jnp.zeros(hs_ref.shape, F32)

    def conv_act(x_ref, w_ref, bias_ref, ext):
        xf = x_ref[...].astype(F32)
        ext[8:8 + L, :] = xf
        acc = bias_ref[...] + w_ref[3:4, :] * xf
        for k in (1, 2, 3):
            acc = acc + w_ref[3 - k:4 - k, :] * ext[8 - k:8 - k + L, :]
        ext[0:8, :] = ext[L:L + 8, :]
        return acc * jax.nn.sigmoid(acc)

    xs = conv_act(xs_ref, cwx_ref, cbx_ref, extx)
    bm = conv_act(b_ref, cwb_ref, cbb_ref, extb)
    cm = conv_act(c_ref, cwc_ref, cbc_ref, extc)

    dt = _softplus(dt_ref[...] + dtb_ref[...])
    da = dt * (-jnp.exp(alog_ref[...]))
    row = lax.broadcasted_iota(I32, (L, L), 0)
    col = lax.broadcasted_iota(I32, (L, L), 1)
    tril = row >= col
    trif = jnp.where(tril, 1.0, 0.0).astype(BF16)
    da_hi, da_lo = _split_bf16(da)
    acum = _dot(trif, da_hi) + _dot(trif, da_lo)
    acum_t = acum.T
    a_end = acum[L - 1:L, :]
    ea = jnp.exp(acum)
    wend = jnp.exp(a_end - acum) * dt
    eend = jnp.exp(a_end)

    cb16 = cm.astype(BF16)
    bt16 = bm.T.astype(BF16)
    cb = _dot(cb16, bt16)

    ys = []
    for hh in range(SSD_HPG):
        seg = acum[:, hh:hh + 1] - acum_t[hh:hh + 1, :]
        decay = jnp.exp(jnp.where(tril, seg, NEG))
        g = (cb * decay).astype(BF16)
        xh = xs[:, hh * P:(hh + 1) * P]
        xdt = (xh * dt[:, hh:hh + 1]).astype(BF16)
        hprev = hs_ref[hh]
        y = _dot(g, xdt) + _dot(cb16, hprev.astype(BF16)) * ea[:, hh:hh + 1]
        xw = (xh * wend[:, hh:hh + 1]).astype(BF16)
        hs_ref[hh] = hprev * eend[:, hh:hh + 1] + _dot(bt16, xw)
        ys.append(y)
    y = jnp.concatenate(ys, axis=1)
    y = y + dexp_ref[...] * xs
    z = z_ref[...].astype(F32)
    y = y * (z * jax.nn.sigmoid(z))
    ms = jnp.mean(y * y, axis=-1, keepdims=True)
    y_ref[...] = (y * lax.rsqrt(ms + EPS) * nw_ref[...]).astype(BF16)


def _ssd(main3, tail3, conv_w, conv_b, dtb, alog, dexp, norm_w):
    B, S, _ = main3.shape
    L = math.gcd(S, SSD_CHUNK)
    nc = S // L
    G, N, GW = SSD_N_GROUPS, SSD_D_STATE, SSD_D_INNER // SSD_N_GROUPS
    xs0, b0, c0 = C_XS // GW, C_B // N, C_C // N
    cb0, cc0 = SSD_D_INNER // N, (SSD_D_INNER + G * N) // N

    def seq(w, off):
        return pl.BlockSpec((None, L, w), lambda b, g, c: (b, c, off + g))

    def par(r, w, off):
        return pl.BlockSpec((r, w), lambda b, g, c: (0, off + g))

    return pl.pallas_call(
        _ssd_kernel,
        grid=(B, G, nc),
        in_specs=[seq(GW, 0), seq(GW, xs0), seq(N, b0), seq(N, c0), seq(LANES, 0),
                  par(4, GW, 0), par(4, N, cb0), par(4, N, cc0),
                  par(1, GW, 0), par(1, N, cb0), par(1, N, cc0),
                  par(1, LANES, 0), par(1, LANES, 0), par(1, GW, 0), par(1, GW, 0)],
        out_specs=pl.BlockSpec((None, L, GW), lambda b, g, c: (b, c, g)),
        out_shape=jax.ShapeDtypeStruct((B, S, SSD_D_INNER), BF16),
        scratch_shapes=[pltpu.VMEM((L + 8, GW), F32), pltpu.VMEM((L + 8, N), F32), pltpu.VMEM((L + 8, N), F32),
                        pltpu.VMEM((SSD_HPG, N, SSD_HEAD_DIM), F32)],
        compiler_params=_cparams(("arbitrary", "arbitrary", "arbitrary")),
        name="ssd",
    )(main3, main3, main3, main3, tail3, conv_w, conv_w, conv_w, conv_b, conv_b, conv_b,
      dtb, alog, dexp, norm_w)


def _nsa_norm_kernel(q_ref, ks_ref, kw_ref, bdq_ref, bdk_ref, wq_ref, wk_ref, qo_ref, kso_ref, kwo_ref):
    def head_norm(x_ref, bd_ref, w_ref):
        x = x_ref[...].astype(F32)
        hi, lo = _split_bf16(x * x)
        ms = (_dot(hi, bd_ref[...]) + _dot(lo, bd_ref[...])) * (1.0 / NSA_HEAD_DIM)
        return (x * lax.rsqrt(ms + EPS) * w_ref[...]).astype(BF16)

    qo_ref[...] = head_norm(q_ref, bdq_ref, wq_ref)
    kso_ref[...] = head_norm(ks_ref, bdk_ref, wk_ref)
    kwo_ref[...] = head_norm(kw_ref, bdk_ref, wk_ref)


def _nsa_norm(main, bdq, bdk, wq, wk, tm):
    T = main.shape[0]
    QW, KW = NSA_N_HEADS * NSA_HEAD_DIM, NSA_N_KV * NSA_HEAD_DIM
    const = lambda shape: pl.BlockSpec(shape, lambda i: (0, 0))
    return pl.pallas_call(
        _nsa_norm_kernel,
        grid=(T // tm,),
        in_specs=[pl.BlockSpec((tm, QW), lambda i: (i, C_Q // QW)),
                  pl.BlockSpec((tm, KW), lambda i: (i, C_KS // KW)),
                  pl.BlockSpec((tm, KW), lambda i: (i, C_KW // KW)),
                  const((QW, QW)), const((KW, KW)), const((1, QW)), const((1, KW))],
        out_specs=[pl.BlockSpec((tm, QW), lambda i: (i, 0)),
                   pl.BlockSpec((tm, KW), lambda i: (i, 0)),
                   pl.BlockSpec((tm, KW), lambda i: (i, 0))],
        out_shape=[jax.ShapeDtypeStruct((T, QW), BF16), jax.ShapeDtypeStruct((T, KW), BF16),
                   jax.ShapeDtypeStruct((T, KW), BF16)],
        compiler_params=_cparams(("arbitrary",)),
        name="nsa_norm",
    )(main, main, main, bdq, bdk, wq, wk)


def _compress_kernel(u_ref, pos_ref, w1_ref, w2_ref, nw_ref, o_ref, *, normalize):
    half = u_ref.shape[1]
    nrow = u_ref.shape[0]
    u = u_ref[...]
    a = _dot(u, w1_ref[0:half, :])
    b = _dot(u, w1_ref[half:2 * half, :])
    posc = _dot(pos_ref[...], w1_ref[...])[0:1, :]
    pre = a + pltpu.roll(b, nrow - 1, 0) + posc
    act = 0.5 * pre * (1.0 + jnp.tanh(math.sqrt(2.0 / math.pi) * (pre + 0.044715 * (pre * pre * pre))))
    o = _dot(act.astype(BF16), w2_ref[...])
    if normalize:
        ms = jnp.mean(o * o, axis=-1, keepdims=True)
        o = o * lax.rsqrt(ms + EPS) * nw_ref[...]
    o_ref[...] = o


def _compress(u, pos8, w1, w2, nw, normalize):
    BG, nrow, half = u.shape
    const = lambda shape: pl.BlockSpec(shape, lambda i: (0, 0))
    return pl.pallas_call(
        functools.partial(_compress_kernel, normalize=normalize),
        grid=(BG,),
        in_specs=[pl.BlockSpec((None, nrow, half), lambda i: (i, 0, 0)),
                  const(pos8.shape), const(w1.shape), const(w2.shape), const(nw.shape)],
        out_specs=pl.BlockSpec((None, nrow, NSA_HEAD_DIM), lambda i: (i, 0, 0)),
        out_shape=jax.ShapeDtypeStruct((BG, nrow, NSA_HEAD_DIM), F32),
        compiler_params=_cparams(("arbitrary",)),
        name="nsa_compress_norm" if normalize else "nsa_compress",
    )(u, pos8, w1, w2, nw)


def _nsa_cmp_kernel(qt_ref, kc_ref, vct_ref, ovt_ref, pm_ref, glt_ref, oct_ref, sb_ref, lst_ref, imp_scr):
    i = pl.program_id(2)
    ncmp = kc_ref.shape[0]
    nsel = ovt_ref.shape[0]
    QB = Q_BLOCK
    W = NSA_HPG * QB
    CH = min(LANES, ncmp)
    span = CH * NSA_CMP_STRIDE
    nch = jnp.minimum((i * QB + QB - NSA_CMP_LEN) // span + 1, ncmp // CH)

    def attend(rows):
        s = _dot(kc_ref[0:rows, :], qt_ref[...])
        jrow = lax.broadcasted_iota(I32, (rows, W), 0)
        lane = lax.broadcasted_iota(I32, (rows, W), 1)
        t = i * QB + (lane & (QB - 1))
        mask = (NSA_CMP_STRIDE * jrow + (NSA_CMP_LEN - 1)) <= t
        sm = jnp.where(mask, s, NEG)
        m = jnp.max(sm, axis=0, keepdims=True)
        p = jnp.where(mask, jnp.exp2(sm - m), 0.0)
        l = jnp.sum(p, axis=0, keepdims=True)
        pn = p * (1.0 / jnp.maximum(l, 1e-30))
        oct_ref[...] = _dot(vct_ref[:, 0:rows], pn.astype(BF16)) * jax.nn.sigmoid(glt_ref[0:1, :])
        psum = pn[:, 0:QB]
        for h in range(1, NSA_HPG):
            psum = psum + pn[:, h * QB:(h + 1) * QB]
        imp_scr[...] = _dot(ovt_ref[:, 0:rows], psum.astype(BF16))

    for k in range(1, ncmp // CH + 1):
        pl.when(nch == k)(functools.partial(attend, k * CH))
    imp = imp_scr[...]

    jf = lax.broadcasted_iota(I32, (nsel, QB), 0).astype(F32)
    tq = (i * QB + lax.broadcasted_iota(I32, (nsel, QB), 1))
    cur = (tq >> 6).astype(F32)
    forced = (jf == 0.0) | (jf == cur) | (jf == cur - 1.0)
    valid = jf <= cur
    v0 = jnp.where(forced, FORCED_SCORE, jnp.where(valid, imp, -1.0))

    def pick_one(_, carry):
        v, sel = carry
        mx = jnp.max(v, axis=0, keepdims=True)
        first = jnp.min(jnp.where(v == mx, jf, float(nsel)), axis=0, keepdims=True)
        pick = jf == first
        return jnp.where(pick, -2.0, v), jnp.where(pick, 1.0, sel)

    _, sel = lax.fori_loop(0, min(NSA_SEL_TOP, nsel), pick_one, (v0, jnp.zeros((nsel, QB), F32)))
    sel = jnp.where(valid, sel, 0.0)
    sb_ref[...] = jnp.where(sel > 0.0, 0.0, NEG)

    npair = nsel // 2
    pairsel = _dot(pm_ref[...], sel.astype(BF16))
    jp = lax.broadcasted_iota(I32, (npair, LANES), 0)
    need = (jnp.sum(pairsel, axis=1, keepdims=True) > 0.0) & (jp < i)
    needf = jnp.where(need, 1.0, 0.0)
    r2 = lax.broadcasted_iota(I32, (npair, npair), 0)
    c2 = lax.broadcasted_iota(I32, (npair, npair), 1)
    tri = jnp.where(c2 <= r2, 1.0, 0.0).astype(BF16)
    prefix = _dot(tri, needf.astype(BF16))
    slot = lax.broadcasted_iota(I32, (npair, LANES), 1).astype(F32)
    onehot = jnp.where(need & (prefix == slot + 1.0), 1.0, 0.0).astype(BF16)
    jpv = lax.broadcasted_iota(I32, (8, npair), 1).astype(F32).astype(BF16)
    lst = _dot(jpv, onehot)
    cnt = _dot(jnp.ones((8, npair), BF16), needf.astype(BF16))
    r8 = lax.broadcasted_iota(I32, (8, LANES), 0)
    lst_ref[...] = jnp.where(r8 == 0, lst, cnt).astype(I32)


def _nsa_cmp(qt, kc, vct, ovt, pm, glt):
    B, G, nb, _, W = qt.shape
    ncmp = kc.shape[2]
    nsel = ovt.shape[0]
    blk = lambda *shape: pl.BlockSpec((None, None, None) + shape, lambda b, g, i: (b, g, i) + (0,) * len(shape))
    per_bg = lambda *shape: pl.BlockSpec((None, None) + shape, lambda b, g, i: (b, g) + (0,) * len(shape))
    const = lambda shape: pl.BlockSpec(shape, lambda b, g, i: (0,) * len(shape))
    return pl.pallas_call(
        _nsa_cmp_kernel,
        grid=(B, G, nb),
        in_specs=[blk(K_AUG, W), per_bg(ncmp, K_AUG), per_bg(NSA_HEAD_DIM, ncmp),
                  const(ovt.shape), const(pm.shape), blk(8, W)],
        out_specs=[blk(NSA_HEAD_DIM, W), blk(nsel, Q_BLOCK), blk(8, LANES)],
        out_shape=[jax.ShapeDtypeStruct((B, G, nb, NSA_HEAD_DIM, W), F32),
                   jax.ShapeDtypeStruct((B, G, nb, nsel, Q_BLOCK), F32),
                   jax.ShapeDtypeStruct((B, G, nb, 8, LANES), I32)],
        scratch_shapes=[pltpu.VMEM((nsel, Q_BLOCK), F32)],
        compiler_params=_cparams(("arbitrary", "arbitrary", "arbitrary")),
        name="nsa_cmp_select",
    )(qt, kc, vct, ovt, pm, glt)


def _tile4(r):
    return jnp.concatenate([r] * NSA_HPG, axis=1)


def _nsa_sel_kernel(lst_ref, qt_ref, ks_ref, vst_ref, kw_ref, vwt_ref, sb_ref, glt_ref, oct_ref, o_ref,
                    s_scr, d_scr, ow_scr):
    i = pl.program_id(2)
    QB = Q_BLOCK
    W = NSA_HPG * QB
    HB = NSA_SEL_LEN
    qt = qt_ref[...]
    count = lst_ref[1, 0]

    def flash(s, vt, carry):
        m, l, acc = carry
        mn = jnp.maximum(m, jnp.max(s, axis=0, keepdims=True))
        alpha = jnp.exp2(m - mn)
        p = jnp.exp2(s - mn)
        l = alpha * l + jnp.sum(p, axis=0, keepdims=True)
        acc = alpha * acc + _dot(vt, p.astype(BF16))
        return mn, l, acc

    def pair_scores(k):
        jp = lst_ref[0, k]
        live = k < count
        s = _dot(ks_ref[pl.ds(pl.multiple_of(jp * QB, QB), QB), :], qt)
        b0 = jnp.where(live, _tile4(sb_ref[2 * jp]), NEG)
        b1 = jnp.where(live, _tile4(sb_ref[2 * jp + 1]), NEG)
        return jnp.concatenate([s[0:HB] + b0, s[HB:2 * HB] + b1], axis=0)

    def scores(it):
        return jnp.concatenate([pair_scores(2 * it), pair_scores(2 * it + 1)], axis=0)

    s_scr[0] = scores(0)

    r = lax.broadcasted_iota(I32, (QB, W), 0)
    c = lax.broadcasted_iota(I32, (QB, W), 1) & (QB - 1)
    kd = ks_ref[pl.ds(pl.multiple_of(i * QB, QB), QB), :]
    d_scr[...] = jnp.where(r <= c, _dot(kd, qt), NEG)

    ss, vts = [], []
    for w in range(NSA_WINDOW // QB + 1):
        pw = i - NSA_WINDOW // QB + w
        pc = jnp.maximum(pw, 0)
        kk = kw_ref[pl.ds(pl.multiple_of(pc * QB, QB), QB), :]
        s = _dot(kk, qt)
        if w == 0:
            s = jnp.where((r > c) & (pw >= 0), s, NEG)
        elif w == NSA_WINDOW // QB:
            s = jnp.where(r <= c, s, NEG)
        else:
            s = jnp.where(pw >= 0, s, NEG)
        ss.append(s)
        vts.append(vwt_ref[pc])
    s = jnp.concatenate(ss, axis=0)
    mw = jnp.max(s, axis=0, keepdims=True)
    p = jnp.exp2(s - mw)
    lw = jnp.sum(p, axis=0, keepdims=True)
    o_win = _dot(jnp.concatenate(vts, axis=1), p.astype(BF16)) * (1.0 / lw)
    ow_scr[...] = oct_ref[...] + jax.nn.sigmoid(glt_ref[2:3, :]) * o_win

    def body(it, carry):
        slot = it & 1
        s = s_scr[slot]
        vt = jnp.concatenate([vst_ref[lst_ref[0, 2 * it]], vst_ref[lst_ref[0, 2 * it + 1]]], axis=1)
        s_scr[1 - slot] = scores(it + 1)
        return flash(s, vt, carry)

    init = (jnp.full((1, W), NEG, F32), jnp.zeros((1, W), F32), jnp.zeros((NSA_HEAD_DIM, W), F32))
    carry = lax.fori_loop(0, (count + 1) // 2, body, init)
    m, l, acc = flash(d_scr[...], vst_ref[i], carry)
    o_ref[...] = (ow_scr[...] + jax.nn.sigmoid(glt_ref[1:2, :]) * (acc * (1.0 / l))).astype(BF16)


def _nsa_sel(lst, qt, ks, vst, kw, vwt, sb, glt, oct):
    B, G, nb, _, W = qt.shape
    S = ks.shape[2]
    npair = vst.shape[2]
    nsel = sb.shape[3]
    blk = lambda *shape: pl.BlockSpec((None, None, None) + shape, lambda b, g, i: (b, g, i) + (0,) * len(shape))
    per_bg = lambda *shape: pl.BlockSpec((None, None) + shape, lambda b, g, i: (b, g) + (0,) * len(shape))
    return pl.pallas_call(
        _nsa_sel_kernel,
        grid=(B, G, nb),
        in_specs=[pl.BlockSpec((None, None, None, 8, LANES), lambda b, g, i: (b, g, i, 0, 0),
                               memory_space=pltpu.SMEM),
                  blk(K_AUG, W), per_bg(S, K_AUG), per_bg(npair, NSA_HEAD_DIM, Q_BLOCK),
                  per_bg(S, K_AUG), per_bg(npair, NSA_HEAD_DIM, Q_BLOCK),
                  blk(nsel, 1, Q_BLOCK), blk(8, W), blk(NSA_HEAD_DIM, W)],
        out_specs=blk(NSA_HEAD_DIM, W),
        out_shape=jax.ShapeDtypeStruct((B, G, nb, NSA_HEAD_DIM, W), BF16),
        scratch_shapes=[pltpu.VMEM((2, 2 * Q_BLOCK, W), F32), pltpu.VMEM((Q_BLOCK, W), F32),
                        pltpu.VMEM((NSA_HEAD_DIM, W), F32)],
        compiler_params=_cparams(("arbitrary", "arbitrary", "arbitrary")),
        name="nsa_select_window",
    )(lst, qt, ks, vst, kw, vwt, sb, glt, oct)


def _merge_kernel(x_ref, ys_ref, on_ref, gs_ref, gn_ref, wssd_ref, wnsa_ref, wout_ref, ln2_ref,
                  rwh_ref, rwl_ref, rb_ref, x1_ref, xn2_ref, route_ref, cnt_ref, base_ref):
    tm = x_ref.shape[0]

    @pl.when(pl.program_id(0) == 0)
    def _():
        base_ref[...] = jnp.zeros(base_ref.shape, F32)

    y_ssd = _dot(ys_ref[...], wssd_ref[...])
    y_nsa = _dot(on_ref[...], wnsa_ref[...])
    merged = (jax.nn.sigmoid(gs_ref[...].astype(F32)) * y_ssd
              + jax.nn.sigmoid(gn_ref[...].astype(F32)) * y_nsa)
    x1 = x_ref[...] + _dot(merged.astype(BF16), wout_ref[...])
    x1_ref[...] = x1
    ms = jnp.mean(x1 * x1, axis=-1, keepdims=True)
    xn2 = x1 * lax.rsqrt(ms + EPS) * ln2_ref[...]
    xn2_ref[...] = xn2

    xh, xl = _split_bf16(xn2)
    logits = _dot(xh, rwh_ref[...]) + _dot(xl, rwh_ref[...]) + _dot(xh, rwl_ref[...]) + rb_ref[...]
    lane = lax.broadcasted_iota(I32, (tm, LANES), 1)
    lanef = lane.astype(F32)
    v = logits
    onehot = jnp.zeros((tm, LANES), F32)
    vals, picks = [], []
    for _ in range(TOP_K):
        mx = jnp.max(v, axis=-1, keepdims=True)
        first = jnp.min(jnp.where(v == mx, lanef, float(LANES)), axis=-1, keepdims=True)
        pick = lanef == first
        v = jnp.where(pick, 2.0 * NEG, v)
        onehot = jnp.where(pick, 1.0, onehot)
        vals.append(mx)
        picks.append((pick, first))
    es = [jnp.exp(val - vals[0]) for val in vals]
    inv = 1.0 / (es[0] + es[1] + es[2] + es[3])

    r2 = lax.broadcasted_iota(I32, (tm, tm), 0)
    c2 = lax.broadcasted_iota(I32, (tm, tm), 1)
    stril = jnp.where(c2 < r2, 1.0, 0.0).astype(BF16)
    posmap = base_ref[...] + _dot(stril, onehot.astype(BF16))
    route = jnp.zeros((tm, LANES), F32)
    for k in range(TOP_K):
        pick, first = picks[k]
        pos = jnp.sum(jnp.where(pick, posmap, 0.0), axis=-1, keepdims=True)
        route = jnp.where(lane == k, first, route)
        route = jnp.where(lane == TOP_K + k, pos, route)
        route = jnp.where(lane == 2 * TOP_K + k, es[k] * inv, route)
    route_ref[...] = route
    base = base_ref[...] + jnp.sum(onehot, axis=0, keepdims=True)
    base_ref[...] = base
    cnt_ref[...] = jnp.broadcast_to(base, cnt_ref.shape)


def _merge(x2, yssd, onsa, main, wssd, wnsa, wout, ln2, rwh, rwl, rb, tm):
    T = x2.shape[0]
    const = lambda shape: pl.BlockSpec(shape, lambda i: (0, 0))
    row = lambda w, j=0: pl.BlockSpec((tm, w), lambda i: (i, j))
    return pl.pallas_call(
        _merge_kernel,
        grid=(T // tm,),
        in_specs=[row(D_MODEL), row(SSD_D_INNER), row(D_MODEL), row(D_MODEL, C_MS // D_MODEL),
                  row(D_MODEL, C_MN // D_MODEL),
                  const(wssd.shape), const(wnsa.shape), const(wout.shape), const(ln2.shape),
                  const(rwh.shape), const(rwl.shape), const(rb.shape)],
        out_specs=[row(D_MODEL), row(D_MODEL), row(LANES), const((8, LANES))],
        out_shape=[jax.ShapeDtypeStruct((T, D_MODEL), F32), jax.ShapeDtypeStruct((T, D_MODEL), F32),
                   jax.ShapeDtypeStruct((T, LANES), F32), jax.ShapeDtypeStruct((8, LANES), F32)],
        scratch_shapes=[pltpu.VMEM((1, LANES), F32)],
        compiler_params=_cparams(("arbitrary",)),
        name="merge_router",
    )(x2, yssd, onsa, main, main, wssd, wnsa, wout, ln2, rwh, rwl, rb)


def _expert_kernel(be_ref, nu_ref, tok_ref, tokn_ref, xn_hbm, w1g_ref, w1l_ref, b1g_ref, b1l_ref, w2_ref, b2_ref,
                   ys_ref, xbuf, sem):
    bm = xbuf.shape[1]
    blk = pl.program_id(0)
    slot = blk & 1

    def gather(tok_smem, s):
        def issue(r, c):
            pltpu.make_async_copy(xn_hbm.at[pl.ds(tok_smem[0, r], 1)], xbuf.at[s, pl.ds(r, 1)], sem.at[s]).start()
            return c

        lax.fori_loop(0, bm, issue, 0, unroll=8)

    @pl.when(blk == 0)
    def _():
        gather(tok_ref, 0)

    for s in (0, 1):
        @pl.when((blk + 1 < nu_ref[0]) & (slot == 1 - s))
        def _():
            gather(tokn_ref, s)

    @pl.when(blk < nu_ref[0])
    def _():
        pltpu.make_async_copy(xn_hbm.at[pl.ds(0, bm)], xbuf.at[slot], sem.at[slot]).wait()
        x = xbuf[slot].astype(BF16)
        glu = jnp.minimum(_dot(x, w1g_ref[...]) + b1g_ref[...], SWIGLU_LIMIT)
        lin = jnp.clip(_dot(x, w1l_ref[...]) + b1l_ref[...], -SWIGLU_LIMIT, SWIGLU_LIMIT)
        act = glu * jax.nn.sigmoid(SWIGLU_ALPHA * glu) * (lin + 1.0)
        ys_ref[...] = _dot(act.astype(BF16), w2_ref[...]) + b2_ref[...]

    @pl.when(blk >= nu_ref[0])
    def _():
        ys_ref[...] = jnp.zeros(ys_ref.shape, F32)


def _experts(block_e, n_used, buf_tok3, xn2, w1p, b1g, b1l, w2, b2, bm):
    n_blocks = buf_tok3.shape[0]
    DE = w2.shape[1]
    wspec = lambda r, c, j=0: pl.BlockSpec((None, r, c), lambda b, be, nu: (be[b], 0, j))
    return pl.pallas_call(
        _expert_kernel,
        grid_spec=pltpu.PrefetchScalarGridSpec(
            num_scalar_prefetch=2,
            grid=(n_blocks,),
            in_specs=[pl.BlockSpec((None, 1, bm), lambda b, be, nu: (b, 0, 0), memory_space=pltpu.SMEM),
                      pl.BlockSpec((None, 1, bm), lambda b, be, nu: (jnp.minimum(b + 1, n_blocks - 1), 0, 0),
                                   memory_space=pltpu.SMEM),
                      pl.BlockSpec(memory_space=pl.ANY),
                      wspec(D_MODEL, DE, 0), wspec(D_MODEL, DE, 1), wspec(1, DE), wspec(1, DE),
                      wspec(DE, D_MODEL), wspec(1, D_MODEL)],
            out_specs=pl.BlockSpec((bm, D_MODEL), lambda b, be, nu: (b, 0)),
            scratch_shapes=[pltpu.VMEM((2, bm, D_MODEL), F32), pltpu.SemaphoreType.DMA((2,))]),
        out_shape=jax.ShapeDtypeStruct((n_blocks * bm, D_MODEL), F32),
        compiler_params=_cparams(("arbitrary",)),
        name="moe_experts",
    )(block_e, n_used, buf_tok3, buf_tok3, xn2, w1p, w1p, b1g, b1l, w2, b2)


def _deinterleave_kernel(w_ref, p_ref, o_ref):
    o_ref[...] = _dot(w_ref[...].astype(BF16), p_ref[...]).astype(BF16)


def _deinterleave(w1):
    E, D, W = w1.shape
    src = np.concatenate([np.arange(0, W, 2), np.arange(1, W, 2)])
    perm = jnp.asarray(np.arange(W)[:, None] == src[None, :], BF16)
    rows = D // 2
    return pl.pallas_call(
        _deinterleave_kernel,
        grid=(E, D // rows),
        in_specs=[pl.BlockSpec((None, rows, W), lambda e, r: (e, r, 0)),
                  pl.BlockSpec((W, W), lambda e, r: (0, 0))],
        out_specs=pl.BlockSpec((None, rows, W), lambda e, r: (e, r, 0)),
        out_shape=jax.ShapeDtypeStruct((E, D, W), BF16),
        compiler_params=_cparams(("arbitrary", "arbitrary")),
        name="moe_w1_deinterleave",
    )(w1, perm)


def _combine_kernel(dest_ref, destn_ref, route_ref, x1_ref, ys_hbm, o_ref, buf, sem):
    tc = x1_ref.shape[0]
    i = pl.program_id(0)
    slot = i & 1

    def gather(dest_smem, s):
        def issue(r, c):
            pltpu.make_async_copy(ys_hbm.at[pl.ds(dest_smem[0, r], 1)], buf.at[s, pl.ds(r, 1)], sem.at[s]).start()
            return c

        lax.fori_loop(0, TOP_K * tc, issue, 0, unroll=8)

    @pl.when(i == 0)
    def _():
        gather(dest_ref, 0)

    for s in (0, 1):
        @pl.when((i + 1 < pl.num_programs(0)) & (slot == 1 - s))
        def _():
            gather(destn_ref, s)

    pltpu.make_async_copy(ys_hbm.at[pl.ds(0, TOP_K * tc)], buf.at[slot], sem.at[slot]).wait()
    route = route_ref[...]
    acc = x1_ref[...]
    for k in range(TOP_K):
        acc = acc + route[:, 2 * TOP_K + k:2 * TOP_K + k + 1] * buf[slot, k * tc:(k + 1) * tc, :]
    o_ref[...] = acc


def _combine(dest3, route, x1, ys, tc):
    T = x1.shape[0]
    n = T // tc
    return pl.pallas_call(
        _combine_kernel,
        grid=(n,),
        in_specs=[pl.BlockSpec((None, 1, TOP_K * tc), lambda i: (i, 0, 0), memory_space=pltpu.SMEM),
                  pl.BlockSpec((None, 1, TOP_K * tc), lambda i: (jnp.minimum(i + 1, n - 1), 0, 0),
                               memory_space=pltpu.SMEM),
                  pl.BlockSpec((tc, LANES), lambda i: (i, 0)),
                  pl.BlockSpec((tc, D_MODEL), lambda i: (i, 0)),
                  pl.BlockSpec(memory_space=pl.ANY)],
        out_specs=pl.BlockSpec((tc, D_MODEL), lambda i: (i, 0)),
        out_shape=jax.ShapeDtypeStruct((T, D_MODEL), F32),
        scratch_shapes=[pltpu.VMEM((2, TOP_K * tc, D_MODEL), F32), pltpu.SemaphoreType.DMA((2,))],
        compiler_params=_cparams(("arbitrary",)),
        name="moe_combine",
    )(dest3, dest3, route, x1, ys)


def _split_pos(pos):
    lo = pos % 256
    return lo.astype(np.float32), (pos - lo).astype(np.float32)


def _aug_keys(k, pos):
    lo, hi = _split_pos(pos)
    cols = np.stack([lo, hi, lo, hi] + [np.zeros_like(lo)] * (K_AUG - NSA_HEAD_DIM - 4), axis=-1)
    cols = jnp.broadcast_to(jnp.asarray(cols, BF16), k.shape[:-1] + (cols.shape[-1],))
    return jnp.concatenate([k, cols], axis=-1)


def _slope_rows():
    slopes = (2.0 ** (-8.0 * np.arange(1, NSA_N_HEADS + 1) / NSA_N_HEADS)).astype(np.float32)
    sl = jnp.asarray(np.repeat((slopes * np.float32(LOG2E)).reshape(NSA_N_KV, NSA_HPG), Q_BLOCK, axis=1), F32)
    hi, lo = _split_bf16(sl)
    zero = jnp.zeros_like(hi)
    return jnp.stack([hi, hi, lo, lo] + [zero] * (K_AUG - NSA_HEAD_DIM - 4), axis=1)


def _nsa(main3, tail3, q_norm_w, k_norm_w, cmp_pos_k, cmp_pos_v, cmp_k_w1, cmp_k_w2, cmp_v_w1, cmp_v_w2):
    B, S, _ = main3.shape
    T = B * S
    G, HPG, DH, QB = NSA_N_KV, NSA_HPG, NSA_HEAD_DIM, Q_BLOCK
    nb, npair, nsel, ncmp = S // QB, S // QB, S // NSA_SEL_LEN, S // NSA_CMP_STRIDE
    main = main3.reshape(T, MAIN_W)

    hd = np.arange(NSA_N_HEADS * DH) // DH
    bdq = jnp.asarray(hd[:, None] == hd[None, :], BF16)
    bdk = bdq[:G * DH, :G * DH]
    wq = (jnp.tile(q_norm_w.astype(F32), NSA_N_HEADS) * (DH ** -0.5 * LOG2E)).reshape(1, -1)
    wk = jnp.tile(k_norm_w.astype(F32), G).reshape(1, -1)
    qn, ksn, kwn = _nsa_norm(main, bdq, bdk, wq, wk, tm=min(512, T))

    def kv_heads(t):
        return t.reshape(B, S, G, DH).transpose(0, 2, 1, 3)

    def v_pairs_t(v):
        return v.reshape(B, G, npair, QB, DH).transpose(0, 1, 2, 4, 3)

    qt = qn.reshape(B, nb, QB, G, HPG, DH).transpose(0, 3, 1, 5, 4, 2).reshape(B, G, nb, DH, HPG * QB)
    srow = jnp.broadcast_to(_slope_rows()[None, :, None], (B, G, nb, K_AUG - DH, HPG * QB))
    qt = jnp.concatenate([qt, srow], axis=3)

    tok_pos = np.arange(S)
    ks = _aug_keys(kv_heads(ksn), tok_pos)
    kw = _aug_keys(kv_heads(kwn), tok_pos)
    vst = v_pairs_t(kv_heads(main[:, C_VS:C_VS + G * DH]))
    vwt = v_pairs_t(kv_heads(main[:, C_VW:C_VW + G * DH]))

    def halves(c0):
        return kv_heads(main[:, c0:c0 + G * DH]).reshape(B * G, ncmp, NSA_CMP_STRIDE * DH)

    def pos8(pos):
        return jnp.zeros((8, NSA_CMP_LEN * DH), F32).at[0].set(pos.reshape(-1)).astype(BF16)

    ones = jnp.ones((1, DH), F32)
    kc = _compress(halves(C_KC), pos8(cmp_pos_k), cmp_k_w1.astype(BF16), cmp_k_w2.astype(BF16),
                   k_norm_w.astype(F32).reshape(1, DH), True)
    vc = _compress(halves(C_VC), pos8(cmp_pos_v), cmp_v_w1.astype(BF16), cmp_v_w2.astype(BF16), ones, False)
    cmp_end = np.arange(ncmp) * NSA_CMP_STRIDE + NSA_CMP_LEN - 1
    kc = _aug_keys(kc.astype(BF16).reshape(B, G, ncmp, DH), cmp_end)
    vct = vc.astype(BF16).reshape(B, G, ncmp, DH).transpose(0, 1, 3, 2)

    c_start = (np.arange(ncmp) * NSA_CMP_STRIDE)[:, None]
    s_start = (np.arange(nsel) * NSA_SEL_LEN)[None, :]
    overlap = np.clip(np.minimum(c_start + NSA_CMP_LEN, s_start + NSA_SEL_LEN)
                      - np.maximum(c_start, s_start), 0, None) / NSA_CMP_LEN
    overlap[(S - NSA_CMP_LEN) // NSA_CMP_STRIDE + 1:] = 0.0
    ovt = jnp.asarray(overlap.T, BF16)
    pm = jnp.asarray(np.arange(nsel)[None, :] // 2 == np.arange(nsel // 2)[:, None], BF16)

    gl = tail3.reshape(B, nb, QB, G, LANES)[..., 8:8 + 3 * HPG].reshape(B, nb, QB, G, HPG, 3)
    glt = gl.transpose(0, 3, 1, 5, 4, 2).reshape(B, G, nb, 3, HPG * QB)
    glt = jnp.concatenate([glt, jnp.zeros((B, G, nb, 5, HPG * QB), F32)], axis=3)

    oct, sb, lst = _nsa_cmp(qt, kc, vct, ovt, pm, glt)
    sb = sb.reshape(B, G, nb, nsel, 1, QB)
    ot = _nsa_sel(lst, qt, ks, vst, kw, vwt, sb, glt, oct)
    o = ot.reshape(B, G, nb, DH, HPG, QB).transpose(0, 2, 5, 1, 4, 3)
    return o.reshape(T, NSA_N_HEADS * DH)


def _moe_tables(route, cnt, T, bm):
    idx = route[:, 0:TOP_K].astype(I32)
    pos = route[:, TOP_K:2 * TOP_K].astype(I32)
    counts = cnt[0, :N_EXPERTS].astype(I32)
    padded = (counts + bm - 1) // bm * bm
    pend = jnp.cumsum(padded)
    pstart = pend - padded
    dest = pstart[idx] + pos
    n_blocks = -(-(T * TOP_K + N_EXPERTS * (bm - 1)) // bm)
    buf_tok = jnp.zeros((n_blocks * bm,), I32).at[dest.reshape(-1)].set(
        jnp.repeat(jnp.arange(T, dtype=I32), TOP_K))
    b_start = jnp.arange(n_blocks, dtype=I32) * bm
    block_e = jnp.minimum(jnp.sum(b_start[:, None] >= pend[None, :], axis=1), N_EXPERTS - 1).astype(I32)
    n_used = (pend[-1] // bm).astype(I32).reshape(1)
    return dest, buf_tok.reshape(n_blocks, 1, bm), block_e, n_used


def kernel(x, ln1_w, w_in, ssd_conv_w, ssd_conv_b, ssd_dt_bias, ssd_a_log, ssd_d, ssd_norm_w, ssd_out_w,
           nsa_q_norm_w, nsa_k_norm_w, cmp_pos_k, cmp_pos_v, cmp_k_w1, cmp_k_w2, cmp_v_w1, cmp_v_w2,
           nsa_out_w, w_out, ln2_w, router_w, router_b, exp_w1, exp_b1, exp_w2, exp_b2):
    B, S, D = x.shape
    T = B * S
    depth = ln1_w.shape[0]
    x2 = x.reshape(T, D)
    G = SSD_N_GROUPS

    tail_src = np.zeros((TAIL_W,), np.int32)
    tail_on = np.zeros((TAIL_W,), bool)
    for g in range(G):
        tail_src[g * LANES:g * LANES + SSD_HPG] = O_DT + SSD_HPG * g + np.arange(SSD_HPG)
        tail_src[g * LANES + 8:g * LANES + 8 + 3 * NSA_HPG] = O_GATE + 3 * NSA_HPG * g + np.arange(3 * NSA_HPG)
        tail_on[g * LANES:g * LANES + 8 + 3 * NSA_HPG] = True
    main_src = np.concatenate([np.arange(0, O_DT), np.arange(O_MERGE, O_MERGE + 2 * D_MODEL),
                               np.arange(O_Q, O_GATE)])

    def group_lanes(v):
        out = jnp.zeros((G, LANES), F32).at[:, :SSD_HPG].set(v.astype(F32).reshape(G, SSD_HPG))
        return out.reshape(1, TAIL_W)

    tm_rows = min(512, T)
    bm = 512
    for l in range(depth):
        w_main = w_in[l][:, main_src].astype(BF16)
        w_tail = jnp.where(tail_on[None, :], w_in[l][:, tail_src], 0.0).astype(BF16)
        main, tail = _inproj(x2, ln1_w[l].reshape(1, D), w_main, w_tail, tm=tm_rows, tn=MAIN_W // 4)
        main3 = main.reshape(B, S, MAIN_W)
        tail3 = tail.reshape(B, S, TAIL_W)

        yssd = _ssd(main3, tail3, ssd_conv_w[l], ssd_conv_b[l].reshape(1, -1),
                    group_lanes(ssd_dt_bias[l]), group_lanes(ssd_a_log[l]),
                    jnp.repeat(ssd_d[l].astype(F32), SSD_HEAD_DIM).reshape(1, -1),
                    ssd_norm_w[l].reshape(1, -1))
        onsa = _nsa(main3, tail3, nsa_q_norm_w[l], nsa_k_norm_w[l], cmp_pos_k[l], cmp_pos_v[l],
                    cmp_k_w1[l], cmp_k_w2[l], cmp_v_w1[l], cmp_v_w2[l])

        rw = jnp.zeros((D, LANES), F32).at[:, :N_EXPERTS].set(router_w[l])
        rwh, rwl = _split_bf16(rw)
        rb = jnp.full((1, LANES), NEG, F32).at[0, :N_EXPERTS].set(router_b[l])
        x1, xn2, route, cnt = _merge(x2, yssd.reshape(T, -1), onsa, main,
                                     ssd_out_w[l].astype(BF16), nsa_out_w[l].astype(BF16), w_out[l].astype(BF16),
                                     ln2_w[l].reshape(1, D), rwh, rwl, rb, tm=min(256, T))

        dest, buf_tok3, block_e, n_used = _moe_tables(route, cnt, T, bm)
        ys = _experts(block_e, n_used, buf_tok3, xn2, _deinterleave(exp_w1[l]),
                      exp_b1[l][:, None, 0::2], exp_b1[l][:, None, 1::2],
                      exp_w2[l].astype(BF16), exp_b2[l][:, None, :], bm)
        tc = min(256, T)
        dest3 = dest.reshape(T // tc, tc, TOP_K).transpose(0, 2, 1).reshape(T // tc, 1, TOP_K * tc)
        x2 = _combine(dest3, route, x1, ys, tc)
    return x2.reshape(B, S, D)
```

```python
import functools
import math

import numpy as np
import jax
import jax.numpy as jnp
from jax import lax
from jax.experimental import pallas as pl
from jax.experimental.pallas import tpu as pltpu

F32 = jnp.float32
BF16 = jnp.bfloat16
I32 = jnp.int32

D_MODEL = 1024
SSD_D_INNER = 2048
SSD_HEAD_DIM = 64
SSD_N_HEADS = 32
SSD_N_GROUPS = 4
SSD_D_STATE = 128
SSD_CHUNK = 256
SSD_HPG = SSD_N_HEADS // SSD_N_GROUPS
NSA_N_HEADS = 16
NSA_N_KV = 4
NSA_HPG = NSA_N_HEADS // NSA_N_KV
NSA_HEAD_DIM = 64
NSA_CMP_LEN = 32
NSA_CMP_STRIDE = 16
NSA_CMP_HIDDEN = 256
NSA_SEL_LEN = 64
NSA_SEL_TOP = 16
NSA_WINDOW = 512
Q_BLOCK = 128
FORCED_SCORE = 1.0e4
N_EXPERTS = 32
TOP_K = 4
SWIGLU_LIMIT = 7.0
SWIGLU_ALPHA = 1.702
EPS = 1e-6

NEG = -1.0e30
LOG2E = 1.4426950408889634
LANES = 128
K_AUG = 80
VMEM_LIMIT = 56 * 1024 * 1024

C_Z, C_XS, C_B, C_C, C_MS, C_MN, C_Q, C_KC, C_VC, C_KS, C_VS, C_KW, C_VW = (
    0, 2048, 4096, 4608, 5120, 6144, 7168, 8192, 8448, 8704, 8960, 9216, 9472)
MAIN_W = 9728
TAIL_W = 512
O_DT, O_Q, O_GATE, O_MERGE = 5120, 5152, 7712, 7760


def _cparams(sem):
    return pltpu.CompilerParams(dimension_semantics=sem, vmem_limit_bytes=VMEM_LIMIT)


def _dot(a, b):
    return jnp.dot(a, b, preferred_element_type=F32)


def _split_bf16(x):
    hi = x.astype(BF16)
    lo = (x - hi.astype(F32)).astype(BF16)
    return hi, lo


def _inproj_kernel(x_ref, lnw_ref, w_ref, wt_ref, main_ref, tail_ref, xn_ref):
    @pl.when(pl.program_id(1) == 0)
    def _():
        x = x_ref[...]
        ms = jnp.mean(x * x, axis=-1, keepdims=True)
        xn = (x * lax.rsqrt(ms + EPS) * lnw_ref[...]).astype(BF16)
        xn_ref[...] = xn
        tail_ref[...] = _dot(xn, wt_ref[...])

    main_ref[...] = _dot(xn_ref[...], w_ref[...]).astype(BF16)


def _inproj(x2, ln_w, w_main, w_tail, tm, tn):
    T = x2.shape[0]
    return pl.pallas_call(
        _inproj_kernel,
        grid=(T // tm, MAIN_W // tn),
        in_specs=[pl.BlockSpec((tm, D_MODEL), lambda i, j: (i, 0)),
                  pl.BlockSpec((1, D_MODEL), lambda i, j: (0, 0)),
                  pl.BlockSpec((D_MODEL, tn), lambda i, j: (0, j)),
                  pl.BlockSpec((D_MODEL, TAIL_W), lambda i, j: (0, 0))],
        out_specs=[pl.BlockSpec((tm, tn), lambda i, j: (i, j)),
                   pl.BlockSpec((tm, TAIL_W), lambda i, j: (i, 0))],
        out_shape=[jax.ShapeDtypeStruct((T, MAIN_W), BF16),
                   jax.ShapeDtypeStruct((T, TAIL_W), F32)],
        scratch_shapes=[pltpu.VMEM((tm, D_MODEL), BF16)],
        compiler_params=_cparams(("arbitrary", "arbitrary")),
        name="inproj",
    )(x2, ln_w, w_main, w_tail)


def _softplus(x):
    return jnp.maximum(x, 0.0) + jnp.log1p(jnp.exp(-jnp.abs(x)))


def _ssd_kernel(z_ref, xs_ref, b_ref, c_ref, dt_ref, cwx_ref, cwb_ref, cwc_ref, cbx_ref, cbb_ref, cbc_ref,
                dtb_ref, alog_ref, dexp_ref, nw_ref, hexp_ref, y_ref, extx, extb, extc, hs_ref):
    L = xs_ref.shape[0]
    P = SSD_HEAD_DIM

    @pl.when(pl.program_id(2) == 0)
    def _():
        extx[0:8, :] = jnp.zeros((8, extx.shape[1]), F32)
        extb[0:8, :] = jnp.zeros((8, extb.shape[1]), F32)
        extc[0:8, :] = jnp.zeros((8, extc.shape[1]), F32)
        hs_ref[...] = jnp.zeros(hs_ref.shape, F32)

    def conv_act(x_ref, w_ref, bias_ref, ext):
        xf = x_ref[...].astype(F32)
        ext[8:8 + L, :] = xf
        acc = bias_ref[...] + w_ref[3:4, :] * xf
        for k in (1, 2, 3):
            acc = acc + w_ref[3 - k:4 - k, :] * ext[8 - k:8 - k + L, :]
        ext[0:8, :] = ext[L:L + 8, :]
        return acc * jax.nn.sigmoid(acc)

    xs = conv_act(xs_ref, cwx_ref, cbx_ref, extx)
    bm = conv_act(b_ref, cwb_ref, cbb_ref, extb)
    cm = conv_act(c_ref, cwc_ref, cbc_ref, extc)

    dt = _softplus(dt_ref[...] + dtb_ref[...])
    da = dt * (-jnp.exp(alog_ref[...]))
    row = lax.broadcasted_iota(I32, (L, L), 0)
    col = lax.broadcasted_iota(I32, (L, L), 1)
    tril = row >= col
    trif = jnp.where(tril, 1.0, 0.0).astype(BF16)
    da_hi, da_lo = _split_bf16(da)
    acum = _dot(trif, da_hi) + _dot(trif, da_lo)
    acum_t = acum.T
    a_end = acum[L - 1:L, :]
    ea = jnp.exp(acum)
    wend = jnp.exp(a_end - acum) * dt
    eend = jnp.exp(a_end)

    cb16 = cm.astype(BF16)
    bt16 = bm.T.astype(BF16)
    cb = _dot(cb16, bt16)

    def head_lanes(v):
        hi, lo = _split_bf16(v)
        return _dot(hi, hexp_ref[...]) + _dot(lo, hexp_ref[...])

    xdt = (xs * head_lanes(dt)).astype(BF16)
    xw = (xs * head_lanes(wend)).astype(BF16)
    y_intra, y_inter = [], []
    for hh in range(SSD_HPG):
        seg = acum[:, hh:hh + 1] - acum_t[hh:hh + 1, :]
        decay = jnp.exp(jnp.where(tril, seg, NEG))
        g = (cb * decay).astype(BF16)
        hprev = hs_ref[hh]
        y_intra.append(_dot(g, xdt[:, hh * P:(hh + 1) * P]))
        y_inter.append(_dot(cb16, hprev.astype(BF16)))
        hs_ref[hh] = hprev * eend[:, hh:hh + 1] + _dot(bt16, xw[:, hh * P:(hh + 1) * P])
    y = jnp.concatenate(y_intra, axis=1) + jnp.concatenate(y_inter, axis=1) * head_lanes(ea)
    y = y + dexp_ref[...] * xs
    z = z_ref[...].astype(F32)
    y = y * (z * jax.nn.sigmoid(z))
    ms = jnp.mean(y * y, axis=-1, keepdims=True)
    y_ref[...] = (y * lax.rsqrt(ms + EPS) * nw_ref[...]).astype(BF16)


def _ssd(main3, tail3, conv_w, conv_b, dtb, alog, dexp, norm_w):
    B, S, _ = main3.shape
    L = math.gcd(S, SSD_CHUNK)
    nc = S // L
    G, N, GW = SSD_N_GROUPS, SSD_D_STATE, SSD_D_INNER // SSD_N_GROUPS
    xs0, b0, c0 = C_XS // GW, C_B // N, C_C // N
    cb0, cc0 = SSD_D_INNER // N, (SSD_D_INNER + G * N) // N
    hexp = jnp.asarray(np.arange(LANES)[:, None] == np.arange(GW)[None, :] // SSD_HEAD_DIM, BF16)

    def seq(w, off):
        return pl.BlockSpec((None, L, w), lambda b, g, c: (b, c, off + g))

    def par(r, w, off):
        return pl.BlockSpec((r, w), lambda b, g, c: (0, off + g))

    return pl.pallas_call(
        _ssd_kernel,
        grid=(B, G, nc),
        in_specs=[seq(GW, 0), seq(GW, xs0), seq(N, b0), seq(N, c0), seq(LANES, 0),
                  par(4, GW, 0), par(4, N, cb0), par(4, N, cc0),
                  par(1, GW, 0), par(1, N, cb0), par(1, N, cc0),
                  par(1, LANES, 0), par(1, LANES, 0), par(1, GW, 0), par(1, GW, 0),
                  pl.BlockSpec((LANES, GW), lambda b, g, c: (0, 0))],
        out_specs=pl.BlockSpec((None, L, GW), lambda b, g, c: (b, c, g)),
        out_shape=jax.ShapeDtypeStruct((B, S, SSD_D_INNER), BF16),
        scratch_shapes=[pltpu.VMEM((L + 8, GW), F32), pltpu.VMEM((L + 8, N), F32), pltpu.VMEM((L + 8, N), F32),
                        pltpu.VMEM((SSD_HPG, N, SSD_HEAD_DIM), F32)],
        compiler_params=_cparams(("arbitrary", "arbitrary", "arbitrary")),
        name="ssd",
    )(main3, main3, main3, main3, tail3, conv_w, conv_w, conv_w, conv_b, conv_b, conv_b,
      dtb, alog, dexp, norm_w, hexp)


def _nsa_norm_kernel(q_ref, ks_ref, kw_ref, vs_ref, vw_ref, bdq_ref, bdk_ref, wq_ref, wk_ref, srow_ref,
                     qt_ref, kso_ref, kwo_ref, vst_ref, vwt_ref):
    QB, DH, G, HPG = Q_BLOCK, NSA_HEAD_DIM, NSA_N_KV, NSA_HPG
    i = pl.program_id(1)

    def head_norm(x_ref, bd_ref, w_ref):
        x = x_ref[...].astype(F32)
        hi, lo = _split_bf16(x * x)
        ms = (_dot(hi, bd_ref[...]) + _dot(lo, bd_ref[...])) * (1.0 / DH)
        return x * lax.rsqrt(ms + EPS) * w_ref[...]

    qn_t = head_norm(q_ref, bdq_ref, wq_ref).T
    pos = i * QB + lax.broadcasted_iota(I32, (QB, K_AUG - DH), 0)
    lane = lax.broadcasted_iota(I32, (QB, K_AUG - DH), 1)
    lo = pos & 255
    pcols = jnp.where(lane >= 4, 0, jnp.where((lane & 1) == 0, lo, pos - lo)).astype(F32)
    ksn = head_norm(ks_ref, bdk_ref, wk_ref)
    kwn = head_norm(kw_ref, bdk_ref, wk_ref)
    vs_t = vs_ref[...].astype(F32).T
    vw_t = vw_ref[...].astype(F32).T
    for g in range(G):
        heads = [qn_t[(g * HPG + h) * DH:(g * HPG + h + 1) * DH, :] for h in range(HPG)]
        qt_ref[g] = jnp.concatenate([jnp.concatenate(heads, axis=1).astype(BF16), srow_ref[g]], axis=0)
        kso_ref[g] = jnp.concatenate([ksn[:, g * DH:(g + 1) * DH], pcols], axis=1).astype(BF16)
        kwo_ref[g] = jnp.concatenate([kwn[:, g * DH:(g + 1) * DH], pcols], axis=1).astype(BF16)
        vst_ref[g] = vs_t[g * DH:(g + 1) * DH, :].astype(BF16)
        vwt_ref[g] = vw_t[g * DH:(g + 1) * DH, :].astype(BF16)


def _nsa_norm(main3, bdq, bdk, wq, wk, srow):
    B, S, _ = main3.shape
    QB, DH, G = Q_BLOCK, NSA_HEAD_DIM, NSA_N_KV
    nb = S // QB
    QW, KW, W = NSA_N_HEADS * DH, G * DH, NSA_HPG * QB
    const = lambda shape: pl.BlockSpec(shape, lambda b, i: (0,) * len(shape))
    col = lambda w, c0: pl.BlockSpec((None, QB, w), lambda b, i: (b, i, c0 // w))
    return pl.pallas_call(
        _nsa_norm_kernel,
        grid=(B, nb),
        in_specs=[col(QW, C_Q), col(KW, C_KS), col(KW, C_KW), col(KW, C_VS), col(KW, C_VW),
                  const((QW, QW)), const((KW, KW)), const((1, QW)), const((1, KW)), const(srow.shape)],
        out_specs=[pl.BlockSpec((None, G, None, K_AUG, W), lambda b, i: (b, 0, i, 0, 0)),
                   pl.BlockSpec((None, G, QB, K_AUG), lambda b, i: (b, 0, i, 0)),
                   pl.BlockSpec((None, G, QB, K_AUG), lambda b, i: (b, 0, i, 0)),
                   pl.BlockSpec((None, G, None, DH, QB), lambda b, i: (b, 0, i, 0, 0)),
                   pl.BlockSpec((None, G, None, DH, QB), lambda b, i: (b, 0, i, 0, 0))],
        out_shape=[jax.ShapeDtypeStruct((B, G, nb, K_AUG, W), BF16),
                   jax.ShapeDtypeStruct((B, G, S, K_AUG), BF16), jax.ShapeDtypeStruct((B, G, S, K_AUG), BF16),
                   jax.ShapeDtypeStruct((B, G, nb, DH, QB), BF16), jax.ShapeDtypeStruct((B, G, nb, DH, QB), BF16)],
        compiler_params=_cparams(("arbitrary", "arbitrary")),
        name="nsa_norm_layout",
    )(main3, main3, main3, main3, main3, bdq, bdk, wq, wk, srow)


def _compress_kernel(u_ref, pos_ref, w1_ref, w2_ref, nw_ref, o_ref, *, normalize):
    half = u_ref.shape[1]
    nrow = u_ref.shape[0]
    u = u_ref[...]
    a = _dot(u, w1_ref[0:half, :])
    b = _dot(u, w1_ref[half:2 * half, :])
    posc = _dot(pos_ref[...], w1_ref[...])[0:1, :]
    pre = a + pltpu.roll(b, nrow - 1, 0) + posc
    act = 0.5 * pre * (1.0 + jnp.tanh(math.sqrt(2.0 / math.pi) * (pre + 0.044715 * (pre * pre * pre))))
    o = _dot(act.astype(BF16), w2_ref[...])
    if normalize:
        ms = jnp.mean(o * o, axis=-1, keepdims=True)
        o = o * lax.rsqrt(ms + EPS) * nw_ref[...]
    o_ref[...] = o


def _compress(u, pos8, w1, w2, nw, normalize):
    BG, nrow, half = u.shape
    const = lambda shape: pl.BlockSpec(shape, lambda i: (0, 0))
    return pl.pallas_call(
        functools.partial(_compress_kernel, normalize=normalize),
        grid=(BG,),
        in_specs=[pl.BlockSpec((None, nrow, half), lambda i: (i, 0, 0)),
                  const(pos8.shape), const(w1.shape), const(w2.shape), const(nw.shape)],
        out_specs=pl.BlockSpec((None, nrow, NSA_HEAD_DIM), lambda i: (i, 0, 0)),
        out_shape=jax.ShapeDtypeStruct((BG, nrow, NSA_HEAD_DIM), F32),
        compiler_params=_cparams(("arbitrary",)),
        name="nsa_compress_norm" if normalize else "nsa_compress",
    )(u, pos8, w1, w2, nw)


def _nsa_cmp_kernel(qt_ref, kc_ref, vct_ref, ovt_ref, pm_ref, glt_ref, oct_ref, sb_ref, lst_ref, imp_scr):
    i = pl.program_id(2)
    ncmp = kc_ref.shape[0]
    nsel = ovt_ref.shape[0]
    QB = Q_BLOCK
    W = NSA_HPG * QB
    CH = min(LANES, ncmp)
    span = CH * NSA_CMP_STRIDE
    nch = jnp.minimum((i * QB + QB - NSA_CMP_LEN) // span + 1, ncmp // CH)

    def attend(rows):
        s = _dot(kc_ref[0:rows, :], qt_ref[...])
        jrow = lax.broadcasted_iota(I32, (rows, W), 0)
        lane = lax.broadcasted_iota(I32, (rows, W), 1)
        t = i * QB + (lane & (QB - 1))
        mask = (NSA_CMP_STRIDE * jrow + (NSA_CMP_LEN - 1)) <= t
        sm = jnp.where(mask, s, NEG)
        m = jnp.max(sm, axis=0, keepdims=True)
        p = jnp.where(mask, jnp.exp2(sm - m), 0.0)
        l = jnp.sum(p, axis=0, keepdims=True)
        pn = p * (1.0 / jnp.maximum(l, 1e-30))
        oct_ref[...] = _dot(vct_ref[:, 0:rows], pn.astype(BF16)) * jax.nn.sigmoid(glt_ref[0:1, :])
        psum = pn[:, 0:QB]
        for h in range(1, NSA_HPG):
            psum = psum + pn[:, h * QB:(h + 1) * QB]
        imp_scr[...] = _dot(ovt_ref[:, 0:rows], psum.astype(BF16))

    for k in range(1, ncmp // CH + 1):
        pl.when(nch == k)(functools.partial(attend, k * CH))
    imp = imp_scr[...]

    jf = lax.broadcasted_iota(I32, (nsel, QB), 0).astype(F32)
    tq = (i * QB + lax.broadcasted_iota(I32, (nsel, QB), 1))
    cur = (tq >> 6).astype(F32)
    forced = (jf == 0.0) | (jf == cur) | (jf == cur - 1.0)
    valid = jf <= cur
    v0 = jnp.where(forced, FORCED_SCORE, jnp.where(valid, imp, -1.0))

    def pick_one(_, carry):
        v, sel = carry
        mx = jnp.max(v, axis=0, keepdims=True)
        first = jnp.min(jnp.where(v == mx, jf, float(nsel)), axis=0, keepdims=True)
        pick = jf == first
        return jnp.where(pick, -2.0, v), jnp.where(pick, 1.0, sel)

    _, sel = lax.fori_loop(0, min(NSA_SEL_TOP, nsel), pick_one, (v0, jnp.zeros((nsel, QB), F32)))
    sel = jnp.where(valid, sel, 0.0)
    sb_ref[...] = jnp.where(sel > 0.0, 0.0, NEG)

    npair = nsel // 2
    pairsel = _dot(pm_ref[...], sel.astype(BF16))
    jp = lax.broadcasted_iota(I32, (npair, LANES), 0)
    need = (jnp.sum(pairsel, axis=1, keepdims=True) > 0.0) & (jp < i)
    needf = jnp.where(need, 1.0, 0.0)
    r2 = lax.broadcasted_iota(I32, (npair, npair), 0)
    c2 = lax.broadcasted_iota(I32, (npair, npair), 1)
    tri = jnp.where(c2 <= r2, 1.0, 0.0).astype(BF16)
    prefix = _dot(tri, needf.astype(BF16))
    slot = lax.broadcasted_iota(I32, (npair, LANES), 1).astype(F32)
    onehot = jnp.where(need & (prefix == slot + 1.0), 1.0, 0.0).astype(BF16)
    jpv = lax.broadcasted_iota(I32, (8, npair), 1).astype(F32).astype(BF16)
    lst = _dot(jpv, onehot)
    cnt = _dot(jnp.ones((8, npair), BF16), needf.astype(BF16))
    r8 = lax.broadcasted_iota(I32, (8, LANES), 0)
    lst_ref[...] = jnp.where(r8 == 0, lst, cnt).astype(I32)


def _nsa_cmp(qt, kc, vct, ovt, pm, glt):
    B, G, nb, _, W = qt.shape
    ncmp = kc.shape[2]
    nsel = ovt.shape[0]
    blk = lambda *shape: pl.BlockSpec((None, None, None) + shape, lambda b, g, i: (b, g, i) + (0,) * len(shape))
    per_bg = lambda *shape: pl.BlockSpec((None, None) + shape, lambda b, g, i: (b, g) + (0,) * len(shape))
    const = lambda shape: pl.BlockSpec(shape, lambda b, g, i: (0,) * len(shape))
    return pl.pallas_call(
        _nsa_cmp_kernel,
        grid=(B, G, nb),
        in_specs=[blk(K_AUG, W), per_bg(ncmp, K_AUG), per_bg(NSA_HEAD_DIM, ncmp),
                  const(ovt.shape), const(pm.shape), blk(8, W)],
        out_specs=[blk(NSA_HEAD_DIM, W), blk(nsel, Q_BLOCK), blk(8, LANES)],
        out_shape=[jax.ShapeDtypeStruct((B, G, nb, NSA_HEAD_DIM, W), F32),
                   jax.ShapeDtypeStruct((B, G, nb, nsel, Q_BLOCK), F32),
                   jax.ShapeDtypeStruct((B, G, nb, 8, LANES), I32)],
        scratch_shapes=[pltpu.VMEM((nsel, Q_BLOCK), F32)],
        compiler_params=_cparams(("arbitrary", "arbitrary", "arbitrary")),
        name="nsa_cmp_select",
    )(qt, kc, vct, ovt, pm, glt)


def _tile4(r):
    return jnp.concatenate([r] * NSA_HPG, axis=1)


def _nsa_sel_kernel(lst_ref, qt_ref, ks_ref, vst_ref, kw_ref, vwt_ref, sb_ref, glt_ref, oct_ref, o_ref,
                    s_scr, d_scr, ow_scr):
    i = pl.program_id(2)
    QB = Q_BLOCK
    W = NSA_HPG * QB
    HB = NSA_SEL_LEN
    qt = qt_ref[...]
    count = lst_ref[1, 0]

    def flash(s, vt, carry):
        m, l, acc = carry
        mn = jnp.maximum(m, jnp.max(s, axis=0, keepdims=True))
        alpha = jnp.exp2(m - mn)
        p = jnp.exp2(s - mn)
        l = alpha * l + jnp.sum(p, axis=0, keepdims=True)
        acc = alpha * acc + _dot(vt, p.astype(BF16))
        return mn, l, acc

    def pair_scores(k):
        jp = lst_ref[0, k]
        live = k < count
        s = _dot(ks_ref[pl.ds(pl.multiple_of(jp * QB, QB), QB), :], qt)
        b0 = jnp.where(live, _tile4(sb_ref[pl.ds(2 * jp, 1), :]), NEG)
        b1 = jnp.where(live, _tile4(sb_ref[pl.ds(2 * jp + 1, 1), :]), NEG)
        return jnp.concatenate([s[0:HB] + b0, s[HB:2 * HB] + b1], axis=0)

    def scores(it):
        return jnp.concatenate([pair_scores(2 * it), pair_scores(2 * it + 1)], axis=0)

    s_scr[0] = scores(0)

    r = lax.broadcasted_iota(I32, (QB, W), 0)
    c = lax.broadcasted_iota(I32, (QB, W), 1) & (QB - 1)
    kd = ks_ref[pl.ds(pl.multiple_of(i * QB, QB), QB), :]
    d_scr[...] = jnp.where(r <= c, _dot(kd, qt), NEG)

    ss, vts = [], []
    for w in range(NSA_WINDOW // QB + 1):
        pw = i - NSA_WINDOW // QB + w
        pc = jnp.maximum(pw, 0)
        kk = kw_ref[pl.ds(pl.multiple_of(pc * QB, QB), QB), :]
        s = _dot(kk, qt)
        if w == 0:
            s = jnp.where((r > c) & (pw >= 0), s, NEG)
        elif w == NSA_WINDOW // QB:
            s = jnp.where(r <= c, s, NEG)
        else:
            s = jnp.where(pw >= 0, s, NEG)
        ss.append(s)
        vts.append(vwt_ref[pc])
    s = jnp.concatenate(ss, axis=0)
    mw = jnp.max(s, axis=0, keepdims=True)
    p = jnp.exp2(s - mw)
    lw = jnp.sum(p, axis=0, keepdims=True)
    o_win = _dot(jnp.concatenate(vts, axis=1), p.astype(BF16)) * (1.0 / lw)
    ow_scr[...] = oct_ref[...] + jax.nn.sigmoid(glt_ref[2:3, :]) * o_win

    def body(it, carry):
        slot = it & 1
        s = s_scr[slot]
        vt = jnp.concatenate([vst_ref[lst_ref[0, 2 * it]], vst_ref[lst_ref[0, 2 * it + 1]]], axis=1)
        s_scr[1 - slot] = scores(it + 1)
        return flash(s, vt, carry)

    init = (jnp.full((1, W), NEG, F32), jnp.zeros((1, W), F32), jnp.zeros((NSA_HEAD_DIM, W), F32))
    carry = lax.fori_loop(0, (count + 1) // 2, body, init)
    m, l, acc = flash(d_scr[...], vst_ref[i], carry)
    o_t = ow_scr[...] + jax.nn.sigmoid(glt_ref[1:2, :]) * (acc * (1.0 / l))
    o_ref[...] = jnp.concatenate([o_t[:, h * QB:(h + 1) * QB].T for h in range(NSA_HPG)], axis=1).astype(BF16)


def _nsa_sel(lst, qt, ks, vst, kw, vwt, sb, glt, oct):
    B, G, nb, _, W = qt.shape
    S = ks.shape[2]
    npair = vst.shape[2]
    nsel = sb.shape[3]
    blk = lambda *shape: pl.BlockSpec((None, None, None) + shape, lambda b, g, i: (b, g, i) + (0,) * len(shape))
    per_bg = lambda *shape: pl.BlockSpec((None, None) + shape, lambda b, g, i: (b, g) + (0,) * len(shape))
    return pl.pallas_call(
        _nsa_sel_kernel,
        grid=(B, G, nb),
        in_specs=[pl.BlockSpec((None, None, None, 8, LANES), lambda b, g, i: (b, g, i, 0, 0),
                               memory_space=pltpu.SMEM),
                  blk(K_AUG, W), per_bg(S, K_AUG), per_bg(npair, NSA_HEAD_DIM, Q_BLOCK),
                  per_bg(S, K_AUG), per_bg(npair, NSA_HEAD_DIM, Q_BLOCK),
                  blk(nsel, Q_BLOCK), blk(8, W), blk(NSA_HEAD_DIM, W)],
        out_specs=pl.BlockSpec((None, Q_BLOCK, NSA_HPG * NSA_HEAD_DIM), lambda b, g, i: (b, i, g)),
        out_shape=jax.ShapeDtypeStruct((B, S, NSA_N_HEADS * NSA_HEAD_DIM), BF16),
        scratch_shapes=[pltpu.VMEM((2, 2 * Q_BLOCK, W), F32), pltpu.VMEM((Q_BLOCK, W), F32),
                        pltpu.VMEM((NSA_HEAD_DIM, W), F32)],
        compiler_params=_cparams(("arbitrary", "arbitrary", "arbitrary")),
        name="nsa_select_window",
    )(lst, qt, ks, vst, kw, vwt, sb, glt, oct)


ROW_TILE = 8


def _store_row_tiles(ref, x):
    n = x.shape[0]
    for c in range(ROW_TILE):
        ref[pl.ds(c, n, stride=ROW_TILE), :] = x[:, c * LANES:(c + 1) * LANES]


def _load_row_tiles(ref, start, n):
    return jnp.concatenate([ref[pl.ds(start + c, n, stride=ROW_TILE), :] for c in range(ROW_TILE)], axis=1)


def _merge_kernel(x_ref, ys_ref, on_ref, gs_ref, gn_ref, wssd_ref, wnsa_ref, wout_ref, ln2_ref,
                  rwh_ref, rwl_ref, rb_ref, x1_ref, xn2_ref, route_ref, cnt_ref, base_ref):
    tm = x_ref.shape[0]

    @pl.when(pl.program_id(0) == 0)
    def _():
        base_ref[...] = jnp.zeros(base_ref.shape, F32)

    y_ssd = _dot(ys_ref[...], wssd_ref[...])
    y_nsa = _dot(on_ref[...], wnsa_ref[...])
    merged = (jax.nn.sigmoid(gs_ref[...].astype(F32)) * y_ssd
              + jax.nn.sigmoid(gn_ref[...].astype(F32)) * y_nsa)
    x1 = x_ref[...] + _dot(merged.astype(BF16), wout_ref[...])
    x1_ref[...] = x1
    ms = jnp.mean(x1 * x1, axis=-1, keepdims=True)
    xn2 = x1 * lax.rsqrt(ms + EPS) * ln2_ref[...]
    _store_row_tiles(xn2_ref, xn2)

    xh, xl = _split_bf16(xn2)
    logits = _dot(xh, rwh_ref[...]) + _dot(xl, rwh_ref[...]) + _dot(xh, rwl_ref[...]) + rb_ref[...]
    lane = lax.broadcasted_iota(I32, (tm, LANES), 1)
    lanef = lane.astype(F32)
    v = logits
    onehot = jnp.zeros((tm, LANES), F32)
    vals, picks = [], []
    for _ in range(TOP_K):
        mx = jnp.max(v, axis=-1, keepdims=True)
        first = jnp.min(jnp.where(v == mx, lanef, float(LANES)), axis=-1, keepdims=True)
        pick = lanef == first
        v = jnp.where(pick, 2.0 * NEG, v)
        onehot = jnp.where(pick, 1.0, onehot)
        vals.append(mx)
        picks.append((pick, first))
    es = [jnp.exp(val - vals[0]) for val in vals]
    inv = 1.0 / (es[0] + es[1] + es[2] + es[3])

    r2 = lax.broadcasted_iota(I32, (tm, tm), 0)
    c2 = lax.broadcasted_iota(I32, (tm, tm), 1)
    stril = jnp.where(c2 < r2, 1.0, 0.0).astype(BF16)
    posmap = base_ref[...] + _dot(stril, onehot.astype(BF16))
    route = jnp.zeros((tm, LANES), F32)
    for k in range(TOP_K):
        pick, first = picks[k]
        pos = jnp.sum(jnp.where(pick, posmap, 0.0), axis=-1, keepdims=True)
        route = jnp.where(lane == k, first, route)
        route = jnp.where(lane == TOP_K + k, pos, route)
        route = jnp.where(lane == 2 * TOP_K + k, es[k] * inv, route)
    route_ref[...] = route
    base = base_ref[...] + jnp.sum(onehot, axis=0, keepdims=True)
    base_ref[...] = base
    cnt_ref[...] = jnp.broadcast_to(base, cnt_ref.shape)


def _merge(x2, yssd, onsa, main, wssd, wnsa, wout, ln2, rwh, rwl, rb, tm):
    T = x2.shape[0]
    const = lambda shape: pl.BlockSpec(shape, lambda i: (0, 0))
    row = lambda w, j=0: pl.BlockSpec((tm, w), lambda i: (i, j))
    return pl.pallas_call(
        _merge_kernel,
        grid=(T // tm,),
        in_specs=[row(D_MODEL), row(SSD_D_INNER), row(D_MODEL), row(D_MODEL, C_MS // D_MODEL),
                  row(D_MODEL, C_MN // D_MODEL),
                  const(wssd.shape), const(wnsa.shape), const(wout.shape), const(ln2.shape),
                  const(rwh.shape), const(rwl.shape), const(rb.shape)],
        out_specs=[row(D_MODEL), pl.BlockSpec((tm * ROW_TILE, LANES), lambda i: (i, 0)), row(LANES),
                   const((8, LANES))],
        out_shape=[jax.ShapeDtypeStruct((T, D_MODEL), F32), jax.ShapeDtypeStruct((T * ROW_TILE, LANES), F32),
                   jax.ShapeDtypeStruct((T, LANES), F32), jax.ShapeDtypeStruct((8, LANES), F32)],
        scratch_shapes=[pltpu.VMEM((1, LANES), F32)],
        compiler_params=_cparams(("arbitrary",)),
        name="merge_router",
    )(x2, yssd, onsa, main, main, wssd, wnsa, wout, ln2, rwh, rwl, rb)


def _expert_kernel(be_ref, nu_ref, tok_ref, tokn_ref, xn_hbm, w1g_ref, w1l_ref, b1g_ref, b1l_ref, w2_ref, b2_ref,
                   ys_ref, xbuf, sem):
    RT = ROW_TILE
    bm = xbuf.shape[1] // RT
    blk = pl.program_id(0)
    slot = blk & 1

    def gather(tok_smem, s):
        def issue(r, c):
            src = pl.multiple_of(tok_smem[0, r] * RT, RT)
            dst = pl.multiple_of(r * RT, RT)
            pltpu.make_async_copy(xn_hbm.at[pl.ds(src, RT)], xbuf.at[s, pl.ds(dst, RT)], sem.at[s]).start()
            return c

        lax.fori_loop(0, bm, issue, 0, unroll=8)

    @pl.when(blk == 0)
    def _():
        gather(tok_ref, 0)

    for s in (0, 1):
        @pl.when((blk + 1 < nu_ref[0]) & (slot == 1 - s))
        def _():
            gather(tokn_ref, s)

    @pl.when(blk < nu_ref[0])
    def _():
        pltpu.make_async_copy(xn_hbm.at[pl.ds(0, bm * RT)], xbuf.at[slot], sem.at[slot]).wait()
        x = _load_row_tiles(xbuf.at[slot], 0, bm).astype(BF16)
        glu = jnp.minimum(_dot(x, w1g_ref[...]) + b1g_ref[...], SWIGLU_LIMIT)
        lin = jnp.clip(_dot(x, w1l_ref[...]) + b1l_ref[...], -SWIGLU_LIMIT, SWIGLU_LIMIT)
        act = glu * jax.nn.sigmoid(SWIGLU_ALPHA * glu) * (lin + 1.0)
        _store_row_tiles(ys_ref, _dot(act.astype(BF16), w2_ref[...]) + b2_ref[...])

    @pl.when(blk >= nu_ref[0])
    def _():
        ys_ref[...] = jnp.zeros(ys_ref.shape, F32)


def _experts(block_e, n_used, buf_tok3, xn2, w1p, b1g, b1l, w2, b2, bm):
    n_blocks = buf_tok3.shape[0]
    DE = w2.shape[1]
    wspec = lambda r, c, j=0: pl.BlockSpec((None, r, c), lambda b, be, nu: (be[b], 0, j))
    return pl.pallas_call(
        _expert_kernel,
        grid_spec=pltpu.PrefetchScalarGridSpec(
            num_scalar_prefetch=2,
            grid=(n_blocks,),
            in_specs=[pl.BlockSpec((None, 1, bm), lambda b, be, nu: (b, 0, 0), memory_space=pltpu.SMEM),
                      pl.BlockSpec((None, 1, bm), lambda b, be, nu: (jnp.minimum(b + 1, n_blocks - 1), 0, 0),
                                   memory_space=pltpu.SMEM),
                      pl.BlockSpec(memory_space=pl.ANY),
                      wspec(D_MODEL, DE, 0), wspec(D_MODEL, DE, 1), wspec(1, DE), wspec(1, DE),
                      wspec(DE, D_MODEL), wspec(1, D_MODEL)],
            out_specs=pl.BlockSpec((bm * ROW_TILE, LANES), lambda b, be, nu: (b, 0)),
            scratch_shapes=[pltpu.VMEM((2, bm * ROW_TILE, LANES), F32), pltpu.SemaphoreType.DMA((2,))]),
        out_shape=jax.ShapeDtypeStruct((n_blocks * bm * ROW_TILE, LANES), F32),
        compiler_params=_cparams(("arbitrary",)),
        name="moe_experts",
    )(block_e, n_used, buf_tok3, buf_tok3, xn2, w1p, w1p, b1g, b1l, w2, b2)


def _deinterleave_kernel(w_ref, p_ref, o_ref):
    o_ref[...] = _dot(w_ref[...].astype(BF16), p_ref[...]).astype(BF16)


def _deinterleave(w1):
    E, D, W = w1.shape
    src = np.concatenate([np.arange(0, W, 2), np.arange(1, W, 2)])
    perm = jnp.asarray(np.arange(W)[:, None] == src[None, :], BF16)
    rows = D // 2
    return pl.pallas_call(
        _deinterleave_kernel,
        grid=(E, D // rows),
        in_specs=[pl.BlockSpec((None, rows, W), lambda e, r: (e, r, 0)),
                  pl.BlockSpec((W, W), lambda e, r: (0, 0))],
        out_specs=pl.BlockSpec((None, rows, W), lambda e, r: (e, r, 0)),
        out_shape=jax.ShapeDtypeStruct((E, D, W), BF16),
        compiler_params=_cparams(("arbitrary", "arbitrary")),
        name="moe_w1_deinterleave",
    )(w1, perm)


def _combine_kernel(dest_ref, destn_ref, route_ref, x1_ref, ys_hbm, o_ref, buf, sem):
    tc = x1_ref.shape[0]
    i = pl.program_id(0)
    slot = i & 1

    RT = ROW_TILE

    def gather(dest_smem, s):
        def issue(r, c):
            src = pl.multiple_of(dest_smem[0, r] * RT, RT)
            dst = pl.multiple_of(r * RT, RT)
            pltpu.make_async_copy(ys_hbm.at[pl.ds(src, RT)], buf.at[s, pl.ds(dst, RT)], sem.at[s]).start()
            return c

        lax.fori_loop(0, TOP_K * tc, issue, 0, unroll=8)

    @pl.when(i == 0)
    def _():
        gather(dest_ref, 0)

    for s in (0, 1):
        @pl.when((i + 1 < pl.num_programs(0)) & (slot == 1 - s))
        def _():
            gather(destn_ref, s)

    pltpu.make_async_copy(ys_hbm.at[pl.ds(0, TOP_K * tc * RT)], buf.at[slot], sem.at[slot]).wait()
    route = route_ref[...]
    acc = x1_ref[...]
    for k in range(TOP_K):
        acc = acc + route[:, 2 * TOP_K + k:2 * TOP_K + k + 1] * _load_row_tiles(buf.at[slot], k * tc * RT, tc)
    o_ref[...] = acc


def _combine(dest3, route, x1, ys, tc):
    T = x1.shape[0]
    n = T // tc
    return pl.pallas_call(
        _combine_kernel,
        grid=(n,),
        in_specs=[pl.BlockSpec((None, 1, TOP_K * tc), lambda i: (i, 0, 0), memory_space=pltpu.SMEM),
                  pl.BlockSpec((None, 1, TOP_K * tc), lambda i: (jnp.minimum(i + 1, n - 1), 0, 0),
                               memory_space=pltpu.SMEM),
                  pl.BlockSpec((tc, LANES), lambda i: (i, 0)),
                  pl.BlockSpec((tc, D_MODEL), lambda i: (i, 0)),
                  pl.BlockSpec(memory_space=pl.ANY)],
        out_specs=pl.BlockSpec((tc, D_MODEL), lambda i: (i, 0)),
        out_shape=jax.ShapeDtypeStruct((T, D_MODEL), F32),
        scratch_shapes=[pltpu.VMEM((2, TOP_K * tc * ROW_TILE, LANES), F32), pltpu.SemaphoreType.DMA((2,))],
        compiler_params=_cparams(("arbitrary",)),
        name="moe_combine",
    )(dest3, dest3, route, x1, ys)


def _split_pos(pos):
    lo = pos % 256
    return lo.astype(np.float32), (pos - lo).astype(np.float32)


def _aug_keys(k, pos):
    lo, hi = _split_pos(pos)
    cols = np.stack([lo, hi, lo, hi] + [np.zeros_like(lo)] * (K_AUG - NSA_HEAD_DIM - 4), axis=-1)
    cols = jnp.broadcast_to(jnp.asarray(cols, BF16), k.shape[:-1] + (cols.shape[-1],))
    return jnp.concatenate([k, cols], axis=-1)


def _slope_rows():
    slopes = (2.0 ** (-8.0 * np.arange(1, NSA_N_HEADS + 1) / NSA_N_HEADS)).astype(np.float32)
    sl = jnp.asarray(np.repeat((slopes * np.float32(LOG2E)).reshape(NSA_N_KV, NSA_HPG), Q_BLOCK, axis=1), F32)
    hi, lo = _split_bf16(sl)
    zero = jnp.zeros_like(hi)
    return jnp.stack([hi, hi, lo, lo] + [zero] * (K_AUG - NSA_HEAD_DIM - 4), axis=1)


def _nsa(main3, tail3, q_norm_w, k_norm_w, cmp_pos_k, cmp_pos_v, cmp_k_w1, cmp_k_w2, cmp_v_w1, cmp_v_w2):
    B, S, _ = main3.shape
    T = B * S
    G, HPG, DH, QB = NSA_N_KV, NSA_HPG, NSA_HEAD_DIM, Q_BLOCK
    nb, npair, nsel, ncmp = S // QB, S // QB, S // NSA_SEL_LEN, S // NSA_CMP_STRIDE
    main = main3.reshape(T, MAIN_W)

    hd = np.arange(NSA_N_HEADS * DH) // DH
    bdq = jnp.asarray(hd[:, None] == hd[None, :], BF16)
    bdk = bdq[:G * DH, :G * DH]
    wq = (jnp.tile(q_norm_w.astype(F32), NSA_N_HEADS) * (DH ** -0.5 * LOG2E)).reshape(1, -1)
    wk = jnp.tile(k_norm_w.astype(F32), G).reshape(1, -1)
    qt, ks, kw, vst, vwt = _nsa_norm(main3, bdq, bdk, wq, wk, _slope_rows())

    def kv_heads(t):
        return t.reshape(B, S, G, DH).transpose(0, 2, 1, 3)

    def halves(c0):
        return kv_heads(main[:, c0:c0 + G * DH]).reshape(B * G, ncmp, NSA_CMP_STRIDE * DH)

    def pos8(pos):
        return jnp.zeros((8, NSA_CMP_LEN * DH), F32).at[0].set(pos.reshape(-1)).astype(BF16)

    ones = jnp.ones((1, DH), F32)
    kc = _compress(halves(C_KC), pos8(cmp_pos_k), cmp_k_w1.astype(BF16), cmp_k_w2.astype(BF16),
                   k_norm_w.astype(F32).reshape(1, DH), True)
    vc = _compress(halves(C_VC), pos8(cmp_pos_v), cmp_v_w1.astype(BF16), cmp_v_w2.astype(BF16), ones, False)
    cmp_end = np.arange(ncmp) * NSA_CMP_STRIDE + NSA_CMP_LEN - 1
    kc = _aug_keys(kc.astype(BF16).reshape(B, G, ncmp, DH), cmp_end)
    vct = vc.astype(BF16).reshape(B, G, ncmp, DH).transpose(0, 1, 3, 2)

    c_start = (np.arange(ncmp) * NSA_CMP_STRIDE)[:, None]
    s_start = (np.arange(nsel) * NSA_SEL_LEN)[None, :]
    overlap = np.clip(np.minimum(c_start + NSA_CMP_LEN, s_start + NSA_SEL_LEN)
                      - np.maximum(c_start, s_start), 0, None) / NSA_CMP_LEN
    overlap[(S - NSA_CMP_LEN) // NSA_CMP_STRIDE + 1:] = 0.0
    ovt = jnp.asarray(overlap.T, BF16)
    pm = jnp.asarray(np.arange(nsel)[None, :] // 2 == np.arange(nsel // 2)[:, None], BF16)

    gl = tail3.reshape(B, nb, QB, G, LANES)[..., 8:8 + 3 * HPG].reshape(B, nb, QB, G, HPG, 3)
    glt = gl.transpose(0, 3, 1, 5, 4, 2).reshape(B, G, nb, 3, HPG * QB)
    glt = jnp.concatenate([glt, jnp.zeros((B, G, nb, 5, HPG * QB), F32)], axis=3)

    oct, sb, lst = _nsa_cmp(qt, kc, vct, ovt, pm, glt)
    return _nsa_sel(lst, qt, ks, vst, kw, vwt, sb, glt, oct).reshape(T, NSA_N_HEADS * DH)


def _moe_tables(route, cnt, T, bm):
    idx = route[:, 0:TOP_K].astype(I32)
    pos = route[:, TOP_K:2 * TOP_K].astype(I32)
    counts = cnt[0, :N_EXPERTS].astype(I32)
    padded = (counts + bm - 1) // bm * bm
    pend = jnp.cumsum(padded)
    pstart = pend - padded
    dest = pstart[idx] + pos
    n_blocks = -(-(T * TOP_K + N_EXPERTS * (bm - 1)) // bm)
    buf_tok = jnp.zeros((n_blocks * bm,), I32).at[dest.reshape(-1)].set(
        jnp.repeat(jnp.arange(T, dtype=I32), TOP_K))
    b_start = jnp.arange(n_blocks, dtype=I32) * bm
    block_e = jnp.minimum(jnp.sum(b_start[:, None] >= pend[None, :], axis=1), N_EXPERTS - 1).astype(I32)
    n_used = (pend[-1] // bm).astype(I32).reshape(1)
    return dest, buf_tok.reshape(n_blocks, 1, bm), block_e, n_used


def kernel(x, ln1_w, w_in, ssd_conv_w, ssd_conv_b, ssd_dt_bias, ssd_a_log, ssd_d, ssd_norm_w, ssd_out_w,
           nsa_q_norm_w, nsa_k_norm_w, cmp_pos_k, cmp_pos_v, cmp_k_w1, cmp_k_w2, cmp_v_w1, cmp_v_w2,
           nsa_out_w, w_out, ln2_w, router_w, router_b, exp_w1, exp_b1, exp_w2, exp_b2):
    B, S, D = x.shape
    T = B * S
    depth = ln1_w.shape[0]
    x2 = x.reshape(T, D)
    G = SSD_N_GROUPS

    tail_src = np.zeros((TAIL_W,), np.int32)
    tail_on = np.zeros((TAIL_W,), bool)
    for g in range(G):
        tail_src[g * LANES:g * LANES + SSD_HPG] = O_DT + SSD_HPG * g + np.arange(SSD_HPG)
        tail_src[g * LANES + 8:g * LANES + 8 + 3 * NSA_HPG] = O_GATE + 3 * NSA_HPG * g + np.arange(3 * NSA_HPG)
        tail_on[g * LANES:g * LANES + 8 + 3 * NSA_HPG] = True
    main_src = np.concatenate([np.arange(0, O_DT), np.arange(O_MERGE, O_MERGE + 2 * D_MODEL),
                               np.arange(O_Q, O_GATE)])

    def group_lanes(v):
        out = jnp.zeros((G, LANES), F32).at[:, :SSD_HPG].set(v.astype(F32).reshape(G, SSD_HPG))
        return out.reshape(1, TAIL_W)

    tm_rows = min(1024, T)
    bm = 512
    for l in range(depth):
        w_main = w_in[l][:, main_src].astype(BF16)
        w_tail = jnp.where(tail_on[None, :], w_in[l][:, tail_src], 0.0).astype(BF16)
        main, tail = _inproj(x2, ln1_w[l].reshape(1, D), w_main, w_tail, tm=tm_rows, tn=MAIN_W // 4)
        main3 = main.reshape(B, S, MAIN_W)
        tail3 = tail.reshape(B, S, TAIL_W)

        yssd = _ssd(main3, tail3, ssd_conv_w[l], ssd_conv_b[l].reshape(1, -1),
                    group_lanes(ssd_dt_bias[l]), group_lanes(ssd_a_log[l]),
                    jnp.repeat(ssd_d[l].astype(F32), SSD_HEAD_DIM).reshape(1, -1),
                    ssd_norm_w[l].reshape(1, -1))
        onsa = _nsa(main3, tail3, nsa_q_norm_w[l], nsa_k_norm_w[l], cmp_pos_k[l], cmp_pos_v[l],
                    cmp_k_w1[l], cmp_k_w2[l], cmp_v_w1[l], cmp_v_w2[l])

        rw = jnp.zeros((D, LANES), F32).at[:, :N_EXPERTS].set(router_w[l])
        rwh, rwl = _split_bf16(rw)
        rb = jnp.full((1, LANES), NEG, F32).at[0, :N_EXPERTS].set(router_b[l])
        x1, xn2, route, cnt = _merge(x2, yssd.reshape(T, -1), onsa, main,
                                     ssd_out_w[l].astype(BF16), nsa_out_w[l].astype(BF16), w_out[l].astype(BF16),
                                     ln2_w[l].reshape(1, D), rwh, rwl, rb, tm=min(256, T))

        dest, buf_tok3, block_e, n_used = _moe_tables(route, cnt, T, bm)
        ys = _experts(block_e, n_used, buf_tok3, xn2, _deinterleave(exp_w1[l]),
                      exp_b1[l][:, None, 0::2], exp_b1[l][:, None, 1::2],
                      exp_w2[l].astype(BF16), exp_b2[l][:, None, :], bm)
        tc = min(256, T)
        dest3 = dest.reshape(T // tc, tc, TOP_K).transpose(0, 2, 1).reshape(T // tc, 1, TOP_K * tc)
        x2 = _combine(dest3, route, x1, ys, tc)
    return x2.reshape(B, S, D)
```

```python
import functools
import math

import numpy as np
import jax
import jax.numpy as jnp
from jax import lax
from jax.experimental import pallas as pl
from jax.experimental.pallas import tpu as pltpu

F32 = jnp.float32
BF16 = jnp.bfloat16
I32 = jnp.int32

D_MODEL = 1024
SSD_D_INNER = 2048
SSD_HEAD_DIM = 64
SSD_N_HEADS = 32
SSD_N_GROUPS = 4
SSD_D_STATE = 128
SSD_CHUNK = 256
SSD_HPG = SSD_N_HEADS // SSD_N_GROUPS
NSA_N_HEADS = 16
NSA_N_KV = 4
NSA_HPG = NSA_N_HEADS // NSA_N_KV
NSA_HEAD_DIM = 64
NSA_CMP_LEN = 32
NSA_CMP_STRIDE = 16
NSA_CMP_HIDDEN = 256
NSA_SEL_LEN = 64
NSA_SEL_TOP = 16
NSA_WINDOW = 512
Q_BLOCK = 128
FORCED_SCORE = 1.0e4
N_EXPERTS = 32
TOP_K = 4
SWIGLU_LIMIT = 7.0
SWIGLU_ALPHA = 1.702
EPS = 1e-6

NEG = -1.0e30
LOG2E = 1.4426950408889634
LANES = 128
K_AUG = 80
V_AUG = 80
NSA_PAIRS_PER_ITER = 2
NSA_TILES_PER_STEP = 2
VMEM_LIMIT = 56 * 1024 * 1024

C_Z, C_XS, C_B, C_C, C_MS, C_MN, C_Q, C_KC, C_VC, C_KS, C_VS, C_KW, C_VW = (
    0, 2048, 4096, 4608, 5120, 6144, 7168, 8192, 8448, 8704, 8960, 9216, 9472)
MAIN_W = 9728
TAIL_W = 512
O_DT, O_Q, O_GATE, O_MERGE = 5120, 5152, 7712, 7760


def _cparams(sem):
    return pltpu.CompilerParams(dimension_semantics=sem, vmem_limit_bytes=VMEM_LIMIT)


def _dot(a, b):
    return jnp.dot(a, b, preferred_element_type=F32)


def _split_bf16(x):
    hi = x.astype(BF16)
    lo = (x - hi.astype(F32)).astype(BF16)
    return hi, lo


def _inproj_kernel(x_ref, lnw_ref, w_ref, wt_ref, main_ref, tail_ref, xn_ref):
    @pl.when(pl.program_id(1) == 0)
    def _():
        x = x_ref[...]
        ms = jnp.mean(x * x, axis=-1, keepdims=True)
        xn = (x * lax.rsqrt(ms + EPS) * lnw_ref[...]).astype(BF16)
        xn_ref[...] = xn
        tail_ref[...] = _dot(xn, wt_ref[...])

    main_ref[...] = _dot(xn_ref[...], w_ref[...]).astype(BF16)


def _inproj(x2, ln_w, w_main, w_tail, tm, tn):
    T = x2.shape[0]
    return pl.pallas_call(
        _inproj_kernel,
        grid=(T // tm, MAIN_W // tn),
        in_specs=[pl.BlockSpec((tm, D_MODEL), lambda i, j: (i, 0)),
                  pl.BlockSpec((1, D_MODEL), lambda i, j: (0, 0)),
                  pl.BlockSpec((D_MODEL, tn), lambda i, j: (0, j)),
                  pl.BlockSpec((D_MODEL, TAIL_W), lambda i, j: (0, 0))],
        out_specs=[pl.BlockSpec((tm, tn), lambda i, j: (i, j)),
                   pl.BlockSpec((tm, TAIL_W), lambda i, j: (i, 0))],
        out_shape=[jax.ShapeDtypeStruct((T, MAIN_W), BF16),
                   jax.ShapeDtypeStruct((T, TAIL_W), F32)],
        scratch_shapes=[pltpu.VMEM((tm, D_MODEL), BF16)],
        compiler_params=_cparams(("arbitrary", "arbitrary")),
        name="inproj",
    )(x2, ln_w, w_main, w_tail)


def _softplus(x):
    return jnp.maximum(x, 0.0) + jnp.log1p(jnp.exp(-jnp.abs(x)))


def _ssd_kernel(z_ref, xs_ref, b_ref, c_ref, dt_ref, cwx_ref, cwb_ref, cwc_ref, cbx_ref, cbb_ref, cbc_ref,
                dtb_ref, alog_ref, dexp_ref, nw_ref, hexp_ref, y_ref, extx, extb, extc, hs_ref):
    L = xs_ref.shape[0]
    P = SSD_HEAD_DIM

    @pl.when(pl.program_id(2) == 0)
    def _():
        extx[0:8, :] = jnp.zeros((8, extx.shape[1]), F32)
        extb[0:8, :] = jnp.zeros((8, extb.shape[1]), F32)
        extc[0:8, :] = jnp.zeros((8, extc.shape[1]), F32)
        hs_ref[...] = jnp.zeros(hs_ref.shape, F32)

    def conv_act(x_ref, w_ref, bias_ref, ext):
        xf = x_ref[...].astype(F32)
        ext[8:8 + L, :] = xf
        acc = bias_ref[...] + w_ref[3:4, :] * xf
        for k in (1, 2, 3):
            acc = acc + w_ref[3 - k:4 - k, :] * ext[8 - k:8 - k + L, :]
        ext[0:8, :] = ext[L:L + 8, :]
        return acc * jax.nn.sigmoid(acc)

    xs = conv_act(xs_ref, cwx_ref, cbx_ref, extx)
    bm = conv_act(b_ref, cwb_ref, cbb_ref, extb)
    cm = conv_act(c_ref, cwc_ref, cbc_ref, extc)

    dt = _softplus(dt_ref[...] + dtb_ref[...])
    da = dt * (-jnp.exp(alog_ref[...]))
    row = lax.broadcasted_iota(I32, (L, L), 0)
    col = lax.broadcasted_iota(I32, (L, L), 1)
    tril = row >= col
    trif = jnp.where(tril, 1.0, 0.0).astype(BF16)
    da_hi, da_lo = _split_bf16(da)
    acum = _dot(trif, da_hi) + _dot(trif, da_lo)
    acum_t = acum.T
    a_end = acum[L - 1:L, :]
    ea = jnp.exp(acum)
    wend = jnp.exp(a_end - acum) * dt
    eend = jnp.exp(a_end)

    cb16 = cm.astype(BF16)
    bt16 = bm.T.astype(BF16)
    cb = _dot(cb16, bt16)

    def head_lanes(v):
        hi, lo = _split_bf16(v)
        return _dot(hi, hexp_ref[...]) + _dot(lo, hexp_ref[...])

    xdt = (xs * head_lanes(dt)).astype(BF16)
    xw = (xs * head_lanes(wend)).astype(BF16)
    y_intra, y_inter = [], []
    for hh in range(SSD_HPG):
        seg = acum[:, hh:hh + 1] - acum_t[hh:hh + 1, :]
        decay = jnp.exp(jnp.where(tril, seg, NEG))
        g = (cb * decay).astype(BF16)
        hprev = hs_ref[hh]
        y_intra.append(_dot(g, xdt[:, hh * P:(hh + 1) * P]))
        y_inter.append(_dot(cb16, hprev.astype(BF16)))
        hs_ref[hh] = hprev * eend[:, hh:hh + 1] + _dot(bt16, xw[:, hh * P:(hh + 1) * P])
    y = jnp.concatenate(y_intra, axis=1) + jnp.concatenate(y_inter, axis=1) * head_lanes(ea)
    y = y + dexp_ref[...] * xs
    z = z_ref[...].astype(F32)
    y = y * (z * jax.nn.sigmoid(z))
    ms = jnp.mean(y * y, axis=-1, keepdims=True)
    y_ref[...] = (y * lax.rsqrt(ms + EPS) * nw_ref[...]).astype(BF16)


def _ssd(main3, tail3, conv_w, conv_b, dtb, alog, dexp, norm_w):
    B, S, _ = main3.shape
    L = math.gcd(S, SSD_CHUNK)
    nc = S // L
    G, N, GW = SSD_N_GROUPS, SSD_D_STATE, SSD_D_INNER // SSD_N_GROUPS
    xs0, b0, c0 = C_XS // GW, C_B // N, C_C // N
    cb0, cc0 = SSD_D_INNER // N, (SSD_D_INNER + G * N) // N
    hexp = jnp.asarray(np.arange(LANES)[:, None] == np.arange(GW)[None, :] // SSD_HEAD_DIM, BF16)

    def seq(w, off):
        return pl.BlockSpec((None, L, w), lambda b, g, c: (b, c, off + g))

    def par(r, w, off):
        return pl.BlockSpec((r, w), lambda b, g, c: (0, off + g))

    return pl.pallas_call(
        _ssd_kernel,
        grid=(B, G, nc),
        in_specs=[seq(GW, 0), seq(GW, xs0), seq(N, b0), seq(N, c0), seq(LANES, 0),
                  par(4, GW, 0), par(4, N, cb0), par(4, N, cc0),
                  par(1, GW, 0), par(1, N, cb0), par(1, N, cc0),
                  par(1, LANES, 0), par(1, LANES, 0), par(1, GW, 0), par(1, GW, 0),
                  pl.BlockSpec((LANES, GW), lambda b, g, c: (0, 0))],
        out_specs=pl.BlockSpec((None, L, GW), lambda b, g, c: (b, c, g)),
        out_shape=jax.ShapeDtypeStruct((B, S, SSD_D_INNER), BF16),
        scratch_shapes=[pltpu.VMEM((L + 8, GW), F32), pltpu.VMEM((L + 8, N), F32), pltpu.VMEM((L + 8, N), F32),
                        pltpu.VMEM((SSD_HPG, N, SSD_HEAD_DIM), F32)],
        compiler_params=_cparams(("arbitrary", "arbitrary", "arbitrary")),
        name="ssd",
    )(main3, main3, main3, main3, tail3, conv_w, conv_w, conv_w, conv_b, conv_b, conv_b,
      dtb, alog, dexp, norm_w, hexp)


def _nsa_norm_kernel(q_ref, ks_ref, kw_ref, vs_ref, vw_ref, bdq_ref, bdk_ref, wq_ref, wk_ref, srow_ref,
                     qt_ref, kso_ref, kwo_ref, vst_ref, vwt_ref):
    QB, DH, G, HPG = Q_BLOCK, NSA_HEAD_DIM, NSA_N_KV, NSA_HPG
    i = pl.program_id(1)

    def head_norm(x_ref, bd_ref, w_ref):
        x = x_ref[...].astype(F32)
        hi, lo = _split_bf16(x * x)
        ms = (_dot(hi, bd_ref[...]) + _dot(lo, bd_ref[...])) * (1.0 / DH)
        return x * lax.rsqrt(ms + EPS) * w_ref[...]

    qn_t = head_norm(q_ref, bdq_ref, wq_ref).T
    pos = i * QB + lax.broadcasted_iota(I32, (QB, K_AUG - DH), 0)
    lane = lax.broadcasted_iota(I32, (QB, K_AUG - DH), 1)
    lo = pos & 255
    pcols = jnp.where(lane >= 4, 0, jnp.where((lane & 1) == 0, lo, pos - lo)).astype(F32)
    ksn = head_norm(ks_ref, bdk_ref, wk_ref)
    kwn = head_norm(kw_ref, bdk_ref, wk_ref)
    vs_t = vs_ref[...].astype(F32).T
    vw_t = vw_ref[...].astype(F32).T
    ones_rows = jnp.where(lax.broadcasted_iota(I32, (V_AUG - DH, QB), 0) == 0, 1.0, 0.0).astype(BF16)
    for g in range(G):
        heads = [qn_t[(g * HPG + h) * DH:(g * HPG + h + 1) * DH, :] for h in range(HPG)]
        qt_ref[g] = jnp.concatenate([jnp.concatenate(heads, axis=1).astype(BF16), srow_ref[g]], axis=0)
        kso_ref[g] = jnp.concatenate([ksn[:, g * DH:(g + 1) * DH], pcols], axis=1).astype(BF16)
        kwo_ref[g] = jnp.concatenate([kwn[:, g * DH:(g + 1) * DH], pcols], axis=1).astype(BF16)
        vst_ref[g] = jnp.concatenate([vs_t[g * DH:(g + 1) * DH, :].astype(BF16), ones_rows], axis=0)
        vwt_ref[g] = jnp.concatenate([vw_t[g * DH:(g + 1) * DH, :].astype(BF16), ones_rows], axis=0)


def _nsa_norm(main3, bdq, bdk, wq, wk, srow):
    B, S, _ = main3.shape
    QB, DH, G = Q_BLOCK, NSA_HEAD_DIM, NSA_N_KV
    nb = S // QB
    QW, KW, W = NSA_N_HEADS * DH, G * DH, NSA_HPG * QB
    const = lambda shape: pl.BlockSpec(shape, lambda b, i: (0,) * len(shape))
    col = lambda w, c0: pl.BlockSpec((None, QB, w), lambda b, i: (b, i, c0 // w))
    return pl.pallas_call(
        _nsa_norm_kernel,
        grid=(B, nb),
        in_specs=[col(QW, C_Q), col(KW, C_KS), col(KW, C_KW), col(KW, C_VS), col(KW, C_VW),
                  const((QW, QW)), const((KW, KW)), const((1, QW)), const((1, KW)), const(srow.shape)],
        out_specs=[pl.BlockSpec((None, G, None, K_AUG, W), lambda b, i: (b, 0, i, 0, 0)),
                   pl.BlockSpec((None, G, QB, K_AUG), lambda b, i: (b, 0, i, 0)),
                   pl.BlockSpec((None, G, QB, K_AUG), lambda b, i: (b, 0, i, 0)),
                   pl.BlockSpec((None, G, None, V_AUG, QB), lambda b, i: (b, 0, i, 0, 0)),
                   pl.BlockSpec((None, G, None, V_AUG, QB), lambda b, i: (b, 0, i, 0, 0))],
        out_shape=[jax.ShapeDtypeStruct((B, G, nb, K_AUG, W), BF16),
                   jax.ShapeDtypeStruct((B, G, S, K_AUG), BF16), jax.ShapeDtypeStruct((B, G, S, K_AUG), BF16),
                   jax.ShapeDtypeStruct((B, G, nb, V_AUG, QB), BF16),
                   jax.ShapeDtypeStruct((B, G, nb, V_AUG, QB), BF16)],
        compiler_params=_cparams(("arbitrary", "arbitrary")),
        name="nsa_norm_layout",
    )(main3, main3, main3, main3, main3, bdq, bdk, wq, wk, srow)


def _compress_kernel(u_ref, pos_ref, w1_ref, w2_ref, nw_ref, o_ref, *, normalize):
    half = u_ref.shape[1]
    nrow = u_ref.shape[0]
    u = u_ref[...]
    a = _dot(u, w1_ref[0:half, :])
    b = _dot(u, w1_ref[half:2 * half, :])
    posc = _dot(pos_ref[...], w1_ref[...])[0:1, :]
    pre = a + pltpu.roll(b, nrow - 1, 0) + posc
    act = 0.5 * pre * (1.0 + jnp.tanh(math.sqrt(2.0 / math.pi) * (pre + 0.044715 * (pre * pre * pre))))
    o = _dot(act.astype(BF16), w2_ref[...])
    if normalize:
        ms = jnp.mean(o * o, axis=-1, keepdims=True)
        o = o * lax.rsqrt(ms + EPS) * nw_ref[...]
    o_ref[...] = o


def _compress(u, pos8, w1, w2, nw, normalize):
    BG, nrow, half = u.shape
    const = lambda shape: pl.BlockSpec(shape, lambda i: (0, 0))
    return pl.pallas_call(
        functools.partial(_compress_kernel, normalize=normalize),
        grid=(BG,),
        in_specs=[pl.BlockSpec((None, nrow, half), lambda i: (i, 0, 0)),
                  const(pos8.shape), const(w1.shape), const(w2.shape), const(nw.shape)],
        out_specs=pl.BlockSpec((None, nrow, NSA_HEAD_DIM), lambda i: (i, 0, 0)),
        out_shape=jax.ShapeDtypeStruct((BG, nrow, NSA_HEAD_DIM), F32),
        compiler_params=_cparams(("arbitrary",)),
        name="nsa_compress_norm" if normalize else "nsa_compress",
    )(u, pos8, w1, w2, nw)


def _nsa_cmp_kernel(qt_ref, kc_ref, vct_ref, ovt_ref, pm_ref, glt_ref, oct_ref, sb_ref, lst_ref, imp_scr):
    NT = qt_ref.shape[0]
    tiles = range(NT)
    idx = [pl.program_id(2) * NT + t for t in tiles]
    ncmp = kc_ref.shape[0]
    nsel = ovt_ref.shape[0]
    QB = Q_BLOCK
    W = NSA_HPG * QB
    CH = min(LANES, ncmp)
    span = CH * NSA_CMP_STRIDE
    nch = jnp.minimum((idx[-1] * QB + QB - NSA_CMP_LEN) // span + 1, ncmp // CH)

    def attend(rows):
        jrow = lax.broadcasted_iota(I32, (rows, W), 0)
        lane = lax.broadcasted_iota(I32, (rows, W), 1)
        for t in tiles:
            s = _dot(kc_ref[0:rows, :], qt_ref[t])
            mask = (NSA_CMP_STRIDE * jrow + (NSA_CMP_LEN - 1)) <= idx[t] * QB + (lane & (QB - 1))
            sm = jnp.where(mask, s, NEG)
            m = jnp.max(sm, axis=0, keepdims=True)
            p = jnp.where(mask, jnp.exp2(sm - m), 0.0)
            l = jnp.sum(p, axis=0, keepdims=True)
            pn = p * (1.0 / jnp.maximum(l, 1e-30))
            oct_ref[t] = _dot(vct_ref[:, 0:rows], pn.astype(BF16)) * jax.nn.sigmoid(glt_ref[t, 0:1, :])
            psum = pn[:, 0:QB]
            for h in range(1, NSA_HPG):
                psum = psum + pn[:, h * QB:(h + 1) * QB]
            imp_scr[t] = _dot(ovt_ref[:, 0:rows], psum.astype(BF16))

    for k in range(1, ncmp // CH + 1):
        pl.when(nch == k)(functools.partial(attend, k * CH))

    jf = lax.broadcasted_iota(I32, (nsel, QB), 0).astype(F32)
    lane_q = lax.broadcasted_iota(I32, (nsel, QB), 1)
    valid, v0 = [], []
    for t in tiles:
        cur = ((idx[t] * QB + lane_q) >> 6).astype(F32)
        forced = (jf == 0.0) | (jf == cur) | (jf == cur - 1.0)
        valid.append(jf <= cur)
        v0.append(jnp.where(forced, FORCED_SCORE, jnp.where(valid[t], imp_scr[t], -1.0)))

    def pick_one(_, carry):
        out = []
        for v, sel in carry:
            mx = jnp.max(v, axis=0, keepdims=True)
            first = jnp.min(jnp.where(v == mx, jf, float(nsel)), axis=0, keepdims=True)
            pick = jf == first
            out.append((jnp.where(pick, -2.0, v), jnp.where(pick, 1.0, sel)))
        return tuple(out)

    picked = lax.fori_loop(0, min(NSA_SEL_TOP, nsel), pick_one,
                           tuple((v0[t], jnp.zeros((nsel, QB), F32)) for t in tiles))

    npair = nsel // 2
    jp = lax.broadcasted_iota(I32, (npair, LANES), 0)
    r2 = lax.broadcasted_iota(I32, (npair, npair), 0)
    c2 = lax.broadcasted_iota(I32, (npair, npair), 1)
    tri = jnp.where(c2 <= r2, 1.0, 0.0).astype(BF16)
    slot = lax.broadcasted_iota(I32, (npair, LANES), 1).astype(F32)
    jpv = lax.broadcasted_iota(I32, (8, npair), 1).astype(F32).astype(BF16)
    r8 = lax.broadcasted_iota(I32, (8, LANES), 0)
    for t in tiles:
        sel = jnp.where(valid[t], picked[t][1], 0.0)
        sb_ref[t] = jnp.where(sel > 0.0, 0.0, NEG)
        pairsel = _dot(pm_ref[...], sel.astype(BF16))
        need = (jnp.sum(pairsel, axis=1, keepdims=True) > 0.0) & (jp < idx[t])
        needf = jnp.where(need, 1.0, 0.0)
        prefix = _dot(tri, needf.astype(BF16))
        onehot = jnp.where(need & (prefix == slot + 1.0), 1.0, 0.0).astype(BF16)
        lst = _dot(jpv, onehot)
        cnt = _dot(jnp.ones((8, npair), BF16), needf.astype(BF16))
        lst_ref[t] = jnp.where(r8 == 0, lst, cnt).astype(I32)


def _nsa_cmp(qt, kc, vct, ovt, pm, glt):
    B, G, nb, _, W = qt.shape
    ncmp = kc.shape[2]
    nsel = ovt.shape[0]
    NT = NSA_TILES_PER_STEP
    blk = lambda *shape: pl.BlockSpec((None, None, NT) + shape, lambda b, g, i: (b, g, i) + (0,) * len(shape))
    per_bg = lambda *shape: pl.BlockSpec((None, None) + shape, lambda b, g, i: (b, g) + (0,) * len(shape))
    const = lambda shape: pl.BlockSpec(shape, lambda b, g, i: (0,) * len(shape))
    return pl.pallas_call(
        _nsa_cmp_kernel,
        grid=(B, G, nb // NT),
        in_specs=[blk(K_AUG, W), per_bg(ncmp, K_AUG), per_bg(NSA_HEAD_DIM, ncmp),
                  const(ovt.shape), const(pm.shape), blk(8, W)],
        out_specs=[blk(NSA_HEAD_DIM, W), blk(nsel, Q_BLOCK), blk(8, LANES)],
        out_shape=[jax.ShapeDtypeStruct((B, G, nb, NSA_HEAD_DIM, W), F32),
                   jax.ShapeDtypeStruct((B, G, nb, nsel, Q_BLOCK), F32),
                   jax.ShapeDtypeStruct((B, G, nb, 8, LANES), I32)],
        scratch_shapes=[pltpu.VMEM((NSA_TILES_PER_STEP, nsel, Q_BLOCK), F32)],
        compiler_params=_cparams(("arbitrary", "arbitrary", "arbitrary")),
        name="nsa_cmp_select",
    )(qt, kc, vct, ovt, pm, glt)


def _tile4(r):
    return jnp.concatenate([r] * NSA_HPG, axis=1)


def _nsa_sel_kernel(lst_ref, qt_ref, ks_ref, vst_ref, kw_ref, vwt_ref, sb_ref, glt_ref, oct_ref, o_ref,
                    s_scr, d_scr, ow_scr):
    NT = qt_ref.shape[0]
    QB = Q_BLOCK
    W = NSA_HPG * QB
    HB = NSA_SEL_LEN
    DH = NSA_HEAD_DIM
    NPI = NSA_PAIRS_PER_ITER
    tiles = range(NT)
    qts = [qt_ref[t] for t in tiles]
    counts = [lst_ref[t, 1, 0] for t in tiles]
    idx = [pl.program_id(2) * NT + t for t in tiles]

    def flash(s, vt, carry):
        m, acc = carry
        mn = jnp.maximum(m, jnp.max(s, axis=0, keepdims=True))
        p = jnp.exp2(s - mn)
        return mn, jnp.exp2(m - mn) * acc + _dot(vt, p.astype(BF16))

    def pair_scores(t, k):
        jp = lst_ref[t, 0, k]
        live = k < counts[t]
        s = _dot(ks_ref[pl.ds(pl.multiple_of(jp * QB, QB), QB), :], qts[t])
        b0 = jnp.where(live, _tile4(sb_ref[t, pl.ds(2 * jp, 1), :]), NEG)
        b1 = jnp.where(live, _tile4(sb_ref[t, pl.ds(2 * jp + 1, 1), :]), NEG)
        return jnp.concatenate([s[0:HB] + b0, s[HB:2 * HB] + b1], axis=0)

    def scores(t, it):
        return jnp.concatenate([pair_scores(t, NPI * it + u) for u in range(NPI)], axis=0)

    r = lax.broadcasted_iota(I32, (QB, W), 0)
    c = lax.broadcasted_iota(I32, (QB, W), 1) & (QB - 1)
    for t in tiles:
        i = idx[t]
        s_scr[t, 0] = scores(t, 0)
        kd = ks_ref[pl.ds(pl.multiple_of(i * QB, QB), QB), :]
        d_scr[t] = jnp.where(r <= c, _dot(kd, qts[t]), NEG)

        ss, vts = [], []
        for w in range(NSA_WINDOW // QB + 1):
            pw = i - NSA_WINDOW // QB + w
            pc = jnp.maximum(pw, 0)
            kk = kw_ref[pl.ds(pl.multiple_of(pc * QB, QB), QB), :]
            s = _dot(kk, qts[t])
            if w == 0:
                s = jnp.where((r > c) & (pw >= 0), s, NEG)
            elif w == NSA_WINDOW // QB:
                s = jnp.where(r <= c, s, NEG)
            else:
                s = jnp.where(pw >= 0, s, NEG)
            ss.append(s)
            vts.append(vwt_ref[pc])
        s = jnp.concatenate(ss, axis=0)
        p = jnp.exp2(s - jnp.max(s, axis=0, keepdims=True))
        aw = _dot(jnp.concatenate(vts, axis=1), p.astype(BF16))
        o_win = aw[0:DH] * (1.0 / aw[DH:DH + 1])
        ow_scr[t] = oct_ref[t] + jax.nn.sigmoid(glt_ref[t, 2:3, :]) * o_win

    def body(it, carry):
        slot = it & 1
        cur = [s_scr[t, slot] for t in tiles]
        vts = [jnp.concatenate([vst_ref[lst_ref[t, 0, NPI * it + u]] for u in range(NPI)], axis=1) for t in tiles]
        nxt = [scores(t, it + 1) for t in tiles]
        out = tuple(flash(cur[t], vts[t], carry[t]) for t in tiles)
        for t in tiles:
            s_scr[t, 1 - slot] = nxt[t]
        return out

    n_iter = (functools.reduce(jnp.maximum, counts) + NPI - 1) // NPI
    init = tuple((jnp.full((1, W), NEG, F32), jnp.zeros((vst_ref.shape[1], W), F32)) for _ in tiles)
    carry = lax.fori_loop(0, n_iter, body, init)
    for t in tiles:
        m, acc = flash(d_scr[t], vst_ref[idx[t]], carry[t])
        o_t = ow_scr[t] + jax.nn.sigmoid(glt_ref[t, 1:2, :]) * (acc[0:DH] * (1.0 / acc[DH:DH + 1]))
        o_ref[t * QB:(t + 1) * QB, :] = jnp.concatenate(
            [o_t[:, h * QB:(h + 1) * QB].T for h in range(NSA_HPG)], axis=1).astype(BF16)


def _nsa_sel(lst, qt, ks, vst, kw, vwt, sb, glt, oct):
    B, G, nb, _, W = qt.shape
    S = ks.shape[2]
    npair = vst.shape[2]
    nsel = sb.shape[3]
    NT = NSA_TILES_PER_STEP
    blk = lambda *shape: pl.BlockSpec((None, None, NT) + shape, lambda b, g, i: (b, g, i) + (0,) * len(shape))
    per_bg = lambda *shape: pl.BlockSpec((None, None) + shape, lambda b, g, i: (b, g) + (0,) * len(shape))
    return pl.pallas_call(
        _nsa_sel_kernel,
        grid=(B, G, nb // NT),
        in_specs=[pl.BlockSpec((None, None, NT, 8, LANES), lambda b, g, i: (b, g, i, 0, 0),
                               memory_space=pltpu.SMEM),
                  blk(K_AUG, W), per_bg(S, K_AUG), per_bg(npair, V_AUG, Q_BLOCK),
                  per_bg(S, K_AUG), per_bg(npair, V_AUG, Q_BLOCK),
                  blk(nsel, Q_BLOCK), blk(8, W), blk(NSA_HEAD_DIM, W)],
        out_specs=pl.BlockSpec((None, NT * Q_BLOCK, NSA_HPG * NSA_HEAD_DIM), lambda b, g, i: (b, i, g)),
        out_shape=jax.ShapeDtypeStruct((B, S, NSA_N_HEADS * NSA_HEAD_DIM), BF16),
        scratch_shapes=[pltpu.VMEM((NT, 2, NSA_PAIRS_PER_ITER * Q_BLOCK, W), F32),
                        pltpu.VMEM((NT, Q_BLOCK, W), F32), pltpu.VMEM((NT, NSA_HEAD_DIM, W), F32)],
        compiler_params=_cparams(("arbitrary", "arbitrary", "arbitrary")),
        name="nsa_select_window",
    )(lst, qt, ks, vst, kw, vwt, sb, glt, oct)


ROW_TILE = 8


def _store_row_tiles(ref, x):
    n = x.shape[0]
    for c in range(ROW_TILE):
        ref[pl.ds(c, n, stride=ROW_TILE), :] = x[:, c * LANES:(c + 1) * LANES]


def _load_row_tiles(ref, start, n):
    return jnp.concatenate([ref[pl.ds(start + c, n, stride=ROW_TILE), :] for c in range(ROW_TILE)], axis=1)


def _merge_kernel(x_ref, ys_ref, on_ref, gs_ref, gn_ref, wssd_ref, wnsa_ref, wout_ref, ln2_ref,
                  rwh_ref, rwl_ref, rb_ref, x1_ref, xn2_ref, route_ref, cnt_ref, base_ref):
    tm = x_ref.shape[0]

    @pl.when(pl.program_id(0) == 0)
    def _():
        base_ref[...] = jnp.zeros(base_ref.shape, F32)

    y_ssd = _dot(ys_ref[...], wssd_ref[...])
    y_nsa = _dot(on_ref[...], wnsa_ref[...])
    merged = (jax.nn.sigmoid(gs_ref[...].astype(F32)) * y_ssd
              + jax.nn.sigmoid(gn_ref[...].astype(F32)) * y_nsa)
    x1 = x_ref[...] + _dot(merged.astype(BF16), wout_ref[...])
    x1_ref[...] = x1
    ms = jnp.mean(x1 * x1, axis=-1, keepdims=True)
    xn2 = x1 * lax.rsqrt(ms + EPS) * ln2_ref[...]
    _store_row_tiles(xn2_ref, xn2)

    xh, xl = _split_bf16(xn2)
    logits = _dot(xh, rwh_ref[...]) + _dot(xl, rwh_ref[...]) + _dot(xh, rwl_ref[...]) + rb_ref[...]
    lane = lax.broadcasted_iota(I32, (tm, LANES), 1)
    lanef = lane.astype(F32)
    v = logits
    onehot = jnp.zeros((tm, LANES), F32)
    vals, picks = [], []
    for _ in range(TOP_K):
        mx = jnp.max(v, axis=-1, keepdims=True)
        first = jnp.min(jnp.where(v == mx, lanef, float(LANES)), axis=-1, keepdims=True)
        pick = lanef == first
        v = jnp.where(pick, 2.0 * NEG, v)
        onehot = jnp.where(pick, 1.0, onehot)
        vals.append(mx)
        picks.append((pick, first))
    es = [jnp.exp(val - vals[0]) for val in vals]
    inv = 1.0 / (es[0] + es[1] + es[2] + es[3])

    r2 = lax.broadcasted_iota(I32, (tm, tm), 0)
    c2 = lax.broadcasted_iota(I32, (tm, tm), 1)
    stril = jnp.where(c2 < r2, 1.0, 0.0).astype(BF16)
    posmap = base_ref[...] + _dot(stril, onehot.astype(BF16))
    route = jnp.zeros((tm, LANES), F32)
    for k in range(TOP_K):
        pick, first = picks[k]
        pos = jnp.sum(jnp.where(pick, posmap, 0.0), axis=-1, keepdims=True)
        route = jnp.where(lane == k, first, route)
        route = jnp.where(lane == TOP_K + k, pos, route)
        route = jnp.where(lane == 2 * TOP_K + k, es[k] * inv, route)
    route_ref[...] = route
    base = base_ref[...] + jnp.sum(onehot, axis=0, keepdims=True)
    base_ref[...] = base
    cnt_ref[...] = jnp.broadcast_to(base, cnt_ref.shape)


def _merge(x2, yssd, onsa, main, wssd, wnsa, wout, ln2, rwh, rwl, rb, tm):
    T = x2.shape[0]
    const = lambda shape: pl.BlockSpec(shape, lambda i: (0, 0))
    row = lambda w, j=0: pl.BlockSpec((tm, w), lambda i: (i, j))
    return pl.pallas_call(
        _merge_kernel,
        grid=(T // tm,),
        in_specs=[row(D_MODEL), row(SSD_D_INNER), row(D_MODEL), row(D_MODEL, C_MS // D_MODEL),
                  row(D_MODEL, C_MN // D_MODEL),
                  const(wssd.shape), const(wnsa.shape), const(wout.shape), const(ln2.shape),
                  const(rwh.shape), const(rwl.shape), const(rb.shape)],
        out_specs=[row(D_MODEL), pl.BlockSpec((tm * ROW_TILE, LANES), lambda i: (i, 0)), row(LANES),
                   const((8, LANES))],
        out_shape=[jax.ShapeDtypeStruct((T, D_MODEL), F32), jax.ShapeDtypeStruct((T * ROW_TILE, LANES), F32),
                   jax.ShapeDtypeStruct((T, LANES), F32), jax.ShapeDtypeStruct((8, LANES), F32)],
        scratch_shapes=[pltpu.VMEM((1, LANES), F32)],
        compiler_params=_cparams(("arbitrary",)),
        name="merge_router",
    )(x2, yssd, onsa, main, main, wssd, wnsa, wout, ln2, rwh, rwl, rb)


def _dispatch_kernel(dest_ref, xn_ref, xs_init, xs_hbm, sem):
    del xs_init
    RT = ROW_TILE
    tc = xn_ref.shape[0] // RT

    def issue(r2, c):
        for pr in (0, 1):
            r = 2 * r2 + pr
            src = pl.multiple_of((r & (tc - 1)) * RT, RT)
            dst = pl.multiple_of(dest_ref[0, r] * RT, RT)
            pltpu.make_async_copy(xn_ref.at[pl.ds(src, RT)], xs_hbm.at[pl.ds(dst, RT)], sem).start(priority=pr)
        return c

    lax.fori_loop(0, TOP_K * tc // 2, issue, 0, unroll=4)
    for _ in range(TOP_K):
        pltpu.make_async_copy(xn_ref, xs_hbm.at[pl.ds(0, tc * RT)], sem).wait()


def _dispatch(dest3, xn2, n_rows, tc):
    T = xn2.shape[0] // ROW_TILE
    xs0 = jnp.zeros((n_rows * ROW_TILE, LANES), F32)
    return pl.pallas_call(
        _dispatch_kernel,
        grid=(T // tc,),
        in_specs=[pl.BlockSpec((None, 1, TOP_K * tc), lambda i: (i, 0, 0), memory_space=pltpu.SMEM),
                  pl.BlockSpec((tc * ROW_TILE, LANES), lambda i: (i, 0)),
                  pl.BlockSpec(memory_space=pl.ANY)],
        out_specs=pl.BlockSpec(memory_space=pl.ANY),
        out_shape=jax.ShapeDtypeStruct(xs0.shape, F32),
        scratch_shapes=[pltpu.SemaphoreType.DMA(())],
        input_output_aliases={2: 0},
        compiler_params=_cparams(("arbitrary",)),
        name="moe_dispatch",
    )(dest3, xn2, xs0)


def _expert_kernel(be_ref, nu_ref, xs_ref, w1g_ref, w1l_ref, b1g_ref, b1l_ref, w2_ref, b2_ref, ys_ref):
    bm = xs_ref.shape[0] // ROW_TILE
    blk = pl.program_id(0)

    @pl.when(blk < nu_ref[0])
    def _():
        x = _load_row_tiles(xs_ref, 0, bm).astype(BF16)
        glu = jnp.minimum(_dot(x, w1g_ref[...]) + b1g_ref[...], SWIGLU_LIMIT)
        lin = jnp.clip(_dot(x, w1l_ref[...]) + b1l_ref[...], -SWIGLU_LIMIT, SWIGLU_LIMIT)
        act = glu * jax.nn.sigmoid(SWIGLU_ALPHA * glu) * (lin + 1.0)
        _store_row_tiles(ys_ref, _dot(act.astype(BF16), w2_ref[...]) + b2_ref[...])

    @pl.when(blk >= nu_ref[0])
    def _():
        ys_ref[...] = jnp.zeros(ys_ref.shape, F32)


def _experts(block_e, n_used, xs, w1p, b1g, b1l, w2, b2, bm):
    n_blocks = xs.shape[0] // (bm * ROW_TILE)
    DE = w2.shape[1]
    wspec = lambda r, c, j=0: pl.BlockSpec((None, r, c), lambda b, be, nu: (be[b], 0, j))
    rows = pl.BlockSpec((bm * ROW_TILE, LANES), lambda b, be, nu: (b, 0))
    return pl.pallas_call(
        _expert_kernel,
        grid_spec=pltpu.PrefetchScalarGridSpec(
            num_scalar_prefetch=2,
            grid=(n_blocks,),
            in_specs=[rows, wspec(D_MODEL, DE, 0), wspec(D_MODEL, DE, 1), wspec(1, DE), wspec(1, DE),
                      wspec(DE, D_MODEL), wspec(1, D_MODEL)],
            out_specs=rows),
        out_shape=jax.ShapeDtypeStruct(xs.shape, F32),
        compiler_params=_cparams(("arbitrary",)),
        name="moe_experts",
    )(block_e, n_used, xs, w1p, w1p, b1g, b1l, w2, b2)


def _deinterleave_kernel(w_ref, p_ref, o_ref):
    o_ref[...] = _dot(w_ref[...].astype(BF16), p_ref[...]).astype(BF16)


def _deinterleave(w1):
    E, D, W = w1.shape
    src = np.concatenate([np.arange(0, W, 2), np.arange(1, W, 2)])
    perm = jnp.asarray(np.arange(W)[:, None] == src[None, :], BF16)
    rows = D // 2
    return pl.pallas_call(
        _deinterleave_kernel,
        grid=(E, D // rows),
        in_specs=[pl.BlockSpec((None, rows, W), lambda e, r: (e, r, 0)),
                  pl.BlockSpec((W, W), lambda e, r: (0, 0))],
        out_specs=pl.BlockSpec((None, rows, W), lambda e, r: (e, r, 0)),
        out_shape=jax.ShapeDtypeStruct((E, D, W), BF16),
        compiler_params=_cparams(("arbitrary", "arbitrary")),
        name="moe_w1_deinterleave",
    )(w1, perm)


def _combine_kernel(dest_ref, destn_ref, route_ref, x1_ref, ys_hbm, o_ref, buf, sem):
    tc = x1_ref.shape[0]
    i = pl.program_id(0)
    slot = i & 1

    RT = ROW_TILE

    def gather(dest_smem, s):
        def issue(r2, c):
            for pr in (0, 1):
                r = 2 * r2 + pr
                src = pl.multiple_of(dest_smem[0, r] * RT, RT)
                dst = pl.multiple_of(r * RT, RT)
                pltpu.make_async_copy(ys_hbm.at[pl.ds(src, RT)], buf.at[s, pl.ds(dst, RT)],
                                      sem.at[s]).start(priority=pr)
            return c

        lax.fori_loop(0, TOP_K * tc // 2, issue, 0, unroll=4)

    @pl.when(i == 0)
    def _():
        gather(dest_ref, 0)

    for s in (0, 1):
        @pl.when((i + 1 < pl.num_programs(0)) & (slot == 1 - s))
        def _():
            gather(destn_ref, s)

    pltpu.make_async_copy(ys_hbm.at[pl.ds(0, TOP_K * tc * RT)], buf.at[slot], sem.at[slot]).wait()
    route = route_ref[...]
    acc = x1_ref[...]
    for k in range(TOP_K):
        acc = acc + route[:, 2 * TOP_K + k:2 * TOP_K + k + 1] * _load_row_tiles(buf.at[slot], k * tc * RT, tc)
    o_ref[...] = acc


def _combine(dest3, route, x1, ys, tc):
    T = x1.shape[0]
    n = T // tc
    return pl.pallas_call(
        _combine_kernel,
        grid=(n,),
        in_specs=[pl.BlockSpec((None, 1, TOP_K * tc), lambda i: (i, 0, 0), memory_space=pltpu.SMEM),
                  pl.BlockSpec((None, 1, TOP_K * tc), lambda i: (jnp.minimum(i + 1, n - 1), 0, 0),
                               memory_space=pltpu.SMEM),
                  pl.BlockSpec((tc, LANES), lambda i: (i, 0)),
                  pl.BlockSpec((tc, D_MODEL), lambda i: (i, 0)),
                  pl.BlockSpec(memory_space=pl.ANY)],
        out_specs=pl.BlockSpec((tc, D_MODEL), lambda i: (i, 0)),
        out_shape=jax.ShapeDtypeStruct((T, D_MODEL), F32),
        scratch_shapes=[pltpu.VMEM((2, TOP_K * tc * ROW_TILE, LANES), F32), pltpu.SemaphoreType.DMA((2,))],
        compiler_params=_cparams(("arbitrary",)),
        name="moe_combine",
    )(dest3, dest3, route, x1, ys)


def _split_pos(pos):
    lo = pos % 256
    return lo.astype(np.float32), (pos - lo).astype(np.float32)


def _aug_keys(k, pos):
    lo, hi = _split_pos(pos)
    cols = np.stack([lo, hi, lo, hi] + [np.zeros_like(lo)] * (K_AUG - NSA_HEAD_DIM - 4), axis=-1)
    cols = jnp.broadcast_to(jnp.asarray(cols, BF16), k.shape[:-1] + (cols.shape[-1],))
    return jnp.concatenate([k, cols], axis=-1)


def _slope_rows():
    slopes = (2.0 ** (-8.0 * np.arange(1, NSA_N_HEADS + 1) / NSA_N_HEADS)).astype(np.float32)
    sl = jnp.asarray(np.repeat((slopes * np.float32(LOG2E)).reshape(NSA_N_KV, NSA_HPG), Q_BLOCK, axis=1), F32)
    hi, lo = _split_bf16(sl)
    zero = jnp.zeros_like(hi)
    return jnp.stack([hi, hi, lo, lo] + [zero] * (K_AUG - NSA_HEAD_DIM - 4), axis=1)


def _nsa(main3, tail3, q_norm_w, k_norm_w, cmp_pos_k, cmp_pos_v, cmp_k_w1, cmp_k_w2, cmp_v_w1, cmp_v_w2):
    B, S, _ = main3.shape
    T = B * S
    G, HPG, DH, QB = NSA_N_KV, NSA_HPG, NSA_HEAD_DIM, Q_BLOCK
    nb, npair, nsel, ncmp = S // QB, S // QB, S // NSA_SEL_LEN, S // NSA_CMP_STRIDE
    main = main3.reshape(T, MAIN_W)

    hd = np.arange(NSA_N_HEADS * DH) // DH
    bdq = jnp.asarray(hd[:, None] == hd[None, :], BF16)
    bdk = bdq[:G * DH, :G * DH]
    wq = (jnp.tile(q_norm_w.astype(F32), NSA_N_HEADS) * (DH ** -0.5 * LOG2E)).reshape(1, -1)
    wk = jnp.tile(k_norm_w.astype(F32), G).reshape(1, -1)
    qt, ks, kw, vst, vwt = _nsa_norm(main3, bdq, bdk, wq, wk, _slope_rows())

    def kv_heads(t):
        return t.reshape(B, S, G, DH).transpose(0, 2, 1, 3)

    def halves(c0):
        return kv_heads(main[:, c0:c0 + G * DH]).reshape(B * G, ncmp, NSA_CMP_STRIDE * DH)

    def pos8(pos):
        return jnp.zeros((8, NSA_CMP_LEN * DH), F32).at[0].set(pos.reshape(-1)).astype(BF16)

    ones = jnp.ones((1, DH), F32)
    kc = _compress(halves(C_KC), pos8(cmp_pos_k), cmp_k_w1.astype(BF16), cmp_k_w2.astype(BF16),
                   k_norm_w.astype(F32).reshape(1, DH), True)
    vc = _compress(halves(C_VC), pos8(cmp_pos_v), cmp_v_w1.astype(BF16), cmp_v_w2.astype(BF16), ones, False)
    cmp_end = np.arange(ncmp) * NSA_CMP_STRIDE + NSA_CMP_LEN - 1
    kc = _aug_keys(kc.astype(BF16).reshape(B, G, ncmp, DH), cmp_end)
    vct = vc.astype(BF16).reshape(B, G, ncmp, DH).transpose(0, 1, 3, 2)

    c_start = (np.arange(ncmp) * NSA_CMP_STRIDE)[:, None]
    s_start = (np.arange(nsel) * NSA_SEL_LEN)[None, :]
    overlap = np.clip(np.minimum(c_start + NSA_CMP_LEN, s_start + NSA_SEL_LEN)
                      - np.maximum(c_start, s_start), 0, None) / NSA_CMP_LEN
    overlap[(S - NSA_CMP_LEN) // NSA_CMP_STRIDE + 1:] = 0.0
    ovt = jnp.asarray(overlap.T, BF16)
    pm = jnp.asarray(np.arange(nsel)[None, :] // 2 == np.arange(nsel // 2)[:, None], BF16)

    gl = tail3.reshape(B, nb, QB, G, LANES)[..., 8:8 + 3 * HPG].reshape(B, nb, QB, G, HPG, 3)
    glt = gl.transpose(0, 3, 1, 5, 4, 2).reshape(B, G, nb, 3, HPG * QB)
    glt = jnp.concatenate([glt, jnp.zeros((B, G, nb, 5, HPG * QB), F32)], axis=3)

    oct, sb, lst = _nsa_cmp(qt, kc, vct, ovt, pm, glt)
    return _nsa_sel(lst, qt, ks, vst, kw, vwt, sb, glt, oct).reshape(T, NSA_N_HEADS * DH)


def _moe_tables(route, cnt, T, bm):
    idx = route[:, 0:TOP_K].astype(I32)
    pos = route[:, TOP_K:2 * TOP_K].astype(I32)
    counts = cnt[0, :N_EXPERTS].astype(I32)
    padded = (counts + bm - 1) // bm * bm
    pend = jnp.cumsum(padded)
    pstart = pend - padded
    dest = pstart[idx] + pos
    n_blocks = -(-(T * TOP_K + N_EXPERTS * (bm - 1)) // bm)
    b_start = jnp.arange(n_blocks, dtype=I32) * bm
    block_e = jnp.minimum(jnp.sum(b_start[:, None] >= pend[None, :], axis=1), N_EXPERTS - 1).astype(I32)
    n_used = (pend[-1] // bm).astype(I32).reshape(1)
    return dest, n_blocks * bm, block_e, n_used


def kernel(x, ln1_w, w_in, ssd_conv_w, ssd_conv_b, ssd_dt_bias, ssd_a_log, ssd_d, ssd_norm_w, ssd_out_w,
           nsa_q_norm_w, nsa_k_norm_w, cmp_pos_k, cmp_pos_v, cmp_k_w1, cmp_k_w2, cmp_v_w1, cmp_v_w2,
           nsa_out_w, w_out, ln2_w, router_w, router_b, exp_w1, exp_b1, exp_w2, exp_b2):
    B, S, D = x.shape
    T = B * S
    depth = ln1_w.shape[0]
    x2 = x.reshape(T, D)
    G = SSD_N_GROUPS

    tail_src = np.zeros((TAIL_W,), np.int32)
    tail_on = np.zeros((TAIL_W,), bool)
    for g in range(G):
        tail_src[g * LANES:g * LANES + SSD_HPG] = O_DT + SSD_HPG * g + np.arange(SSD_HPG)
        tail_src[g * LANES + 8:g * LANES + 8 + 3 * NSA_HPG] = O_GATE + 3 * NSA_HPG * g + np.arange(3 * NSA_HPG)
        tail_on[g * LANES:g * LANES + 8 + 3 * NSA_HPG] = True
    main_src = np.concatenate([np.arange(0, O_DT), np.arange(O_MERGE, O_MERGE + 2 * D_MODEL),
                               np.arange(O_Q, O_GATE)])

    def group_lanes(v):
        out = jnp.zeros((G, LANES), F32).at[:, :SSD_HPG].set(v.astype(F32).reshape(G, SSD_HPG))
        return out.reshape(1, TAIL_W)

    tm_rows = min(1024, T)
    bm = 512
    for l in range(depth):
        w_main = w_in[l][:, main_src].astype(BF16)
        w_tail = jnp.where(tail_on[None, :], w_in[l][:, tail_src], 0.0).astype(BF16)
        main, tail = _inproj(x2, ln1_w[l].reshape(1, D), w_main, w_tail, tm=tm_rows, tn=MAIN_W // 4)
        main3 = main.reshape(B, S, MAIN_W)
        tail3 = tail.reshape(B, S, TAIL_W)

        yssd = _ssd(main3, tail3, ssd_conv_w[l], ssd_conv_b[l].reshape(1, -1),
                    group_lanes(ssd_dt_bias[l]), group_lanes(ssd_a_log[l]),
                    jnp.repeat(ssd_d[l].astype(F32), SSD_HEAD_DIM).reshape(1, -1),
                    ssd_norm_w[l].reshape(1, -1))
        onsa = _nsa(main3, tail3, nsa_q_norm_w[l], nsa_k_norm_w[l], cmp_pos_k[l], cmp_pos_v[l],
                    cmp_k_w1[l], cmp_k_w2[l], cmp_v_w1[l], cmp_v_w2[l])

        rw = jnp.zeros((D, LANES), F32).at[:, :N_EXPERTS].set(router_w[l])
        rwh, rwl = _split_bf16(rw)
        rb = jnp.full((1, LANES), NEG, F32).at[0, :N_EXPERTS].set(router_b[l])
        x1, xn2, route, cnt = _merge(x2, yssd.reshape(T, -1), onsa, main,
                                     ssd_out_w[l].astype(BF16), nsa_out_w[l].astype(BF16), w_out[l].astype(BF16),
                                     ln2_w[l].reshape(1, D), rwh, rwl, rb, tm=min(256, T))

        dest, n_rows, block_e, n_used = _moe_tables(route, cnt, T, bm)
        tc = min(256, T)
        dest3 = dest.reshape(T // tc, tc, TOP_K).transpose(0, 2, 1).reshape(T // tc, 1, TOP_K * tc)
        xs = _dispatch(dest3, xn2, n_rows, tc)
        ys = _experts(block_e, n_used, xs, _deinterleave(exp_w1[l]),
                      exp_b1[l][:, None, 0::2], exp_b1[l][:, None, 1::2],
                      exp_w2[l].astype(BF16), exp_b2[l][:, None, :], bm)
        x2 = _combine(dest3, route, x1, ys, tc)
    return x2.reshape(B, S, D)
```

```python
import functools
import math

import numpy as np
import jax
import jax.numpy as jnp
from jax import lax
from jax.experimental import pallas as pl
from jax.experimental.pallas import tpu as pltpu

F32 = jnp.float32
BF16 = jnp.bfloat16
I32 = jnp.int32

D_MODEL = 1024
SSD_D_INNER = 2048
SSD_HEAD_DIM = 64
SSD_N_HEADS = 32
SSD_N_GROUPS = 4
SSD_D_STATE = 128
SSD_CHUNK = 256
SSD_HPG = SSD_N_HEADS // SSD_N_GROUPS
NSA_N_HEADS = 16
NSA_N_KV = 4
NSA_HPG = NSA_N_HEADS // NSA_N_KV
NSA_HEAD_DIM = 64
NSA_CMP_LEN = 32
NSA_CMP_STRIDE = 16
NSA_CMP_HIDDEN = 256
NSA_SEL_LEN = 64
NSA_SEL_TOP = 16
NSA_WINDOW = 512
Q_BLOCK = 128
FORCED_SCORE = 1.0e4
N_EXPERTS = 32
TOP_K = 4
SWIGLU_LIMIT = 7.0
SWIGLU_ALPHA = 1.702
EPS = 1e-6

NEG = -1.0e30
LOG2E = 1.4426950408889634
LANES = 128
K_AUG = 80
V_AUG = 80
NSA_PAIRS_PER_ITER = 2
NSA_TILES_PER_STEP = 2
NSA_CMP_TILES_PER_STEP = 2
VMEM_LIMIT = 56 * 1024 * 1024

C_Z, C_XS, C_B, C_C, C_MS, C_MN, C_Q, C_KC, C_VC, C_KS, C_VS, C_KW, C_VW = (
    0, 2048, 4096, 4608, 5120, 6144, 7168, 8192, 8448, 8704, 8960, 9216, 9472)
MAIN_W = 9728
TAIL_W = 512
O_DT, O_Q, O_GATE, O_MERGE = 5120, 5152, 7712, 7760


def _cparams(sem):
    return pltpu.CompilerParams(dimension_semantics=sem, vmem_limit_bytes=VMEM_LIMIT)


def _dot(a, b):
    return jnp.dot(a, b, preferred_element_type=F32)


def _split_bf16(x):
    hi = x.astype(BF16)
    lo = (x - hi.astype(F32)).astype(BF16)
    return hi, lo


def _inproj_kernel(x_ref, lnw_ref, w_ref, wt_ref, main_ref, tail_ref, xn_ref):
    @pl.when(pl.program_id(1) == 0)
    def _():
        x = x_ref[...]
        ms = jnp.mean(x * x, axis=-1, keepdims=True)
        xn = (x * lax.rsqrt(ms + EPS) * lnw_ref[...]).astype(BF16)
        xn_ref[...] = xn
        tail_ref[...] = _dot(xn, wt_ref[...])

    main_ref[...] = _dot(xn_ref[...], w_ref[...]).astype(BF16)


def _inproj(x2, ln_w, w_main, w_tail, tm, tn):
    T = x2.shape[0]
    return pl.pallas_call(
        _inproj_kernel,
        grid=(T // tm, MAIN_W // tn),
        in_specs=[pl.BlockSpec((tm, D_MODEL), lambda i, j: (i, 0)),
                  pl.BlockSpec((1, D_MODEL), lambda i, j: (0, 0)),
                  pl.BlockSpec((D_MODEL, tn), lambda i, j: (0, j)),
                  pl.BlockSpec((D_MODEL, TAIL_W), lambda i, j: (0, 0))],
        out_specs=[pl.BlockSpec((tm, tn), lambda i, j: (i, j)),
                   pl.BlockSpec((tm, TAIL_W), lambda i, j: (i, 0))],
        out_shape=[jax.ShapeDtypeStruct((T, MAIN_W), BF16),
                   jax.ShapeDtypeStruct((T, TAIL_W), F32)],
        scratch_shapes=[pltpu.VMEM((tm, D_MODEL), BF16)],
        compiler_params=_cparams(("arbitrary", "arbitrary")),
        name="inproj",
    )(x2, ln_w, w_main, w_tail)


def _softplus(x):
    return jnp.maximum(x, 0.0) + jnp.log1p(jnp.exp(-jnp.abs(x)))


def _silu(x):
    return x * (0.5 * jnp.tanh(0.5 * x) + 0.5)


def _ssd_kernel(z_ref, xs_ref, b_ref, c_ref, dt_ref, cwx_ref, cwb_ref, cwc_ref, cbx_ref, cbb_ref, cbc_ref,
                dtb_ref, alog_ref, dexp_ref, nw_ref, hexp_ref, y_ref, extx, extb, extc, hs_ref):
    L = xs_ref.shape[0]
    P = SSD_HEAD_DIM

    @pl.when(pl.program_id(2) == 0)
    def _():
        extx[0:8, :] = jnp.zeros((8, extx.shape[1]), F32)
        extb[0:8, :] = jnp.zeros((8, extb.shape[1]), F32)
        extc[0:8, :] = jnp.zeros((8, extc.shape[1]), F32)
        hs_ref[...] = jnp.zeros(hs_ref.shape, F32)

    def conv_act(x_ref, w_ref, bias_ref, ext):
        xf = x_ref[...].astype(F32)
        ext[8:8 + L, :] = xf
        acc = bias_ref[...] + w_ref[3:4, :] * xf
        for k in (1, 2, 3):
            acc = acc + w_ref[3 - k:4 - k, :] * ext[8 - k:8 - k + L, :]
        ext[0:8, :] = ext[L:L + 8, :]
        return _silu(acc)

    xs = conv_act(xs_ref, cwx_ref, cbx_ref, extx)
    bm = conv_act(b_ref, cwb_ref, cbb_ref, extb)
    cm = conv_act(c_ref, cwc_ref, cbc_ref, extc)

    dt = _softplus(dt_ref[...] + dtb_ref[...])
    da = dt * (-jnp.exp(alog_ref[...]))
    row = lax.broadcasted_iota(I32, (L, L), 0)
    col = lax.broadcasted_iota(I32, (L, L), 1)
    tril = row >= col
    trif = jnp.where(tril, 1.0, 0.0).astype(BF16)
    da_hi, da_lo = _split_bf16(da)
    acum = _dot(trif, da_hi) + _dot(trif, da_lo)
    acum_t = acum.T
    a_end = acum[L - 1:L, :]
    ea = jnp.exp(acum)
    wend = jnp.exp(a_end - acum) * dt
    eend = jnp.exp(a_end)

    cb16 = cm.astype(BF16)
    bt16 = bm.T.astype(BF16)
    cb = _dot(cb16, bt16)

    def head_lanes(v):
        hi, lo = _split_bf16(v)
        return _dot(hi, hexp_ref[...]) + _dot(lo, hexp_ref[...])

    xdt = (xs * head_lanes(dt)).astype(BF16)
    xw = (xs * head_lanes(wend)).astype(BF16)
    y_intra, y_inter = [], []
    for hh in range(SSD_HPG):
        seg = acum[:, hh:hh + 1] - acum_t[hh:hh + 1, :]
        decay = jnp.exp(jnp.where(tril, seg, NEG))
        g = (cb * decay).astype(BF16)
        hprev = hs_ref[hh]
        y_intra.append(_dot(g, xdt[:, hh * P:(hh + 1) * P]))
        y_inter.append(_dot(cb16, hprev.astype(BF16)))
        hs_ref[hh] = hprev * eend[:, hh:hh + 1] + _dot(bt16, xw[:, hh * P:(hh + 1) * P])
    y = jnp.concatenate(y_intra, axis=1) + jnp.concatenate(y_inter, axis=1) * head_lanes(ea)
    y = y + dexp_ref[...] * xs
    y = y * _silu(z_ref[...].astype(F32))
    ms = jnp.mean(y * y, axis=-1, keepdims=True)
    y_ref[...] = (y * lax.rsqrt(ms + EPS) * nw_ref[...]).astype(BF16)


def _ssd(main3, tail3, conv_w, conv_b, dtb, alog, dexp, norm_w):
    B, S, _ = main3.shape
    L = math.gcd(S, SSD_CHUNK)
    nc = S // L
    G, N, GW = SSD_N_GROUPS, SSD_D_STATE, SSD_D_INNER // SSD_N_GROUPS
    xs0, b0, c0 = C_XS // GW, C_B // N, C_C // N
    cb0, cc0 = SSD_D_INNER // N, (SSD_D_INNER + G * N) // N
    hexp = jnp.asarray(np.arange(LANES)[:, None] == np.arange(GW)[None, :] // SSD_HEAD_DIM, BF16)

    def seq(w, off):
        return pl.BlockSpec((None, L, w), lambda b, g, c: (b, c, off + g))

    def par(r, w, off):
        return pl.BlockSpec((r, w), lambda b, g, c: (0, off + g))

    return pl.pallas_call(
        _ssd_kernel,
        grid=(B, G, nc),
        in_specs=[seq(GW, 0), seq(GW, xs0), seq(N, b0), seq(N, c0), seq(LANES, 0),
                  par(4, GW, 0), par(4, N, cb0), par(4, N, cc0),
                  par(1, GW, 0), par(1, N, cb0), par(1, N, cc0),
                  par(1, LANES, 0), par(1, LANES, 0), par(1, GW, 0), par(1, GW, 0),
                  pl.BlockSpec((LANES, GW), lambda b, g, c: (0, 0))],
        out_specs=pl.BlockSpec((None, L, GW), lambda b, g, c: (b, c, g)),
        out_shape=jax.ShapeDtypeStruct((B, S, SSD_D_INNER), BF16),
        scratch_shapes=[pltpu.VMEM((L + 8, GW), F32), pltpu.VMEM((L + 8, N), F32), pltpu.VMEM((L + 8, N), F32),
                        pltpu.VMEM((SSD_HPG, N, SSD_HEAD_DIM), F32)],
        compiler_params=_cparams(("arbitrary", "arbitrary", "arbitrary")),
        name="ssd",
    )(main3, main3, main3, main3, tail3, conv_w, conv_w, conv_w, conv_b, conv_b, conv_b,
      dtb, alog, dexp, norm_w, hexp)


def _nsa_norm_kernel(q_ref, ks_ref, kw_ref, vs_ref, vw_ref, bdq_ref, bdk_ref, wq_ref, wk_ref, srow_ref,
                     qt_ref, kso_ref, kwo_ref, vst_ref, vwt_ref):
    QB, DH, G, HPG = Q_BLOCK, NSA_HEAD_DIM, NSA_N_KV, NSA_HPG
    i = pl.program_id(1)

    def head_norm(x_ref, bd_ref, w_ref):
        x = x_ref[...].astype(F32)
        hi, lo = _split_bf16(x * x)
        ms = (_dot(hi, bd_ref[...]) + _dot(lo, bd_ref[...])) * (1.0 / DH)
        return x * lax.rsqrt(ms + EPS) * w_ref[...]

    qn_t = head_norm(q_ref, bdq_ref, wq_ref).T
    pos = i * QB + lax.broadcasted_iota(I32, (QB, K_AUG - DH), 0)
    lane = lax.broadcasted_iota(I32, (QB, K_AUG - DH), 1)
    lo = pos & 255
    pcols = jnp.where(lane >= 4, 0, jnp.where((lane & 1) == 0, lo, pos - lo)).astype(F32)
    ksn = head_norm(ks_ref, bdk_ref, wk_ref)
    kwn = head_norm(kw_ref, bdk_ref, wk_ref)
    vs_t = vs_ref[...].astype(F32).T
    vw_t = vw_ref[...].astype(F32).T
    ones_rows = jnp.where(lax.broadcasted_iota(I32, (V_AUG - DH, QB), 0) == 0, 1.0, 0.0).astype(BF16)
    for g in range(G):
        heads = [qn_t[(g * HPG + h) * DH:(g * HPG + h + 1) * DH, :] for h in range(HPG)]
        qt_ref[g] = jnp.concatenate([jnp.concatenate(heads, axis=1).astype(BF16), srow_ref[g]], axis=0)
        kso_ref[g] = jnp.concatenate([ksn[:, g * DH:(g + 1) * DH], pcols], axis=1).astype(BF16)
        kwo_ref[g] = jnp.concatenate([kwn[:, g * DH:(g + 1) * DH], pcols], axis=1).astype(BF16)
        vst_ref[g] = jnp.concatenate([vs_t[g * DH:(g + 1) * DH, :].astype(BF16), ones_rows], axis=0)
        vwt_ref[g] = jnp.concatenate([vw_t[g * DH:(g + 1) * DH, :].astype(BF16), ones_rows], axis=0)


def _nsa_norm(main3, bdq, bdk, wq, wk, srow):
    B, S, _ = main3.shape
    QB, DH, G = Q_BLOCK, NSA_HEAD_DIM, NSA_N_KV
    nb = S // QB
    QW, KW, W = NSA_N_HEADS * DH, G * DH, NSA_HPG * QB
    const = lambda shape: pl.BlockSpec(shape, lambda b, i: (0,) * len(shape))
    col = lambda w, c0: pl.BlockSpec((None, QB, w), lambda b, i: (b, i, c0 // w))
    return pl.pallas_call(
        _nsa_norm_kernel,
        grid=(B, nb),
        in_specs=[col(QW, C_Q), col(KW, C_KS), col(KW, C_KW), col(KW, C_VS), col(KW, C_VW),
                  const((QW, QW)), const((KW, KW)), const((1, QW)), const((1, KW)), const(srow.shape)],
        out_specs=[pl.BlockSpec((None, G, None, K_AUG, W), lambda b, i: (b, 0, i, 0, 0)),
                   pl.BlockSpec((None, G, QB, K_AUG), lambda b, i: (b, 0, i, 0)),
                   pl.BlockSpec((None, G, QB, K_AUG), lambda b, i: (b, 0, i, 0)),
                   pl.BlockSpec((None, G, None, V_AUG, QB), lambda b, i: (b, 0, i, 0, 0)),
                   pl.BlockSpec((None, G, None, V_AUG, QB), lambda b, i: (b, 0, i, 0, 0))],
        out_shape=[jax.ShapeDtypeStruct((B, G, nb, K_AUG, W), BF16),
                   jax.ShapeDtypeStruct((B, G, S, K_AUG), BF16), jax.ShapeDtypeStruct((B, G, S, K_AUG), BF16),
                   jax.ShapeDtypeStruct((B, G, nb, V_AUG, QB), BF16),
                   jax.ShapeDtypeStruct((B, G, nb, V_AUG, QB), BF16)],
        compiler_params=_cparams(("arbitrary", "arbitrary")),
        name="nsa_norm_layout",
    )(main3, main3, main3, main3, main3, bdq, bdk, wq, wk, srow)


def _compress_kernel(u_ref, pos_ref, w1_ref, w2_ref, nw_ref, o_ref, *, normalize):
    half = u_ref.shape[1]
    nrow = u_ref.shape[0]
    u = u_ref[...]
    a = _dot(u, w1_ref[0:half, :])
    b = _dot(u, w1_ref[half:2 * half, :])
    posc = _dot(pos_ref[...], w1_ref[...])[0:1, :]
    pre = a + pltpu.roll(b, nrow - 1, 0) + posc
    act = 0.5 * pre * (1.0 + jnp.tanh(math.sqrt(2.0 / math.pi) * (pre + 0.044715 * (pre * pre * pre))))
    o = _dot(act.astype(BF16), w2_ref[...])
    if normalize:
        ms = jnp.mean(o * o, axis=-1, keepdims=True)
        o = o * lax.rsqrt(ms + EPS) * nw_ref[...]
    o_ref[...] = o


def _compress(u, pos8, w1, w2, nw, normalize):
    BG, nrow, half = u.shape
    const = lambda shape: pl.BlockSpec(shape, lambda i: (0, 0))
    return pl.pallas_call(
        functools.partial(_compress_kernel, normalize=normalize),
        grid=(BG,),
        in_specs=[pl.BlockSpec((None, nrow, half), lambda i: (i, 0, 0)),
                  const(pos8.shape), const(w1.shape), const(w2.shape), const(nw.shape)],
        out_specs=pl.BlockSpec((None, nrow, NSA_HEAD_DIM), lambda i: (i, 0, 0)),
        out_shape=jax.ShapeDtypeStruct((BG, nrow, NSA_HEAD_DIM), F32),
        compiler_params=_cparams(("arbitrary",)),
        name="nsa_compress_norm" if normalize else "nsa_compress",
    )(u, pos8, w1, w2, nw)


def _nsa_cmp_kernel(qt_ref, kc_ref, vct_ref, ovt_ref, pm_ref, glt_ref, oct_ref, sb_ref, lst_ref, imp_scr):
    NT = qt_ref.shape[0]
    tiles = range(NT)
    idx = [pl.program_id(2) * NT + t for t in tiles]
    ncmp = kc_ref.shape[0]
    nsel = ovt_ref.shape[0]
    QB = Q_BLOCK
    W = NSA_HPG * QB
    CH = min(LANES, ncmp)
    span = CH * NSA_CMP_STRIDE
    nch = jnp.minimum((idx[-1] * QB + QB - NSA_CMP_LEN) // span + 1, ncmp // CH)

    def attend(rows):
        jrow = lax.broadcasted_iota(I32, (rows, W), 0)
        lane = lax.broadcasted_iota(I32, (rows, W), 1)
        for t in tiles:
            s = _dot(kc_ref[0:rows, :], qt_ref[t])
            mask = (NSA_CMP_STRIDE * jrow + (NSA_CMP_LEN - 1)) <= idx[t] * QB + (lane & (QB - 1))
            sm = jnp.where(mask, s, NEG)
            m = jnp.max(sm, axis=0, keepdims=True)
            p = jnp.where(mask, jnp.exp2(sm - m), 0.0)
            l = jnp.sum(p, axis=0, keepdims=True)
            pn = p * (1.0 / jnp.maximum(l, 1e-30))
            oct_ref[t] = _dot(vct_ref[:, 0:rows], pn.astype(BF16)) * jax.nn.sigmoid(glt_ref[t, 0:1, :])
            psum = pn[:, 0:QB]
            for h in range(1, NSA_HPG):
                psum = psum + pn[:, h * QB:(h + 1) * QB]
            imp_scr[t] = _dot(ovt_ref[:, 0:rows], psum.astype(BF16))

    for k in range(1, ncmp // CH + 1):
        pl.when(nch == k)(functools.partial(attend, k * CH))

    jf = lax.broadcasted_iota(I32, (nsel, QB), 0).astype(F32)
    lane_q = lax.broadcasted_iota(I32, (nsel, QB), 1)
    valid, v0 = [], []
    for t in tiles:
        cur = ((idx[t] * QB + lane_q) >> 6).astype(F32)
        forced = (jf == 0.0) | (jf == cur) | (jf == cur - 1.0)
        valid.append(jf <= cur)
        v0.append(jnp.where(forced, FORCED_SCORE, jnp.where(valid[t], imp_scr[t], -1.0)))

    def pick_one(_, carry):
        out = []
        for v, sel in carry:
            mx = jnp.max(v, axis=0, keepdims=True)
            first = jnp.min(jnp.where(v == mx, jf, float(nsel)), axis=0, keepdims=True)
            pick = jf == first
            out.append((jnp.where(pick, -2.0, v), jnp.where(pick, 1.0, sel)))
        return tuple(out)

    picked = lax.fori_loop(0, min(NSA_SEL_TOP, nsel), pick_one,
                           tuple((v0[t], jnp.zeros((nsel, QB), F32)) for t in tiles))

    npair = nsel // 2
    jp = lax.broadcasted_iota(I32, (npair, LANES), 0)
    r2 = lax.broadcasted_iota(I32, (npair, npair), 0)
    c2 = lax.broadcasted_iota(I32, (npair, npair), 1)
    tri = jnp.where(c2 <= r2, 1.0, 0.0).astype(BF16)
    slot = lax.broadcasted_iota(I32, (npair, LANES), 1).astype(F32)
    jpv = lax.broadcasted_iota(I32, (8, npair), 1).astype(F32).astype(BF16)
    r8 = lax.broadcasted_iota(I32, (8, LANES), 0)
    for t in tiles:
        sel = jnp.where(valid[t], picked[t][1], 0.0)
        sb_ref[t] = jnp.where(sel > 0.0, 0.0, NEG)
        pairsel = _dot(pm_ref[...], sel.astype(BF16))
        need = (jnp.sum(pairsel, axis=1, keepdims=True) > 0.0) & (jp < idx[t])
        needf = jnp.where(need, 1.0, 0.0)
        prefix = _dot(tri, needf.astype(BF16))
        onehot = jnp.where(need & (prefix == slot + 1.0), 1.0, 0.0).astype(BF16)
        lst = _dot(jpv, onehot)
        cnt = _dot(jnp.ones((8, npair), BF16), needf.astype(BF16))
        lst_ref[t] = jnp.where(r8 == 0, lst, cnt).astype(I32)


def _nsa_cmp(qt, kc, vct, ovt, pm, glt):
    B, G, nb, _, W = qt.shape
    ncmp = kc.shape[2]
    nsel = ovt.shape[0]
    NT = NSA_CMP_TILES_PER_STEP
    blk = lambda *shape: pl.BlockSpec((None, None, NT) + shape, lambda b, g, i: (b, g, i) + (0,) * len(shape))
    per_bg = lambda *shape: pl.BlockSpec((None, None) + shape, lambda b, g, i: (b, g) + (0,) * len(shape))
    const = lambda shape: pl.BlockSpec(shape, lambda b, g, i: (0,) * len(shape))
    return pl.pallas_call(
        _nsa_cmp_kernel,
        grid=(B, G, nb // NT),
        in_specs=[blk(K_AUG, W), per_bg(ncmp, K_AUG), per_bg(NSA_HEAD_DIM, ncmp),
                  const(ovt.shape), const(pm.shape), blk(8, W)],
        out_specs=[blk(NSA_HEAD_DIM, W), blk(nsel, Q_BLOCK), blk(8, LANES)],
        out_shape=[jax.ShapeDtypeStruct((B, G, nb, NSA_HEAD_DIM, W), F32),
                   jax.ShapeDtypeStruct((B, G, nb, nsel, Q_BLOCK), F32),
                   jax.ShapeDtypeStruct((B, G, nb, 8, LANES), I32)],
        scratch_shapes=[pltpu.VMEM((NT, nsel, Q_BLOCK), F32)],
        compiler_params=_cparams(("arbitrary", "arbitrary", "arbitrary")),
        name="nsa_cmp_select",
    )(qt, kc, vct, ovt, pm, glt)


def _tile4(r):
    return jnp.concatenate([r] * NSA_HPG, axis=1)


def _nsa_sel_kernel(lst_ref, qt_ref, ks_ref, vst_ref, kw_ref, vwt_ref, sb_ref, glt_ref, oct_ref, o_ref,
                    s_scr, d_scr, ow_scr):
    NT = qt_ref.shape[0]
    QB = Q_BLOCK
    W = NSA_HPG * QB
    HB = NSA_SEL_LEN
    DH = NSA_HEAD_DIM
    NPI = NSA_PAIRS_PER_ITER
    tiles = range(NT)
    qts = [qt_ref[t] for t in tiles]
    counts = [lst_ref[t, 1, 0] for t in tiles]
    idx = [pl.program_id(2) * NT + t for t in tiles]

    def flash(s, vt, carry):
        m, acc = carry
        mn = jnp.maximum(m, jnp.max(s, axis=0, keepdims=True))
        p = jnp.exp2(s - mn)
        return mn, jnp.exp2(m - mn) * acc + _dot(vt, p.astype(BF16))

    def pair_scores(t, k):
        jp = lst_ref[t, 0, k]
        live = k < counts[t]
        s = _dot(ks_ref[pl.ds(pl.multiple_of(jp * QB, QB), QB), :], qts[t])
        b0 = jnp.where(live, _tile4(sb_ref[t, pl.ds(2 * jp, 1), :]), NEG)
        b1 = jnp.where(live, _tile4(sb_ref[t, pl.ds(2 * jp + 1, 1), :]), NEG)
        return jnp.concatenate([s[0:HB] + b0, s[HB:2 * HB] + b1], axis=0)

    def scores(t, it):
        return jnp.concatenate([pair_scores(t, NPI * it + u) for u in range(NPI)], axis=0)

    r = lax.broadcasted_iota(I32, (QB, W), 0)
    c = lax.broadcasted_iota(I32, (QB, W), 1) & (QB - 1)
    for t in tiles:
        i = idx[t]
        s_scr[t, 0] = scores(t, 0)
        kd = ks_ref[pl.ds(pl.multiple_of(i * QB, QB), QB), :]
        d_scr[t] = jnp.where(r <= c, _dot(kd, qts[t]), NEG)

        ss, vts = [], []
        for w in range(NSA_WINDOW // QB + 1):
            pw = i - NSA_WINDOW // QB + w
            pc = jnp.maximum(pw, 0)
            kk = kw_ref[pl.ds(pl.multiple_of(pc * QB, QB), QB), :]
            s = _dot(kk, qts[t])
            if w == 0:
                s = jnp.where((r > c) & (pw >= 0), s, NEG)
            elif w == NSA_WINDOW // QB:
                s = jnp.where(r <= c, s, NEG)
            else:
                s = jnp.where(pw >= 0, s, NEG)
            ss.append(s)
            vts.append(vwt_ref[pc])
        s = jnp.concatenate(ss, axis=0)
        p = jnp.exp2(s - jnp.max(s, axis=0, keepdims=True))
        aw = _dot(jnp.concatenate(vts, axis=1), p.astype(BF16))
        o_win = aw[0:DH] * (1.0 / aw[DH:DH + 1])
        ow_scr[t] = oct_ref[t] + jax.nn.sigmoid(glt_ref[t, 2:3, :]) * o_win

    def body(it, carry):
        slot = it & 1
        cur = [s_scr[t, slot] for t in tiles]
        vts = [jnp.concatenate([vst_ref[lst_ref[t, 0, NPI * it + u]] for u in range(NPI)], axis=1) for t in tiles]
        nxt = [scores(t, it + 1) for t in tiles]
        out = tuple(flash(cur[t], vts[t], carry[t]) for t in tiles)
        for t in tiles:
            s_scr[t, 1 - slot] = nxt[t]
        return out

    n_iter = (functools.reduce(jnp.maximum, counts) + NPI - 1) // NPI
    init = tuple((jnp.full((1, W), NEG, F32), jnp.zeros((vst_ref.shape[1], W), F32)) for _ in tiles)
    carry = lax.fori_loop(0, n_iter, body, init)
    for t in tiles:
        m, acc = flash(d_scr[t], vst_ref[idx[t]], carry[t])
        o_t = ow_scr[t] + jax.nn.sigmoid(glt_ref[t, 1:2, :]) * (acc[0:DH] * (1.0 / acc[DH:DH + 1]))
        o_ref[t * QB:(t + 1) * QB, :] = jnp.concatenate(
            [o_t[:, h * QB:(h + 1) * QB].T for h in range(NSA_HPG)], axis=1).astype(BF16)


def _nsa_sel(lst, qt, ks, vst, kw, vwt, sb, glt, oct):
    B, G, nb, _, W = qt.shape
    S = ks.shape[2]
    npair = vst.shape[2]
    nsel = sb.shape[3]
    NT = NSA_TILES_PER_STEP
    blk = lambda *shape: pl.BlockSpec((None, None, NT) + shape, lambda b, g, i: (b, g, i) + (0,) * len(shape))
    per_bg = lambda *shape: pl.BlockSpec((None, None) + shape, lambda b, g, i: (b, g) + (0,) * len(shape))
    return pl.pallas_call(
        _nsa_sel_kernel,
        grid=(B, G, nb // NT),
        in_specs=[pl.BlockSpec((None, None, NT, 8, LANES), lambda b, g, i: (b, g, i, 0, 0),
                               memory_space=pltpu.SMEM),
                  blk(K_AUG, W), per_bg(S, K_AUG), per_bg(npair, V_AUG, Q_BLOCK),
                  per_bg(S, K_AUG), per_bg(npair, V_AUG, Q_BLOCK),
                  blk(nsel, Q_BLOCK), blk(8, W), blk(NSA_HEAD_DIM, W)],
        out_specs=pl.BlockSpec((None, NT * Q_BLOCK, NSA_HPG * NSA_HEAD_DIM), lambda b, g, i: (b, i, g)),
        out_shape=jax.ShapeDtypeStruct((B, S, NSA_N_HEADS * NSA_HEAD_DIM), BF16),
        scratch_shapes=[pltpu.VMEM((NT, 2, NSA_PAIRS_PER_ITER * Q_BLOCK, W), F32),
                        pltpu.VMEM((NT, Q_BLOCK, W), F32), pltpu.VMEM((NT, NSA_HEAD_DIM, W), F32)],
        compiler_params=_cparams(("arbitrary", "arbitrary", "arbitrary")),
        name="nsa_select_window",
    )(lst, qt, ks, vst, kw, vwt, sb, glt, oct)


ROW_TILE = 8


def _store_row_tiles(ref, x):
    n = x.shape[0]
    for c in range(ROW_TILE):
        ref[pl.ds(c, n, stride=ROW_TILE), :] = x[:, c * LANES:(c + 1) * LANES]


def _load_row_tiles(ref, start, n):
    return jnp.concatenate([ref[pl.ds(start + c, n, stride=ROW_TILE), :] for c in range(ROW_TILE)], axis=1)


def _merge_kernel(x_ref, ys_ref, on_ref, gs_ref, gn_ref, wssd_ref, wnsa_ref, wout_ref, ln2_ref,
                  rwh_ref, rwl_ref, rb_ref, x1_ref, xn2_ref, route_ref, cnt_ref, base_ref):
    tm = x_ref.shape[0]

    @pl.when(pl.program_id(0) == 0)
    def _():
        base_ref[...] = jnp.zeros(base_ref.shape, F32)

    y_ssd = _dot(ys_ref[...], wssd_ref[...])
    y_nsa = _dot(on_ref[...], wnsa_ref[...])
    merged = (jax.nn.sigmoid(gs_ref[...].astype(F32)) * y_ssd
              + jax.nn.sigmoid(gn_ref[...].astype(F32)) * y_nsa)
    x1 = x_ref[...] + _dot(merged.astype(BF16), wout_ref[...])
    x1_ref[...] = x1
    ms = jnp.mean(x1 * x1, axis=-1, keepdims=True)
    xn2 = x1 * lax.rsqrt(ms + EPS) * ln2_ref[...]
    _store_row_tiles(xn2_ref, xn2)

    xh, xl = _split_bf16(xn2)
    logits = _dot(xh, rwh_ref[...]) + _dot(xl, rwh_ref[...]) + _dot(xh, rwl_ref[...]) + rb_ref[...]
    lane = lax.broadcasted_iota(I32, (tm, LANES), 1)
    lanef = lane.astype(F32)
    v = logits
    onehot = jnp.zeros((tm, LANES), F32)
    vals, picks = [], []
    for _ in range(TOP_K):
        mx = jnp.max(v, axis=-1, keepdims=True)
        first = jnp.min(jnp.where(v == mx, lanef, float(LANES)), axis=-1, keepdims=True)
        pick = lanef == first
        v = jnp.where(pick, 2.0 * NEG, v)
        onehot = jnp.where(pick, 1.0, onehot)
        vals.append(mx)
        picks.append((pick, first))
    es = [jnp.exp(val - vals[0]) for val in vals]
    inv = 1.0 / (es[0] + es[1] + es[2] + es[3])

    r2 = lax.broadcasted_iota(I32, (tm, tm), 0)
    c2 = lax.broadcasted_iota(I32, (tm, tm), 1)
    stril = jnp.where(c2 < r2, 1.0, 0.0).astype(BF16)
    posmap = base_ref[...] + _dot(stril, onehot.astype(BF16))
    route = jnp.zeros((tm, LANES), F32)
    for k in range(TOP_K):
        pick, first = picks[k]
        pos = jnp.sum(jnp.where(pick, posmap, 0.0), axis=-1, keepdims=True)
        route = jnp.where(lane == k, first, route)
        route = jnp.where(lane == TOP_K + k, pos, route)
        route = jnp.where(lane == 2 * TOP_K + k, es[k] * inv, route)
    route_ref[...] = route
    base = base_ref[...] + jnp.sum(onehot, axis=0, keepdims=True)
    base_ref[...] = base
    cnt_ref[...] = jnp.broadcast_to(base, cnt_ref.shape)


def _merge(x2, yssd, onsa, main, wssd, wnsa, wout, ln2, rwh, rwl, rb, tm):
    T = x2.shape[0]
    const = lambda shape: pl.BlockSpec(shape, lambda i: (0, 0))
    row = lambda w, j=0: pl.BlockSpec((tm, w), lambda i: (i, j))
    return pl.pallas_call(
        _merge_kernel,
        grid=(T // tm,),
        in_specs=[row(D_MODEL), row(SSD_D_INNER), row(D_MODEL), row(D_MODEL, C_MS // D_MODEL),
                  row(D_MODEL, C_MN // D_MODEL),
                  const(wssd.shape), const(wnsa.shape), const(wout.shape), const(ln2.shape),
                  const(rwh.shape), const(rwl.shape), const(rb.shape)],
        out_specs=[row(D_MODEL), pl.BlockSpec((tm * ROW_TILE, LANES), lambda i: (i, 0)), row(LANES),
                   const((8, LANES))],
        out_shape=[jax.ShapeDtypeStruct((T, D_MODEL), F32), jax.ShapeDtypeStruct((T * ROW_TILE, LANES), F32),
                   jax.ShapeDtypeStruct((T, LANES), F32), jax.ShapeDtypeStruct((8, LANES), F32)],
        scratch_shapes=[pltpu.VMEM((1, LANES), F32)],
        compiler_params=_cparams(("arbitrary",)),
        name="merge_router",
    )(x2, yssd, onsa, main, main, wssd, wnsa, wout, ln2, rwh, rwl, rb)


def _dispatch_kernel(dest_ref, xn_ref, xs_init, xs_hbm, sem):
    del xs_init
    RT = ROW_TILE
    tc = xn_ref.shape[0] // RT

    def issue(r2, c):
        for pr in (0, 1):
            r = 2 * r2 + pr
            src = pl.multiple_of((r & (tc - 1)) * RT, RT)
            dst = pl.multiple_of(dest_ref[0, r] * RT, RT)
            pltpu.make_async_copy(xn_ref.at[pl.ds(src, RT)], xs_hbm.at[pl.ds(dst, RT)], sem).start(priority=pr)
        return c

    lax.fori_loop(0, TOP_K * tc // 2, issue, 0, unroll=4)
    for _ in range(TOP_K):
        pltpu.make_async_copy(xn_ref, xs_hbm.at[pl.ds(0, tc * RT)], sem).wait()


def _dispatch(dest3, xn2, n_rows, tc):
    T = xn2.shape[0] // ROW_TILE
    xs0 = jnp.zeros((n_rows * ROW_TILE, LANES), F32)
    return pl.pallas_call(
        _dispatch_kernel,
        grid=(T // tc,),
        in_specs=[pl.BlockSpec((None, 1, TOP_K * tc), lambda i: (i, 0, 0), memory_space=pltpu.SMEM),
                  pl.BlockSpec((tc * ROW_TILE, LANES), lambda i: (i, 0)),
                  pl.BlockSpec(memory_space=pl.ANY)],
        out_specs=pl.BlockSpec(memory_space=pl.ANY),
        out_shape=jax.ShapeDtypeStruct(xs0.shape, F32),
        scratch_shapes=[pltpu.SemaphoreType.DMA(())],
        input_output_aliases={2: 0},
        compiler_params=_cparams(("arbitrary",)),
        name="moe_dispatch",
    )(dest3, xn2, xs0)


def _expert_kernel(be_ref, nu_ref, xs_ref, w1g_ref, w1l_ref, b1g_ref, b1l_ref, w2_ref, b2_ref, ys_ref):
    bm = xs_ref.shape[0] // ROW_TILE
    blk = pl.program_id(0)

    @pl.when(blk < nu_ref[0])
    def _():
        x = _load_row_tiles(xs_ref, 0, bm).astype(BF16)
        glu = jnp.minimum(_dot(x, w1g_ref[...]) + b1g_ref[...], SWIGLU_LIMIT)
        lin = jnp.clip(_dot(x, w1l_ref[...]) + b1l_ref[...], -SWIGLU_LIMIT, SWIGLU_LIMIT)
        act = glu * jax.nn.sigmoid(SWIGLU_ALPHA * glu) * (lin + 1.0)
        _store_row_tiles(ys_ref, _dot(act.astype(BF16), w2_ref[...]) + b2_ref[...])

    @pl.when(blk >= nu_ref[0])
    def _():
        ys_ref[...] = jnp.zeros(ys_ref.shape, F32)


def _experts(block_e, n_used, xs, w1p, b1g, b1l, w2, b2, bm):
    n_blocks = xs.shape[0] // (bm * ROW_TILE)
    DE = w2.shape[1]
    wspec = lambda r, c, j=0: pl.BlockSpec((None, r, c), lambda b, be, nu: (be[b], 0, j))
    rows = pl.BlockSpec((bm * ROW_TILE, LANES), lambda b, be, nu: (b, 0))
    return pl.pallas_call(
        _expert_kernel,
        grid_spec=pltpu.PrefetchScalarGridSpec(
            num_scalar_prefetch=2,
            grid=(n_blocks,),
            in_specs=[rows, wspec(D_MODEL, DE, 0), wspec(D_MODEL, DE, 1), wspec(1, DE), wspec(1, DE),
                      wspec(DE, D_MODEL), wspec(1, D_MODEL)],
            out_specs=rows),
        out_shape=jax.ShapeDtypeStruct(xs.shape, F32),
        compiler_params=_cparams(("arbitrary",)),
        name="moe_experts",
    )(block_e, n_used, xs, w1p, w1p, b1g, b1l, w2, b2)


def _deinterleave_kernel(w_ref, p_ref, o_ref):
    CW = p_ref.shape[0]
    half = w_ref.shape[1] // 2
    for c in range(w_ref.shape[1] // CW):
        y = _dot(w_ref[:, c * CW:(c + 1) * CW].astype(BF16), p_ref[...]).astype(BF16)
        o_ref[:, c * LANES:(c + 1) * LANES] = y[:, 0:LANES]
        o_ref[:, half + c * LANES:half + (c + 1) * LANES] = y[:, LANES:CW]


def _deinterleave(w1):
    E, D, W = w1.shape
    CW = 2 * LANES
    src = np.concatenate([np.arange(0, CW, 2), np.arange(1, CW, 2)])
    perm = jnp.asarray(np.arange(CW)[:, None] == src[None, :], BF16)
    rows = D // 2
    return pl.pallas_call(
        _deinterleave_kernel,
        grid=(E, D // rows),
        in_specs=[pl.BlockSpec((None, rows, W), lambda e, r: (e, r, 0)),
                  pl.BlockSpec((CW, CW), lambda e, r: (0, 0))],
        out_specs=pl.BlockSpec((None, rows, W), lambda e, r: (e, r, 0)),
        out_shape=jax.ShapeDtypeStruct((E, D, W), BF16),
        compiler_params=_cparams(("arbitrary", "arbitrary")),
        name="moe_w1_deinterleave",
    )(w1, perm)


def _combine_kernel(dest_ref, destn_ref, route_ref, x1_ref, ys_hbm, o_ref, buf, sem):
    tc = x1_ref.shape[0]
    i = pl.program_id(0)
    slot = i & 1

    RT = ROW_TILE

    def gather(dest_smem, s):
        def issue(r2, c):
            for pr in (0, 1):
                r = 2 * r2 + pr
                src = pl.multiple_of(dest_smem[0, r] * RT, RT)
                dst = pl.multiple_of(r * RT, RT)
                pltpu.make_async_copy(ys_hbm.at[pl.ds(src, RT)], buf.at[s, pl.ds(dst, RT)],
                                      sem.at[s]).start(priority=pr)
            return c

        lax.fori_loop(0, TOP_K * tc // 2, issue, 0, unroll=4)

    @pl.when(i == 0)
    def _():
        gather(dest_ref, 0)

    for s in (0, 1):
        @pl.when((i + 1 < pl.num_programs(0)) & (slot == 1 - s))
        def _():
            gather(destn_ref, s)

    pltpu.make_async_copy(ys_hbm.at[pl.ds(0, TOP_K * tc * RT)], buf.at[slot], sem.at[slot]).wait()
    route = route_ref[...]
    acc = x1_ref[...]
    for k in range(TOP_K):
        acc = acc + route[:, 2 * TOP_K + k:2 * TOP_K + k + 1] * _load_row_tiles(buf.at[slot], k * tc * RT, tc)
    o_ref[...] = acc


def _combine(dest3, route, x1, ys, tc):
    T = x1.shape[0]
    n = T // tc
    return pl.pallas_call(
        _combine_kernel,
        grid=(n,),
        in_specs=[pl.BlockSpec((None, 1, TOP_K * tc), lambda i: (i, 0, 0), memory_space=pltpu.SMEM),
                  pl.BlockSpec((None, 1, TOP_K * tc), lambda i: (jnp.minimum(i + 1, n - 1), 0, 0),
                               memory_space=pltpu.SMEM),
                  pl.BlockSpec((tc, LANES), lambda i: (i, 0)),
                  pl.BlockSpec((tc, D_MODEL), lambda i: (i, 0)),
                  pl.BlockSpec(memory_space=pl.ANY)],
        out_specs=pl.BlockSpec((tc, D_MODEL), lambda i: (i, 0)),
        out_shape=jax.ShapeDtypeStruct((T, D_MODEL), F32),
        scratch_shapes=[pltpu.VMEM((2, TOP_K * tc * ROW_TILE, LANES), F32), pltpu.SemaphoreType.DMA((2,))],
        compiler_params=_cparams(("arbitrary",)),
        name="moe_combine",
    )(dest3, dest3, route, x1, ys)


def _split_pos(pos):
    lo = pos % 256
    return lo.astype(np.float32), (pos - lo).astype(np.float32)


def _aug_keys(k, pos):
    lo, hi = _split_pos(pos)
    cols = np.stack([lo, hi, lo, hi] + [np.zeros_like(lo)] * (K_AUG - NSA_HEAD_DIM - 4), axis=-1)
    cols = jnp.broadcast_to(jnp.asarray(cols, BF16), k.shape[:-1] + (cols.shape[-1],))
    return jnp.concatenate([k, cols], axis=-1)


def _slope_rows():
    slopes = (2.0 ** (-8.0 * np.arange(1, NSA_N_HEADS + 1) / NSA_N_HEADS)).astype(np.float32)
    sl = jnp.asarray(np.repeat((slopes * np.float32(LOG2E)).reshape(NSA_N_KV, NSA_HPG), Q_BLOCK, axis=1), F32)
    hi, lo = _split_bf16(sl)
    zero = jnp.zeros_like(hi)
    return jnp.stack([hi, hi, lo, lo] + [zero] * (K_AUG - NSA_HEAD_DIM - 4), axis=1)


def _nsa(main3, tail3, q_norm_w, k_norm_w, cmp_pos_k, cmp_pos_v, cmp_k_w1, cmp_k_w2, cmp_v_w1, cmp_v_w2):
    B, S, _ = main3.shape
    T = B * S
    G, HPG, DH, QB = NSA_N_KV, NSA_HPG, NSA_HEAD_DIM, Q_BLOCK
    nb, npair, nsel, ncmp = S // QB, S // QB, S // NSA_SEL_LEN, S // NSA_CMP_STRIDE
    main = main3.reshape(T, MAIN_W)

    hd = np.arange(NSA_N_HEADS * DH) // DH
    bdq = jnp.asarray(hd[:, None] == hd[None, :], BF16)
    bdk = bdq[:G * DH, :G * DH]
    wq = (jnp.tile(q_norm_w.astype(F32), NSA_N_HEADS) * (DH ** -0.5 * LOG2E)).reshape(1, -1)
    wk = jnp.tile(k_norm_w.astype(F32), G).reshape(1, -1)
    qt, ks, kw, vst, vwt = _nsa_norm(main3, bdq, bdk, wq, wk, _slope_rows())

    def kv_heads(t):
        return t.reshape(B, S, G, DH).transpose(0, 2, 1, 3)

    def halves(c0):
        return kv_heads(main[:, c0:c0 + G * DH]).reshape(B * G, ncmp, NSA_CMP_STRIDE * DH)

    def pos8(pos):
        return jnp.zeros((8, NSA_CMP_LEN * DH), F32).at[0].set(pos.reshape(-1)).astype(BF16)

    ones = jnp.ones((1, DH), F32)
    kc = _compress(halves(C_KC), pos8(cmp_pos_k), cmp_k_w1.astype(BF16), cmp_k_w2.astype(BF16),
                   k_norm_w.astype(F32).reshape(1, DH), True)
    vc = _compress(halves(C_VC), pos8(cmp_pos_v), cmp_v_w1.astype(BF16), cmp_v_w2.astype(BF16), ones, False)
    cmp_end = np.arange(ncmp) * NSA_CMP_STRIDE + NSA_CMP_LEN - 1
    kc = _aug_keys(kc.astype(BF16).reshape(B, G, ncmp, DH), cmp_end)
    vct = vc.astype(BF16).reshape(B, G, ncmp, DH).transpose(0, 1, 3, 2)

    c_start = (np.arange(ncmp) * NSA_CMP_STRIDE)[:, None]
    s_start = (np.arange(nsel) * NSA_SEL_LEN)[None, :]
    overlap = np.clip(np.minimum(c_start + NSA_CMP_LEN, s_start + NSA_SEL_LEN)
                      - np.maximum(c_start, s_start), 0, None) / NSA_CMP_LEN
    overlap[(S - NSA_CMP_LEN) // NSA_CMP_STRIDE + 1:] = 0.0
    ovt = jnp.asarray(overlap.T, BF16)
    pm = jnp.asarray(np.arange(nsel)[None, :] // 2 == np.arange(nsel // 2)[:, None], BF16)

    gl = tail3.reshape(B, nb, QB, G, LANES)[..., 8:8 + 3 * HPG].reshape(B, nb, QB, G, HPG, 3)
    glt = gl.transpose(0, 3, 1, 5, 4, 2).reshape(B, G, nb, 3, HPG * QB)
    glt = jnp.concatenate([glt, jnp.zeros((B, G, nb, 5, HPG * QB), F32)], axis=3)

    oct, sb, lst = _nsa_cmp(qt, kc, vct, ovt, pm, glt)
    return _nsa_sel(lst, qt, ks, vst, kw, vwt, sb, glt, oct).reshape(T, NSA_N_HEADS * DH)


def _moe_tables(route, cnt, T, bm):
    idx = route[:, 0:TOP_K].astype(I32)
    pos = route[:, TOP_K:2 * TOP_K].astype(I32)
    counts = cnt[0, :N_EXPERTS].astype(I32)
    padded = (counts + bm - 1) // bm * bm
    pend = jnp.cumsum(padded)
    pstart = pend - padded
    dest = pstart[idx] + pos
    n_blocks = -(-(T * TOP_K + N_EXPERTS * (bm - 1)) // bm)
    b_start = jnp.arange(n_blocks, dtype=I32) * bm
    block_e = jnp.minimum(jnp.sum(b_start[:, None] >= pend[None, :], axis=1), N_EXPERTS - 1).astype(I32)
    n_used = (pend[-1] // bm).astype(I32).reshape(1)
    return dest, n_blocks * bm, block_e, n_used


def kernel(x, ln1_w, w_in, ssd_conv_w, ssd_conv_b, ssd_dt_bias, ssd_a_log, ssd_d, ssd_norm_w, ssd_out_w,
           nsa_q_norm_w, nsa_k_norm_w, cmp_pos_k, cmp_pos_v, cmp_k_w1, cmp_k_w2, cmp_v_w1, cmp_v_w2,
           nsa_out_w, w_out, ln2_w, router_w, router_b, exp_w1, exp_b1, exp_w2, exp_b2):
    B, S, D = x.shape
    T = B * S
    depth = ln1_w.shape[0]
    x2 = x.reshape(T, D)
    G = SSD_N_GROUPS

    tail_src = np.zeros((TAIL_W,), np.int32)
    tail_on = np.zeros((TAIL_W,), bool)
    for g in range(G):
        tail_src[g * LANES:g * LANES + SSD_HPG] = O_DT + SSD_HPG * g + np.arange(SSD_HPG)
        tail_src[g * LANES + 8:g * LANES + 8 + 3 * NSA_HPG] = O_GATE + 3 * NSA_HPG * g + np.arange(3 * NSA_HPG)
        tail_on[g * LANES:g * LANES + 8 + 3 * NSA_HPG] = True
    main_parts = ((0, O_DT), (O_MERGE, O_MERGE + 2 * D_MODEL), (O_Q, O_GATE))

    def group_lanes(v):
        out = jnp.zeros((G, LANES), F32).at[:, :SSD_HPG].set(v.astype(F32).reshape(G, SSD_HPG))
        return out.reshape(1, TAIL_W)

    tm_rows = min(1024, T)
    bm = 512
    for l in range(depth):
        w_main = jnp.concatenate([w_in[l][:, a:b].astype(BF16) for a, b in main_parts], axis=1)
        w_tail = jnp.where(tail_on[None, :], w_in[l][:, tail_src], 0.0).astype(BF16)
        main, tail = _inproj(x2, ln1_w[l].reshape(1, D), w_main, w_tail, tm=tm_rows, tn=MAIN_W // 4)
        main3 = main.reshape(B, S, MAIN_W)
        tail3 = tail.reshape(B, S, TAIL_W)

        yssd = _ssd(main3, tail3, ssd_conv_w[l], ssd_conv_b[l].reshape(1, -1),
                    group_lanes(ssd_dt_bias[l]), group_lanes(ssd_a_log[l]),
                    jnp.repeat(ssd_d[l].astype(F32), SSD_HEAD_DIM).reshape(1, -1),
                    ssd_norm_w[l].reshape(1, -1))
        onsa = _nsa(main3, tail3, nsa_q_norm_w[l], nsa_k_norm_w[l], cmp_pos_k[l], cmp_pos_v[l],
                    cmp_k_w1[l], cmp_k_w2[l], cmp_v_w1[l], cmp_v_w2[l])

        rw = jnp.zeros((D, LANES), F32).at[:, :N_EXPERTS].set(router_w[l])
        rwh, rwl = _split_bf16(rw)
        rb = jnp.full((1, LANES), NEG, F32).at[0, :N_EXPERTS].set(router_b[l])
        x1, xn2, route, cnt = _merge(x2, yssd.reshape(T, -1), onsa, main,
                                     ssd_out_w[l].astype(BF16), nsa_out_w[l].astype(BF16), w_out[l].astype(BF16),
                                     ln2_w[l].reshape(1, D), rwh, rwl, rb, tm=min(256, T))

        dest, n_rows, block_e, n_used = _moe_tables(route, cnt, T, bm)
        tc = min(256, T)
        dest3 = dest.reshape(T // tc, tc, TOP_K).transpose(0, 2, 1).reshape(T // tc, 1, TOP_K * tc)
        xs = _dispatch(dest3, xn2, n_rows, tc)
        ys = _experts(block_e, n_used, xs, _deinterleave(exp_w1[l]),
                      exp_b1[l][:, None, 0::2], exp_b1[l][:, None, 1::2],
                      exp_w2[l].astype(BF16), exp_b2[l][:, None, :], bm)
        x2 = _combine(dest3, route, x1, ys, tc)
    return x2.reshape(B, S, D)
```

```python
import functools
import math

import numpy as np
import jax
import jax.numpy as jnp
from jax import lax
from jax.experimental import pallas as pl
from jax.experimental.pallas import tpu as pltpu

F32 = jnp.float32
BF16 = jnp.bfloat16
I32 = jnp.int32

D_MODEL = 1024
SSD_D_INNER = 2048
SSD_HEAD_DIM = 64
SSD_N_HEADS = 32
SSD_N_GROUPS = 4
SSD_D_STATE = 128
SSD_CHUNK = 256
SSD_HPG = SSD_N_HEADS // SSD_N_GROUPS
NSA_N_HEADS = 16
NSA_N_KV = 4
NSA_HPG = NSA_N_HEADS // NSA_N_KV
NSA_HEAD_DIM = 64
NSA_CMP_LEN = 32
NSA_CMP_STRIDE = 16
NSA_CMP_HIDDEN = 256
NSA_SEL_LEN = 64
NSA_SEL_TOP = 16
NSA_WINDOW = 512
Q_BLOCK = 128
FORCED_SCORE = 1.0e4
N_EXPERTS = 32
TOP_K = 4
SWIGLU_LIMIT = 7.0
SWIGLU_ALPHA = 1.702
EPS = 1e-6

NEG = -1.0e30
LOG2E = 1.4426950408889634
LANES = 128
K_AUG = 80
V_AUG = 80
NSA_PAIRS_PER_ITER = 2
NSA_TILES_PER_STEP = 2
NSA_CMP_TILES_PER_STEP = 2
VMEM_LIMIT = 56 * 1024 * 1024

C_Z, C_XS, C_B, C_C, C_MS, C_MN, C_Q, C_KC, C_VC, C_KS, C_VS, C_KW, C_VW = (
    0, 2048, 4096, 4608, 5120, 6144, 7168, 8192, 8448, 8704, 8960, 9216, 9472)
MAIN_W = 9728
TAIL_W = 512
O_DT, O_Q, O_GATE, O_MERGE = 5120, 5152, 7712, 7760


def _cparams(sem):
    return pltpu.CompilerParams(dimension_semantics=sem, vmem_limit_bytes=VMEM_LIMIT)


def _dot(a, b):
    return jnp.dot(a, b, preferred_element_type=F32)


def _split_bf16(x):
    hi = x.astype(BF16)
    lo = (x - hi.astype(F32)).astype(BF16)
    return hi, lo


def _inproj_kernel(x_ref, lnw_ref, w_ref, wt_ref, main_ref, tail_ref, xn_ref):
    @pl.when(pl.program_id(1) == 0)
    def _():
        x = x_ref[...]
        ms = jnp.mean(x * x, axis=-1, keepdims=True)
        xn = (x * lax.rsqrt(ms + EPS) * lnw_ref[...]).astype(BF16)
        xn_ref[...] = xn
        tail_ref[...] = _dot(xn, wt_ref[...])

    main_ref[...] = _dot(xn_ref[...], w_ref[...]).astype(BF16)


def _inproj(x2, ln_w, w_main, w_tail, tm, tn):
    T = x2.shape[0]
    return pl.pallas_call(
        _inproj_kernel,
        grid=(T // tm, MAIN_W // tn),
        in_specs=[pl.BlockSpec((tm, D_MODEL), lambda i, j: (i, 0)),
                  pl.BlockSpec((1, D_MODEL), lambda i, j: (0, 0)),
                  pl.BlockSpec((D_MODEL, tn), lambda i, j: (0, j)),
                  pl.BlockSpec((D_MODEL, TAIL_W), lambda i, j: (0, 0))],
        out_specs=[pl.BlockSpec((tm, tn), lambda i, j: (i, j)),
                   pl.BlockSpec((tm, TAIL_W), lambda i, j: (i, 0))],
        out_shape=[jax.ShapeDtypeStruct((T, MAIN_W), BF16),
                   jax.ShapeDtypeStruct((T, TAIL_W), F32)],
        scratch_shapes=[pltpu.VMEM((tm, D_MODEL), BF16)],
        compiler_params=_cparams(("arbitrary", "arbitrary")),
        name="inproj",
    )(x2, ln_w, w_main, w_tail)


def _softplus(x):
    return jnp.maximum(x, 0.0) + jnp.log1p(jnp.exp(-jnp.abs(x)))


def _silu(x):
    return x * (0.5 * jnp.tanh(0.5 * x) + 0.5)


def _ssd_kernel(z_ref, xs_ref, b_ref, c_ref, dt_ref, cwx_ref, cwb_ref, cwc_ref, cbx_ref, cbb_ref, cbc_ref,
                dtb_ref, alog_ref, dexp_ref, nw_ref, hexp_ref, y_ref, extx, extb, extc, hs_ref):
    L = xs_ref.shape[0]
    P = SSD_HEAD_DIM

    @pl.when(pl.program_id(2) == 0)
    def _():
        extx[0:8, :] = jnp.zeros((8, extx.shape[1]), F32)
        extb[0:8, :] = jnp.zeros((8, extb.shape[1]), F32)
        extc[0:8, :] = jnp.zeros((8, extc.shape[1]), F32)
        hs_ref[...] = jnp.zeros(hs_ref.shape, F32)

    def conv_act(x_ref, w_ref, bias_ref, ext):
        xf = x_ref[...].astype(F32)
        ext[8:8 + L, :] = xf
        acc = bias_ref[...] + w_ref[3:4, :] * xf
        for k in (1, 2, 3):
            acc = acc + w_ref[3 - k:4 - k, :] * ext[8 - k:8 - k + L, :]
        ext[0:8, :] = ext[L:L + 8, :]
        return _silu(acc)

    xs = conv_act(xs_ref, cwx_ref, cbx_ref, extx)
    bm = conv_act(b_ref, cwb_ref, cbb_ref, extb)
    cm = conv_act(c_ref, cwc_ref, cbc_ref, extc)

    dt = _softplus(dt_ref[...] + dtb_ref[...])
    da = dt * (-jnp.exp(alog_ref[...]))
    row = lax.broadcasted_iota(I32, (L, L), 0)
    col = lax.broadcasted_iota(I32, (L, L), 1)
    tril = row >= col
    trif = jnp.where(tril, 1.0, 0.0).astype(BF16)
    da_hi, da_lo = _split_bf16(da)
    acum = _dot(trif, da_hi) + _dot(trif, da_lo)
    acum_t = acum.T
    a_end = acum[L - 1:L, :]
    ea = jnp.exp(acum)
    wend = jnp.exp(a_end - acum) * dt
    eend = jnp.exp(a_end)

    cb16 = cm.astype(BF16)
    bt16 = bm.T.astype(BF16)
    cb = _dot(cb16, bt16)

    def head_lanes(v):
        hi, lo = _split_bf16(v)
        return _dot(hi, hexp_ref[...]) + _dot(lo, hexp_ref[...])

    xdt = (xs * head_lanes(dt)).astype(BF16)
    xw = (xs * head_lanes(wend)).astype(BF16)
    y_intra, y_inter = [], []
    for hh in range(SSD_HPG):
        seg = acum[:, hh:hh + 1] - acum_t[hh:hh + 1, :]
        decay = jnp.exp(jnp.where(tril, seg, NEG))
        g = (cb * decay).astype(BF16)
        hprev = hs_ref[hh]
        y_intra.append(_dot(g, xdt[:, hh * P:(hh + 1) * P]))
        y_inter.append(_dot(cb16, hprev.astype(BF16)))
        hs_ref[hh] = hprev * eend[:, hh:hh + 1] + _dot(bt16, xw[:, hh * P:(hh + 1) * P])
    y = jnp.concatenate(y_intra, axis=1) + jnp.concatenate(y_inter, axis=1) * head_lanes(ea)
    y = y + dexp_ref[...] * xs
    y = y * _silu(z_ref[...].astype(F32))
    ms = jnp.mean(y * y, axis=-1, keepdims=True)
    y_ref[...] = (y * lax.rsqrt(ms + EPS) * nw_ref[...]).astype(BF16)


def _ssd(main3, tail3, conv_w, conv_b, dtb, alog, dexp, norm_w):
    B, S, _ = main3.shape
    L = math.gcd(S, SSD_CHUNK)
    nc = S // L
    G, N, GW = SSD_N_GROUPS, SSD_D_STATE, SSD_D_INNER // SSD_N_GROUPS
    xs0, b0, c0 = C_XS // GW, C_B // N, C_C // N
    cb0, cc0 = SSD_D_INNER // N, (SSD_D_INNER + G * N) // N
    hexp = jnp.asarray(np.arange(LANES)[:, None] == np.arange(GW)[None, :] // SSD_HEAD_DIM, BF16)

    def seq(w, off):
        return pl.BlockSpec((None, L, w), lambda b, g, c: (b, c, off + g))

    def par(r, w, off):
        return pl.BlockSpec((r, w), lambda b, g, c: (0, off + g))

    return pl.pallas_call(
        _ssd_kernel,
        grid=(B, G, nc),
        in_specs=[seq(GW, 0), seq(GW, xs0), seq(N, b0), seq(N, c0), seq(LANES, 0),
                  par(4, GW, 0), par(4, N, cb0), par(4, N, cc0),
                  par(1, GW, 0), par(1, N, cb0), par(1, N, cc0),
                  par(1, LANES, 0), par(1, LANES, 0), par(1, GW, 0), par(1, GW, 0),
                  pl.BlockSpec((LANES, GW), lambda b, g, c: (0, 0))],
        out_specs=pl.BlockSpec((None, L, GW), lambda b, g, c: (b, c, g)),
        out_shape=jax.ShapeDtypeStruct((B, S, SSD_D_INNER), BF16),
        scratch_shapes=[pltpu.VMEM((L + 8, GW), F32), pltpu.VMEM((L + 8, N), F32), pltpu.VMEM((L + 8, N), F32),
                        pltpu.VMEM((SSD_HPG, N, SSD_HEAD_DIM), F32)],
        compiler_params=_cparams(("arbitrary", "arbitrary", "arbitrary")),
        name="ssd",
    )(main3, main3, main3, main3, tail3, conv_w, conv_w, conv_w, conv_b, conv_b, conv_b,
      dtb, alog, dexp, norm_w, hexp)


def _nsa_norm_kernel(q_ref, ks_ref, kw_ref, vs_ref, vw_ref, bdq_ref, bdk_ref, wq_ref, wk_ref, srow_ref,
                     qt_ref, kso_ref, kwo_ref, vst_ref, vwt_ref):
    QB, DH, G, HPG = Q_BLOCK, NSA_HEAD_DIM, NSA_N_KV, NSA_HPG
    i = pl.program_id(1)

    def head_norm(x_ref, bd_ref, w_ref):
        x = x_ref[...].astype(F32)
        hi, lo = _split_bf16(x * x)
        ms = (_dot(hi, bd_ref[...]) + _dot(lo, bd_ref[...])) * (1.0 / DH)
        return x * lax.rsqrt(ms + EPS) * w_ref[...]

    qn_t = head_norm(q_ref, bdq_ref, wq_ref).T
    pos = i * QB + lax.broadcasted_iota(I32, (QB, K_AUG - DH), 0)
    lane = lax.broadcasted_iota(I32, (QB, K_AUG - DH), 1)
    lo = pos & 255
    pcols = jnp.where(lane >= 4, 0, jnp.where((lane & 1) == 0, lo, pos - lo)).astype(F32)
    ksn = head_norm(ks_ref, bdk_ref, wk_ref)
    kwn = head_norm(kw_ref, bdk_ref, wk_ref)
    vs_t = vs_ref[...].astype(F32).T
    vw_t = vw_ref[...].astype(F32).T
    ones_rows = jnp.where(lax.broadcasted_iota(I32, (V_AUG - DH, QB), 0) == 0, 1.0, 0.0).astype(BF16)
    for g in range(G):
        heads = [qn_t[(g * HPG + h) * DH:(g * HPG + h + 1) * DH, :] for h in range(HPG)]
        qt_ref[g] = jnp.concatenate([jnp.concatenate(heads, axis=1).astype(BF16), srow_ref[g]], axis=0)
        kso_ref[g] = jnp.concatenate([ksn[:, g * DH:(g + 1) * DH], pcols], axis=1).astype(BF16)
        kwo_ref[g] = jnp.concatenate([kwn[:, g * DH:(g + 1) * DH], pcols], axis=1).astype(BF16)
        vst_ref[g] = jnp.concatenate([vs_t[g * DH:(g + 1) * DH, :].astype(BF16), ones_rows], axis=0)
        vwt_ref[g] = jnp.concatenate([vw_t[g * DH:(g + 1) * DH, :].astype(BF16), ones_rows], axis=0)


def _nsa_norm(main3, bdq, bdk, wq, wk, srow):
    B, S, _ = main3.shape
    QB, DH, G = Q_BLOCK, NSA_HEAD_DIM, NSA_N_KV
    nb = S // QB
    QW, KW, W = NSA_N_HEADS * DH, G * DH, NSA_HPG * QB
    const = lambda shape: pl.BlockSpec(shape, lambda b, i: (0,) * len(shape))
    col = lambda w, c0: pl.BlockSpec((None, QB, w), lambda b, i: (b, i, c0 // w))
    return pl.pallas_call(
        _nsa_norm_kernel,
        grid=(B, nb),
        in_specs=[col(QW, C_Q), col(KW, C_KS), col(KW, C_KW), col(KW, C_VS), col(KW, C_VW),
                  const((QW, QW)), const((KW, KW)), const((1, QW)), const((1, KW)), const(srow.shape)],
        out_specs=[pl.BlockSpec((None, G, None, K_AUG, W), lambda b, i: (b, 0, i, 0, 0)),
                   pl.BlockSpec((None, G, QB, K_AUG), lambda b, i: (b, 0, i, 0)),
                   pl.BlockSpec((None, G, QB, K_AUG), lambda b, i: (b, 0, i, 0)),
                   pl.BlockSpec((None, G, None, V_AUG, QB), lambda b, i: (b, 0, i, 0, 0)),
                   pl.BlockSpec((None, G, None, V_AUG, QB), lambda b, i: (b, 0, i, 0, 0))],
        out_shape=[jax.ShapeDtypeStruct((B, G, nb, K_AUG, W), BF16),
                   jax.ShapeDtypeStruct((B, G, S, K_AUG), BF16), jax.ShapeDtypeStruct((B, G, S, K_AUG), BF16),
                   jax.ShapeDtypeStruct((B, G, nb, V_AUG, QB), BF16),
                   jax.ShapeDtypeStruct((B, G, nb, V_AUG, QB), BF16)],
        compiler_params=_cparams(("arbitrary", "arbitrary")),
        name="nsa_norm_layout",
    )(main3, main3, main3, main3, main3, bdq, bdk, wq, wk, srow)


def _compress_kernel(u_ref, pos_ref, w1_ref, w2_ref, nw_ref, o_ref, *, normalize):
    half = u_ref.shape[1]
    nrow = u_ref.shape[0]
    u = u_ref[...]
    a = _dot(u, w1_ref[0:half, :])
    b = _dot(u, w1_ref[half:2 * half, :])
    posc = _dot(pos_ref[...], w1_ref[...])[0:1, :]
    pre = a + pltpu.roll(b, nrow - 1, 0) + posc
    act = 0.5 * pre * (1.0 + jnp.tanh(math.sqrt(2.0 / math.pi) * (pre + 0.044715 * (pre * pre * pre))))
    o = _dot(act.astype(BF16), w2_ref[...])
    if normalize:
        ms = jnp.mean(o * o, axis=-1, keepdims=True)
        o = o * lax.rsqrt(ms + EPS) * nw_ref[...]
    o_ref[...] = o


def _compress(u, pos8, w1, w2, nw, normalize):
    BG, nrow, half = u.shape
    const = lambda shape: pl.BlockSpec(shape, lambda i: (0, 0))
    return pl.pallas_call(
        functools.partial(_compress_kernel, normalize=normalize),
        grid=(BG,),
        in_specs=[pl.BlockSpec((None, nrow, half), lambda i: (i, 0, 0)),
                  const(pos8.shape), const(w1.shape), const(w2.shape), const(nw.shape)],
        out_specs=pl.BlockSpec((None, nrow, NSA_HEAD_DIM), lambda i: (i, 0, 0)),
        out_shape=jax.ShapeDtypeStruct((BG, nrow, NSA_HEAD_DIM), F32),
        compiler_params=_cparams(("arbitrary",)),
        name="nsa_compress_norm" if normalize else "nsa_compress",
    )(u, pos8, w1, w2, nw)


def _nsa_cmp_kernel(qt_ref, kc_ref, vct_ref, ovt_ref, pm_ref, glt_ref, oct_ref, sb_ref, lst_ref, imp_scr):
    NT = qt_ref.shape[0]
    tiles = range(NT)
    idx = [pl.program_id(2) * NT + t for t in tiles]
    ncmp = kc_ref.shape[0]
    nsel = ovt_ref.shape[0]
    QB = Q_BLOCK
    W = NSA_HPG * QB
    CH = min(LANES, ncmp)
    span = CH * NSA_CMP_STRIDE
    nch = jnp.minimum((idx[-1] * QB + QB - NSA_CMP_LEN) // span + 1, ncmp // CH)

    def attend(rows):
        jrow = lax.broadcasted_iota(I32, (rows, W), 0)
        lane = lax.broadcasted_iota(I32, (rows, W), 1)
        for t in tiles:
            s = _dot(kc_ref[0:rows, :], qt_ref[t])
            mask = (NSA_CMP_STRIDE * jrow + (NSA_CMP_LEN - 1)) <= idx[t] * QB + (lane & (QB - 1))
            sm = jnp.where(mask, s, NEG)
            m = jnp.max(sm, axis=0, keepdims=True)
            p = jnp.where(mask, jnp.exp2(sm - m), 0.0)
            l = jnp.sum(p, axis=0, keepdims=True)
            pn = p * (1.0 / jnp.maximum(l, 1e-30))
            oct_ref[t] = _dot(vct_ref[:, 0:rows], pn.astype(BF16)) * jax.nn.sigmoid(glt_ref[t, 0:1, :])
            psum = pn[:, 0:QB]
            for h in range(1, NSA_HPG):
                psum = psum + pn[:, h * QB:(h + 1) * QB]
            imp_scr[t] = _dot(ovt_ref[:, 0:rows], psum.astype(BF16))

    for k in range(1, ncmp // CH + 1):
        pl.when(nch == k)(functools.partial(attend, k * CH))

    jf = lax.broadcasted_iota(I32, (nsel, QB), 0).astype(F32)
    lane_q = lax.broadcasted_iota(I32, (nsel, QB), 1)
    valid, v0 = [], []
    for t in tiles:
        cur = ((idx[t] * QB + lane_q) >> (NSA_SEL_LEN.bit_length() - 1)).astype(F32)
        forced = (jf == 0.0) | (jf == cur) | (jf == cur - 1.0)
        valid.append(jf <= cur)
        v0.append(jnp.where(forced, FORCED_SCORE, jnp.where(valid[t], imp_scr[t], -1.0)))

    def pick_one(_, carry):
        out = []
        for v, sel in carry:
            mx = jnp.max(v, axis=0, keepdims=True)
            first = jnp.min(jnp.where(v == mx, jf, float(nsel)), axis=0, keepdims=True)
            pick = jf == first
            out.append((jnp.where(pick, -2.0, v), jnp.where(pick, 1.0, sel)))
        return tuple(out)

    picked = lax.fori_loop(0, min(NSA_SEL_TOP, nsel), pick_one,
                           tuple((v0[t], jnp.zeros((nsel, QB), F32)) for t in tiles))

    npair = nsel // 2
    jp = lax.broadcasted_iota(I32, (npair, LANES), 0)
    r2 = lax.broadcasted_iota(I32, (npair, npair), 0)
    c2 = lax.broadcasted_iota(I32, (npair, npair), 1)
    tri = jnp.where(c2 <= r2, 1.0, 0.0).astype(BF16)
    slot = lax.broadcasted_iota(I32, (npair, LANES), 1).astype(F32)
    jpv = lax.broadcasted_iota(I32, (8, npair), 1).astype(F32).astype(BF16)
    r8 = lax.broadcasted_iota(I32, (8, LANES), 0)
    for t in tiles:
        sel = jnp.where(valid[t], picked[t][1], 0.0)
        sb_ref[t] = jnp.where(sel > 0.0, 0.0, NEG)
        pairsel = _dot(pm_ref[...], sel.astype(BF16))
        need = (jnp.sum(pairsel, axis=1, keepdims=True) > 0.0) & (jp < idx[t])
        needf = jnp.where(need, 1.0, 0.0)
        prefix = _dot(tri, needf.astype(BF16))
        onehot = jnp.where(need & (prefix == slot + 1.0), 1.0, 0.0).astype(BF16)
        lst = _dot(jpv, onehot)
        cnt = _dot(jnp.ones((8, npair), BF16), needf.astype(BF16))
        lst_ref[t] = jnp.where(r8 == 0, lst, cnt).astype(I32)


def _nsa_cmp(qt, kc, vct, ovt, pm, glt):
    B, G, nb, _, W = qt.shape
    ncmp = kc.shape[2]
    nsel = ovt.shape[0]
    NT = NSA_CMP_TILES_PER_STEP
    blk = lambda *shape: pl.BlockSpec((None, None, NT) + shape, lambda b, g, i: (b, g, i) + (0,) * len(shape))
    per_bg = lambda *shape: pl.BlockSpec((None, None) + shape, lambda b, g, i: (b, g) + (0,) * len(shape))
    const = lambda shape: pl.BlockSpec(shape, lambda b, g, i: (0,) * len(shape))
    return pl.pallas_call(
        _nsa_cmp_kernel,
        grid=(B, G, nb // NT),
        in_specs=[blk(K_AUG, W), per_bg(ncmp, K_AUG), per_bg(NSA_HEAD_DIM, ncmp),
                  const(ovt.shape), const(pm.shape), blk(8, W)],
        out_specs=[blk(NSA_HEAD_DIM, W), blk(nsel, Q_BLOCK), blk(8, LANES)],
        out_shape=[jax.ShapeDtypeStruct((B, G, nb, NSA_HEAD_DIM, W), F32),
                   jax.ShapeDtypeStruct((B, G, nb, nsel, Q_BLOCK), F32),
                   jax.ShapeDtypeStruct((B, G, nb, 8, LANES), I32)],
        scratch_shapes=[pltpu.VMEM((NT, nsel, Q_BLOCK), F32)],
        compiler_params=_cparams(("arbitrary", "arbitrary", "arbitrary")),
        name="nsa_cmp_select",
    )(qt, kc, vct, ovt, pm, glt)


def _tile4(r):
    return jnp.concatenate([r] * NSA_HPG, axis=1)


def _nsa_sel_kernel(lst_ref, qt_ref, ks_ref, vst_ref, kw_ref, vwt_ref, sb_ref, glt_ref, oct_ref, o_ref,
                    s_scr, d_scr, ow_scr):
    NT = qt_ref.shape[0]
    QB = Q_BLOCK
    W = NSA_HPG * QB
    HB = NSA_SEL_LEN
    DH = NSA_HEAD_DIM
    NPI = NSA_PAIRS_PER_ITER
    tiles = range(NT)
    qts = [qt_ref[t] for t in tiles]
    counts = [lst_ref[t, 1, 0] for t in tiles]
    idx = [pl.program_id(2) * NT + t for t in tiles]

    def flash(s, vt, carry):
        m, acc = carry
        mn = jnp.maximum(m, jnp.max(s, axis=0, keepdims=True))
        p = jnp.exp2(s - mn)
        return mn, jnp.exp2(m - mn) * acc + _dot(vt, p.astype(BF16))

    def pair_scores(t, k):
        jp = lst_ref[t, 0, k]
        live = k < counts[t]
        s = _dot(ks_ref[pl.ds(pl.multiple_of(jp * QB, QB), QB), :], qts[t])
        b0 = jnp.where(live, _tile4(sb_ref[t, pl.ds(2 * jp, 1), :]), NEG)
        b1 = jnp.where(live, _tile4(sb_ref[t, pl.ds(2 * jp + 1, 1), :]), NEG)
        return jnp.concatenate([s[0:HB] + b0, s[HB:2 * HB] + b1], axis=0)

    def scores(t, it):
        return jnp.concatenate([pair_scores(t, NPI * it + u) for u in range(NPI)], axis=0)

    r = lax.broadcasted_iota(I32, (QB, W), 0)
    c = lax.broadcasted_iota(I32, (QB, W), 1) & (QB - 1)
    for t in tiles:
        i = idx[t]
        s_scr[t, 0] = scores(t, 0)
        kd = ks_ref[pl.ds(pl.multiple_of(i * QB, QB), QB), :]
        d_scr[t] = jnp.where(r <= c, _dot(kd, qts[t]), NEG)

        ss, vts = [], []
        for w in range(NSA_WINDOW // QB + 1):
            pw = i - NSA_WINDOW // QB + w
            pc = jnp.maximum(pw, 0)
            kk = kw_ref[pl.ds(pl.multiple_of(pc * QB, QB), QB), :]
            s = _dot(kk, qts[t])
            if w == 0:
                s = jnp.where((r > c) & (pw >= 0), s, NEG)
            elif w == NSA_WINDOW // QB:
                s = jnp.where(r <= c, s, NEG)
            else:
                s = jnp.where(pw >= 0, s, NEG)
            ss.append(s)
            vts.append(vwt_ref[pc])
        s = jnp.concatenate(ss, axis=0)
        p = jnp.exp2(s - jnp.max(s, axis=0, keepdims=True))
        aw = _dot(jnp.concatenate(vts, axis=1), p.astype(BF16))
        o_win = aw[0:DH] * (1.0 / aw[DH:DH + 1])
        ow_scr[t] = oct_ref[t] + jax.nn.sigmoid(glt_ref[t, 2:3, :]) * o_win

    def body(it, carry):
        slot = it & 1
        cur = [s_scr[t, slot] for t in tiles]
        vts = [jnp.concatenate([vst_ref[lst_ref[t, 0, NPI * it + u]] for u in range(NPI)], axis=1) for t in tiles]
        nxt = [scores(t, it + 1) for t in tiles]
        out = tuple(flash(cur[t], vts[t], carry[t]) for t in tiles)
        for t in tiles:
            s_scr[t, 1 - slot] = nxt[t]
        return out

    n_iter = (functools.reduce(jnp.maximum, counts) + NPI - 1) // NPI
    init = tuple((jnp.full((1, W), NEG, F32), jnp.zeros((vst_ref.shape[1], W), F32)) for _ in tiles)
    carry = lax.fori_loop(0, n_iter, body, init)
    for t in tiles:
        m, acc = flash(d_scr[t], vst_ref[idx[t]], carry[t])
        o_t = ow_scr[t] + jax.nn.sigmoid(glt_ref[t, 1:2, :]) * (acc[0:DH] * (1.0 / acc[DH:DH + 1]))
        o_ref[t * QB:(t + 1) * QB, :] = jnp.concatenate(
            [o_t[:, h * QB:(h + 1) * QB].T for h in range(NSA_HPG)], axis=1).astype(BF16)


def _nsa_sel(lst, qt, ks, vst, kw, vwt, sb, glt, oct):
    B, G, nb, _, W = qt.shape
    S = ks.shape[2]
    npair = vst.shape[2]
    nsel = sb.shape[3]
    NT = NSA_TILES_PER_STEP
    blk = lambda *shape: pl.BlockSpec((None, None, NT) + shape, lambda b, g, i: (b, g, i) + (0,) * len(shape))
    per_bg = lambda *shape: pl.BlockSpec((None, None) + shape, lambda b, g, i: (b, g) + (0,) * len(shape))
    return pl.pallas_call(
        _nsa_sel_kernel,
        grid=(B, G, nb // NT),
        in_specs=[pl.BlockSpec((None, None, NT, 8, LANES), lambda b, g, i: (b, g, i, 0, 0),
                               memory_space=pltpu.SMEM),
                  blk(K_AUG, W), per_bg(S, K_AUG), per_bg(npair, V_AUG, Q_BLOCK),
                  per_bg(S, K_AUG), per_bg(npair, V_AUG, Q_BLOCK),
                  blk(nsel, Q_BLOCK), blk(8, W), blk(NSA_HEAD_DIM, W)],
        out_specs=pl.BlockSpec((None, NT * Q_BLOCK, NSA_HPG * NSA_HEAD_DIM), lambda b, g, i: (b, i, g)),
        out_shape=jax.ShapeDtypeStruct((B, S, NSA_N_HEADS * NSA_HEAD_DIM), BF16),
        scratch_shapes=[pltpu.VMEM((NT, 2, NSA_PAIRS_PER_ITER * Q_BLOCK, W), F32),
                        pltpu.VMEM((NT, Q_BLOCK, W), F32), pltpu.VMEM((NT, NSA_HEAD_DIM, W), F32)],
        compiler_params=_cparams(("arbitrary", "arbitrary", "arbitrary")),
        name="nsa_select_window",
    )(lst, qt, ks, vst, kw, vwt, sb, glt, oct)


ROW_TILE = 8


def _store_row_tiles(ref, x):
    n = x.shape[0]
    for c in range(ROW_TILE):
        ref[pl.ds(c, n, stride=ROW_TILE), :] = x[:, c * LANES:(c + 1) * LANES]


def _load_row_tiles(ref, start, n):
    return jnp.concatenate([ref[pl.ds(start + c, n, stride=ROW_TILE), :] for c in range(ROW_TILE)], axis=1)


def _merge_kernel(x_ref, ys_ref, on_ref, gs_ref, gn_ref, wssd_ref, wnsa_ref, wout_ref, ln2_ref,
                  rwh_ref, rwl_ref, rb_ref, x1_ref, xn2_ref, route_ref, cnt_ref, base_ref):
    tm = x_ref.shape[0]

    @pl.when(pl.program_id(0) == 0)
    def _():
        base_ref[...] = jnp.zeros(base_ref.shape, F32)

    y_ssd = _dot(ys_ref[...], wssd_ref[...])
    y_nsa = _dot(on_ref[...], wnsa_ref[...])
    merged = (jax.nn.sigmoid(gs_ref[...].astype(F32)) * y_ssd
              + jax.nn.sigmoid(gn_ref[...].astype(F32)) * y_nsa)
    x1 = x_ref[...] + _dot(merged.astype(BF16), wout_ref[...])
    x1_ref[...] = x1
    ms = jnp.mean(x1 * x1, axis=-1, keepdims=True)
    xn2 = x1 * lax.rsqrt(ms + EPS) * ln2_ref[...]
    _store_row_tiles(xn2_ref, xn2)

    xh, xl = _split_bf16(xn2)
    logits = _dot(xh, rwh_ref[...]) + _dot(xl, rwh_ref[...]) + _dot(xh, rwl_ref[...]) + rb_ref[...]
    lane = lax.broadcasted_iota(I32, (tm, LANES), 1)
    lanef = lane.astype(F32)
    v = logits
    onehot = jnp.zeros((tm, LANES), F32)
    vals, picks = [], []
    for _ in range(TOP_K):
        mx = jnp.max(v, axis=-1, keepdims=True)
        first = jnp.min(jnp.where(v == mx, lanef, float(LANES)), axis=-1, keepdims=True)
        pick = lanef == first
        v = jnp.where(pick, 2.0 * NEG, v)
        onehot = jnp.where(pick, 1.0, onehot)
        vals.append(mx)
        picks.append((pick, first))
    es = [jnp.exp(val - vals[0]) for val in vals]
    inv = 1.0 / (es[0] + es[1] + es[2] + es[3])

    r2 = lax.broadcasted_iota(I32, (tm, tm), 0)
    c2 = lax.broadcasted_iota(I32, (tm, tm), 1)
    stril = jnp.where(c2 < r2, 1.0, 0.0).astype(BF16)
    posmap = base_ref[...] + _dot(stril, onehot.astype(BF16))
    route = jnp.zeros((tm, LANES), F32)
    for k in range(TOP_K):
        pick, first = picks[k]
        pos = jnp.sum(jnp.where(pick, posmap, 0.0), axis=-1, keepdims=True)
        route = jnp.where(lane == k, first, route)
        route = jnp.where(lane == TOP_K + k, pos, route)
        route = jnp.where(lane == 2 * TOP_K + k, es[k] * inv, route)
    route_ref[...] = route
    base = base_ref[...] + jnp.sum(onehot, axis=0, keepdims=True)
    base_ref[...] = base
    cnt_ref[...] = jnp.broadcast_to(base, cnt_ref.shape)


def _merge(x2, yssd, onsa, main, wssd, wnsa, wout, ln2, rwh, rwl, rb, tm):
    T = x2.shape[0]
    const = lambda shape: pl.BlockSpec(shape, lambda i: (0, 0))
    row = lambda w, j=0: pl.BlockSpec((tm, w), lambda i: (i, j))
    return pl.pallas_call(
        _merge_kernel,
        grid=(T // tm,),
        in_specs=[row(D_MODEL), row(SSD_D_INNER), row(D_MODEL), row(D_MODEL, C_MS // D_MODEL),
                  row(D_MODEL, C_MN // D_MODEL),
                  const(wssd.shape), const(wnsa.shape), const(wout.shape), const(ln2.shape),
                  const(rwh.shape), const(rwl.shape), const(rb.shape)],
        out_specs=[row(D_MODEL), pl.BlockSpec((tm * ROW_TILE, LANES), lambda i: (i, 0)), row(LANES),
                   const((8, LANES))],
        out_shape=[jax.ShapeDtypeStruct((T, D_MODEL), F32), jax.ShapeDtypeStruct((T * ROW_TILE, LANES), F32),
                   jax.ShapeDtypeStruct((T, LANES), F32), jax.ShapeDtypeStruct((8, LANES), F32)],
        scratch_shapes=[pltpu.VMEM((1, LANES), F32)],
        compiler_params=_cparams(("arbitrary",)),
        name="merge_router",
    )(x2, yssd, onsa, main, main, wssd, wnsa, wout, ln2, rwh, rwl, rb)


def _dispatch_kernel(dest_ref, xn_ref, xs_init, xs_hbm, sem):
    del xs_init
    RT = ROW_TILE
    tc = xn_ref.shape[0] // RT

    def issue(r2, c):
        for pr in (0, 1):
            r = 2 * r2 + pr
            src = pl.multiple_of((r & (tc - 1)) * RT, RT)
            dst = pl.multiple_of(dest_ref[0, r] * RT, RT)
            pltpu.make_async_copy(xn_ref.at[pl.ds(src, RT)], xs_hbm.at[pl.ds(dst, RT)], sem).start(priority=pr)
        return c

    lax.fori_loop(0, TOP_K * tc // 2, issue, 0, unroll=4)
    for _ in range(TOP_K):
        pltpu.make_async_copy(xn_ref, xs_hbm.at[pl.ds(0, tc * RT)], sem).wait()


def _dispatch(dest3, xn2, n_rows, tc):
    T = xn2.shape[0] // ROW_TILE
    xs0 = jnp.zeros((n_rows * ROW_TILE, LANES), F32)
    return pl.pallas_call(
        _dispatch_kernel,
        grid=(T // tc,),
        in_specs=[pl.BlockSpec((None, 1, TOP_K * tc), lambda i: (i, 0, 0), memory_space=pltpu.SMEM),
                  pl.BlockSpec((tc * ROW_TILE, LANES), lambda i: (i, 0)),
                  pl.BlockSpec(memory_space=pl.ANY)],
        out_specs=pl.BlockSpec(memory_space=pl.ANY),
        out_shape=jax.ShapeDtypeStruct(xs0.shape, F32),
        scratch_shapes=[pltpu.SemaphoreType.DMA(())],
        input_output_aliases={2: 0},
        compiler_params=_cparams(("arbitrary",)),
        name="moe_dispatch",
    )(dest3, xn2, xs0)


def _expert_kernel(be_ref, nu_ref, xs_ref, w1g_ref, w1l_ref, b1g_ref, b1l_ref, w2_ref, b2_ref, ys_ref):
    bm = xs_ref.shape[0] // ROW_TILE
    blk = pl.program_id(0)

    @pl.when(blk < nu_ref[0])
    def _():
        x = _load_row_tiles(xs_ref, 0, bm).astype(BF16)
        glu = jnp.minimum(_dot(x, w1g_ref[...]) + b1g_ref[...], SWIGLU_LIMIT)
        lin = jnp.clip(_dot(x, w1l_ref[...]) + b1l_ref[...], -SWIGLU_LIMIT, SWIGLU_LIMIT)
        act = glu * jax.nn.sigmoid(SWIGLU_ALPHA * glu) * (lin + 1.0)
        _store_row_tiles(ys_ref, _dot(act.astype(BF16), w2_ref[...]) + b2_ref[...])

    @pl.when(blk >= nu_ref[0])
    def _():
        ys_ref[...] = jnp.zeros(ys_ref.shape, F32)


def _experts(block_e, n_used, xs, w1p, b1g, b1l, w2, b2, bm):
    n_blocks = xs.shape[0] // (bm * ROW_TILE)
    DE = w2.shape[1]
    wspec = lambda r, c, j=0: pl.BlockSpec((None, r, c), lambda b, be, nu: (be[b], 0, j))
    rows = pl.BlockSpec((bm * ROW_TILE, LANES), lambda b, be, nu: (b, 0))
    return pl.pallas_call(
        _expert_kernel,
        grid_spec=pltpu.PrefetchScalarGridSpec(
            num_scalar_prefetch=2,
            grid=(n_blocks,),
            in_specs=[rows, wspec(D_MODEL, DE, 0), wspec(D_MODEL, DE, 1), wspec(1, DE), wspec(1, DE),
                      wspec(DE, D_MODEL), wspec(1, D_MODEL)],
            out_specs=rows),
        out_shape=jax.ShapeDtypeStruct(xs.shape, F32),
        compiler_params=_cparams(("arbitrary",)),
        name="moe_experts",
    )(block_e, n_used, xs, w1p, w1p, b1g, b1l, w2, b2)


def _deinterleave_kernel(w_ref, p_ref, o_ref):
    CW = p_ref.shape[0]
    half = w_ref.shape[1] // 2
    for c in range(w_ref.shape[1] // CW):
        y = _dot(w_ref[:, c * CW:(c + 1) * CW].astype(BF16), p_ref[...]).astype(BF16)
        o_ref[:, c * LANES:(c + 1) * LANES] = y[:, 0:LANES]
        o_ref[:, half + c * LANES:half + (c + 1) * LANES] = y[:, LANES:CW]


def _deinterleave(w1):
    E, D, W = w1.shape
    CW = 2 * LANES
    src = np.concatenate([np.arange(0, CW, 2), np.arange(1, CW, 2)])
    perm = jnp.asarray(np.arange(CW)[:, None] == src[None, :], BF16)
    rows = D // 2
    return pl.pallas_call(
        _deinterleave_kernel,
        grid=(E, D // rows),
        in_specs=[pl.BlockSpec((None, rows, W), lambda e, r: (e, r, 0)),
                  pl.BlockSpec((CW, CW), lambda e, r: (0, 0))],
        out_specs=pl.BlockSpec((None, rows, W), lambda e, r: (e, r, 0)),
        out_shape=jax.ShapeDtypeStruct((E, D, W), BF16),
        compiler_params=_cparams(("arbitrary", "arbitrary")),
        name="moe_w1_deinterleave",
    )(w1, perm)


def _combine_kernel(dest_ref, destn_ref, route_ref, x1_ref, ys_hbm, o_ref, buf, sem):
    tc = x1_ref.shape[0]
    i = pl.program_id(0)
    slot = i & 1

    RT = ROW_TILE

    def gather(dest_smem, s):
        def issue(r2, c):
            for pr in (0, 1):
                r = 2 * r2 + pr
                src = pl.multiple_of(dest_smem[0, r] * RT, RT)
                dst = pl.multiple_of(r * RT, RT)
                pltpu.make_async_copy(ys_hbm.at[pl.ds(src, RT)], buf.at[s, pl.ds(dst, RT)],
                                      sem.at[s]).start(priority=pr)
            return c

        lax.fori_loop(0, TOP_K * tc // 2, issue, 0, unroll=4)

    @pl.when(i == 0)
    def _():
        gather(dest_ref, 0)

    for s in (0, 1):
        @pl.when((i + 1 < pl.num_programs(0)) & (slot == 1 - s))
        def _():
            gather(destn_ref, s)

    pltpu.make_async_copy(ys_hbm.at[pl.ds(0, TOP_K * tc * RT)], buf.at[slot], sem.at[slot]).wait()
    route = route_ref[...]
    acc = x1_ref[...]
    for k in range(TOP_K):
        acc = acc + route[:, 2 * TOP_K + k:2 * TOP_K + k + 1] * _load_row_tiles(buf.at[slot], k * tc * RT, tc)
    o_ref[...] = acc


def _combine(dest3, route, x1, ys, tc):
    T = x1.shape[0]
    n = T // tc
    return pl.pallas_call(
        _combine_kernel,
        grid=(n,),
        in_specs=[pl.BlockSpec((None, 1, TOP_K * tc), lambda i: (i, 0, 0), memory_space=pltpu.SMEM),
                  pl.BlockSpec((None, 1, TOP_K * tc), lambda i: (jnp.minimum(i + 1, n - 1), 0, 0),
                               memory_space=pltpu.SMEM),
                  pl.BlockSpec((tc, LANES), lambda i: (i, 0)),
                  pl.BlockSpec((tc, D_MODEL), lambda i: (i, 0)),
                  pl.BlockSpec(memory_space=pl.ANY)],
        out_specs=pl.BlockSpec((tc, D_MODEL), lambda i: (i, 0)),
        out_shape=jax.ShapeDtypeStruct((T, D_MODEL), F32),
        scratch_shapes=[pltpu.VMEM((2, TOP_K * tc * ROW_TILE, LANES), F32), pltpu.SemaphoreType.DMA((2,))],
        compiler_params=_cparams(("arbitrary",)),
        name="moe_combine",
    )(dest3, dest3, route, x1, ys)


def _split_pos(pos):
    lo = pos % 256
    return lo.astype(np.float32), (pos - lo).astype(np.float32)


def _aug_keys(k, pos):
    lo, hi = _split_pos(pos)
    cols = np.stack([lo, hi, lo, hi] + [np.zeros_like(lo)] * (K_AUG - NSA_HEAD_DIM - 4), axis=-1)
    cols = jnp.broadcast_to(jnp.asarray(cols, BF16), k.shape[:-1] + (cols.shape[-1],))
    return jnp.concatenate([k, cols], axis=-1)


def _slope_rows():
    slopes = (2.0 ** (-8.0 * np.arange(1, NSA_N_HEADS + 1) / NSA_N_HEADS)).astype(np.float32)
    sl = jnp.asarray(np.repeat((slopes * np.float32(LOG2E)).reshape(NSA_N_KV, NSA_HPG), Q_BLOCK, axis=1), F32)
    hi, lo = _split_bf16(sl)
    zero = jnp.zeros_like(hi)
    return jnp.stack([hi, hi, lo, lo] + [zero] * (K_AUG - NSA_HEAD_DIM - 4), axis=1)


def _nsa(main3, tail3, q_norm_w, k_norm_w, cmp_pos_k, cmp_pos_v, cmp_k_w1, cmp_k_w2, cmp_v_w1, cmp_v_w2):
    B, S, _ = main3.shape
    T = B * S
    G, HPG, DH, QB = NSA_N_KV, NSA_HPG, NSA_HEAD_DIM, Q_BLOCK
    nb, npair, nsel, ncmp = S // QB, S // QB, S // NSA_SEL_LEN, S // NSA_CMP_STRIDE
    main = main3.reshape(T, MAIN_W)

    hd = np.arange(NSA_N_HEADS * DH) // DH
    bdq = jnp.asarray(hd[:, None] == hd[None, :], BF16)
    bdk = bdq[:G * DH, :G * DH]
    wq = (jnp.tile(q_norm_w.astype(F32), NSA_N_HEADS) * (DH ** -0.5 * LOG2E)).reshape(1, -1)
    wk = jnp.tile(k_norm_w.astype(F32), G).reshape(1, -1)
    qt, ks, kw, vst, vwt = _nsa_norm(main3, bdq, bdk, wq, wk, _slope_rows())

    def kv_heads(t):
        return t.reshape(B, S, G, DH).transpose(0, 2, 1, 3)

    def halves(c0):
        return kv_heads(main[:, c0:c0 + G * DH]).reshape(B * G, ncmp, NSA_CMP_STRIDE * DH)

    def pos8(pos):
        return jnp.zeros((8, NSA_CMP_LEN * DH), F32).at[0].set(pos.reshape(-1)).astype(BF16)

    ones = jnp.ones((1, DH), F32)
    kc = _compress(halves(C_KC), pos8(cmp_pos_k), cmp_k_w1.astype(BF16), cmp_k_w2.astype(BF16),
                   k_norm_w.astype(F32).reshape(1, DH), True)
    vc = _compress(halves(C_VC), pos8(cmp_pos_v), cmp_v_w1.astype(BF16), cmp_v_w2.astype(BF16), ones, False)
    cmp_end = np.arange(ncmp) * NSA_CMP_STRIDE + NSA_CMP_LEN - 1
    kc = _aug_keys(kc.astype(BF16).reshape(B, G, ncmp, DH), cmp_end)
    vct = vc.astype(BF16).reshape(B, G, ncmp, DH).transpose(0, 1, 3, 2)

    c_start = (np.arange(ncmp) * NSA_CMP_STRIDE)[:, None]
    s_start = (np.arange(nsel) * NSA_SEL_LEN)[None, :]
    overlap = np.clip(np.minimum(c_start + NSA_CMP_LEN, s_start + NSA_SEL_LEN)
                      - np.maximum(c_start, s_start), 0, None) / NSA_CMP_LEN
    overlap[(S - NSA_CMP_LEN) // NSA_CMP_STRIDE + 1:] = 0.0
    ovt = jnp.asarray(overlap.T, BF16)
    pm = jnp.asarray(np.arange(nsel)[None, :] // 2 == np.arange(nsel // 2)[:, None], BF16)

    gl = tail3.reshape(B, nb, QB, G, LANES)[..., 8:8 + 3 * HPG].reshape(B, nb, QB, G, HPG, 3)
    glt = gl.transpose(0, 3, 1, 5, 4, 2).reshape(B, G, nb, 3, HPG * QB)
    glt = jnp.concatenate([glt, jnp.zeros((B, G, nb, 5, HPG * QB), F32)], axis=3)

    oct, sb, lst = _nsa_cmp(qt, kc, vct, ovt, pm, glt)
    return _nsa_sel(lst, qt, ks, vst, kw, vwt, sb, glt, oct).reshape(T, NSA_N_HEADS * DH)


def _moe_tables(route, cnt, T, bm):
    idx = route[:, 0:TOP_K].astype(I32)
    pos = route[:, TOP_K:2 * TOP_K].astype(I32)
    counts = cnt[0, :N_EXPERTS].astype(I32)
    padded = (counts + bm - 1) // bm * bm
    pend = jnp.cumsum(padded)
    pstart = pend - padded
    dest = pstart[idx] + pos
    n_blocks = -(-(T * TOP_K + N_EXPERTS * (bm - 1)) // bm)
    b_start = jnp.arange(n_blocks, dtype=I32) * bm
    block_e = jnp.minimum(jnp.sum(b_start[:, None] >= pend[None, :], axis=1), N_EXPERTS - 1).astype(I32)
    n_used = (pend[-1] // bm).astype(I32).reshape(1)
    return dest, n_blocks * bm, block_e, n_used


def kernel(x, ln1_w, w_in, ssd_conv_w, ssd_conv_b, ssd_dt_bias, ssd_a_log, ssd_d, ssd_norm_w, ssd_out_w,
           nsa_q_norm_w, nsa_k_norm_w, cmp_pos_k, cmp_pos_v, cmp_k_w1, cmp_k_w2, cmp_v_w1, cmp_v_w2,
           nsa_out_w, w_out, ln2_w, router_w, router_b, exp_w1, exp_b1, exp_w2, exp_b2):
    B, S, D = x.shape
    T = B * S
    depth = ln1_w.shape[0]
    x2 = x.reshape(T, D)
    G = SSD_N_GROUPS

    main_parts = ((0, O_DT), (O_MERGE, O_MERGE + 2 * D_MODEL), (O_Q, O_GATE))

    def group_lanes(v):
        out = jnp.zeros((G, LANES), F32).at[:, :SSD_HPG].set(v.astype(F32).reshape(G, SSD_HPG))
        return out.reshape(1, TAIL_W)

    tm_rows = min(1024, T)
    bm = 512
    for l in range(depth):
        w_main = jnp.concatenate([w_in[l][:, a:b].astype(BF16) for a, b in main_parts], axis=1)
        zpad = jnp.zeros((D, LANES - SSD_HPG - 3 * NSA_HPG), BF16)
        w_tail = jnp.concatenate(
            [part for g in range(G) for part in (
                w_in[l][:, O_DT + SSD_HPG * g:O_DT + SSD_HPG * (g + 1)].astype(BF16),
                w_in[l][:, O_GATE + 3 * NSA_HPG * g:O_GATE + 3 * NSA_HPG * (g + 1)].astype(BF16), zpad)], axis=1)
        main, tail = _inproj(x2, ln1_w[l].reshape(1, D), w_main, w_tail, tm=tm_rows, tn=MAIN_W // 4)
        main3 = main.reshape(B, S, MAIN_W)
        tail3 = tail.reshape(B, S, TAIL_W)

        yssd = _ssd(main3, tail3, ssd_conv_w[l], ssd_conv_b[l].reshape(1, -1),
                    group_lanes(ssd_dt_bias[l]), group_lanes(ssd_a_log[l]),
                    jnp.repeat(ssd_d[l].astype(F32), SSD_HEAD_DIM).reshape(1, -1),
                    ssd_norm_w[l].reshape(1, -1))
        onsa = _nsa(main3, tail3, nsa_q_norm_w[l], nsa_k_norm_w[l], cmp_pos_k[l], cmp_pos_v[l],
                    cmp_k_w1[l], cmp_k_w2[l], cmp_v_w1[l], cmp_v_w2[l])

        rw = jnp.zeros((D, LANES), F32).at[:, :N_EXPERTS].set(router_w[l])
        rwh, rwl = _split_bf16(rw)
        rb = jnp.full((1, LANES), NEG, F32).at[0, :N_EXPERTS].set(router_b[l])
        x1, xn2, route, cnt = _merge(x2, yssd.reshape(T, -1), onsa, main,
                                     ssd_out_w[l].astype(BF16), nsa_out_w[l].astype(BF16), w_out[l].astype(BF16),
                                     ln2_w[l].reshape(1, D), rwh, rwl, rb, tm=min(256, T))

        dest, n_rows, block_e, n_used = _moe_tables(route, cnt, T, bm)
        tc = min(256, T)
        dest3 = dest.reshape(T // tc, tc, TOP_K).transpose(0, 2, 1).reshape(T // tc, 1, TOP_K * tc)
        xs = _dispatch(dest3, xn2, n_rows, tc)
        ys = _experts(block_e, n_used, xs, _deinterleave(exp_w1[l]),
                      exp_b1[l][:, None, 0::2], exp_b1[l][:, None, 1::2],
                      exp_w2[l].astype(BF16), exp_b2[l][:, None, :], bm)
        x2 = _combine(dest3, route, x1, ys, tc)
    return x2.reshape(B, S, D)
```

```python
import functools
import math

import numpy as np
import jax
import jax.numpy as jnp
from jax import lax
from jax.experimental import pallas as pl
from jax.experimental.pallas import tpu as pltpu

F32 = jnp.float32
BF16 = jnp.bfloat16
I32 = jnp.int32

D_MODEL = 1024
SSD_D_INNER = 2048
SSD_HEAD_DIM = 64
SSD_N_HEADS = 32
SSD_N_GROUPS = 4
SSD_D_STATE = 128
SSD_CHUNK = 256
SSD_HPG = SSD_N_HEADS // SSD_N_GROUPS
NSA_N_HEADS = 16
NSA_N_KV = 4
NSA_HPG = NSA_N_HEADS // NSA_N_KV
NSA_HEAD_DIM = 64
NSA_CMP_LEN = 32
NSA_CMP_STRIDE = 16
NSA_CMP_HIDDEN = 256
NSA_SEL_LEN = 64
NSA_SEL_TOP = 16
NSA_WINDOW = 512
Q_BLOCK = 128
FORCED_SCORE = 1.0e4
N_EXPERTS = 32
TOP_K = 4
SWIGLU_LIMIT = 7.0
SWIGLU_ALPHA = 1.702
EPS = 1e-6

NEG = -1.0e30
LOG2E = 1.4426950408889634
LANES = 128
K_AUG = 80
V_AUG = 80
NSA_PAIRS_PER_ITER = 2
NSA_TILES_PER_STEP = 2
NSA_CMP_TILES_PER_STEP = 2
VMEM_LIMIT = 56 * 1024 * 1024

C_Z, C_XS, C_B, C_C, C_MS, C_MN, C_Q, C_KC, C_VC, C_KS, C_VS, C_KW, C_VW = (
    0, 2048, 4096, 4608, 5120, 6144, 7168, 8192, 8448, 8704, 8960, 9216, 9472)
MAIN_W = 9728
TAIL_W = 512
O_DT, O_Q, O_GATE, O_MERGE = 5120, 5152, 7712, 7760


def _cparams(sem):
    return pltpu.CompilerParams(dimension_semantics=sem, vmem_limit_bytes=VMEM_LIMIT)


def _dot(a, b):
    return jnp.dot(a, b, preferred_element_type=F32)


def _split_bf16(x):
    hi = x.astype(BF16)
    lo = (x - hi.astype(F32)).astype(BF16)
    return hi, lo


def _inproj_kernel(x_ref, lnw_ref, w_ref, wt_ref, main_ref, tail_ref, xn_ref):
    @pl.when(pl.program_id(1) == 0)
    def _():
        x = x_ref[...]
        ms = jnp.mean(x * x, axis=-1, keepdims=True)
        xn = (x * lax.rsqrt(ms + EPS) * lnw_ref[...]).astype(BF16)
        xn_ref[...] = xn
        tail_ref[...] = _dot(xn, wt_ref[...])

    main_ref[...] = _dot(xn_ref[...], w_ref[...]).astype(BF16)


def _inproj(x2, ln_w, w_main, w_tail, tm, tn):
    T = x2.shape[0]
    return pl.pallas_call(
        _inproj_kernel,
        grid=(T // tm, MAIN_W // tn),
        in_specs=[pl.BlockSpec((tm, D_MODEL), lambda i, j: (i, 0)),
                  pl.BlockSpec((1, D_MODEL), lambda i, j: (0, 0)),
                  pl.BlockSpec((D_MODEL, tn), lambda i, j: (0, j)),
                  pl.BlockSpec((D_MODEL, TAIL_W), lambda i, j: (0, 0))],
        out_specs=[pl.BlockSpec((tm, tn), lambda i, j: (i, j)),
                   pl.BlockSpec((tm, TAIL_W), lambda i, j: (i, 0))],
        out_shape=[jax.ShapeDtypeStruct((T, MAIN_W), BF16),
                   jax.ShapeDtypeStruct((T, TAIL_W), F32)],
        scratch_shapes=[pltpu.VMEM((tm, D_MODEL), BF16)],
        compiler_params=_cparams(("arbitrary", "arbitrary")),
        name="inproj",
    )(x2, ln_w, w_main, w_tail)


def _softplus(x):
    return jnp.maximum(x, 0.0) + jnp.log1p(jnp.exp(-jnp.abs(x)))


def _silu(x):
    return x * (0.5 * jnp.tanh(0.5 * x) + 0.5)


def _ssd_kernel(z_ref, xs_ref, b_ref, c_ref, dt_ref, cwx_ref, cwb_ref, cwc_ref, cbx_ref, cbb_ref, cbc_ref,
                dtb_ref, alog_ref, dexp_ref, nw_ref, hexp_ref, y_ref, extx, extb, extc, hs_ref):
    L = xs_ref.shape[0]
    P = SSD_HEAD_DIM

    @pl.when(pl.program_id(2) == 0)
    def _():
        extx[0:8, :] = jnp.zeros((8, extx.shape[1]), F32)
        extb[0:8, :] = jnp.zeros((8, extb.shape[1]), F32)
        extc[0:8, :] = jnp.zeros((8, extc.shape[1]), F32)
        hs_ref[...] = jnp.zeros(hs_ref.shape, F32)

    def conv_act(x_ref, w_ref, bias_ref, ext):
        xf = x_ref[...].astype(F32)
        ext[8:8 + L, :] = xf
        acc = bias_ref[...] + w_ref[3:4, :] * xf
        for k in (1, 2, 3):
            acc = acc + w_ref[3 - k:4 - k, :] * ext[8 - k:8 - k + L, :]
        ext[0:8, :] = ext[L:L + 8, :]
        return _silu(acc)

    xs = conv_act(xs_ref, cwx_ref, cbx_ref, extx)
    bm = conv_act(b_ref, cwb_ref, cbb_ref, extb)
    cm = conv_act(c_ref, cwc_ref, cbc_ref, extc)

    dt = _softplus(dt_ref[...] + dtb_ref[...])
    da = dt * (-jnp.exp(alog_ref[...]))
    row = lax.broadcasted_iota(I32, (L, L), 0)
    col = lax.broadcasted_iota(I32, (L, L), 1)
    tril = row >= col
    trif = jnp.where(tril, 1.0, 0.0).astype(BF16)
    da_hi, da_lo = _split_bf16(da)
    acum = _dot(trif, da_hi) + _dot(trif, da_lo)
    acum_t = acum.T
    a_end = acum[L - 1:L, :]
    ea = jnp.exp(acum)
    wend = jnp.exp(a_end - acum) * dt
    eend = jnp.exp(a_end)

    cb16 = cm.astype(BF16)
    bt16 = bm.T.astype(BF16)
    cb = _dot(cb16, bt16)

    def head_lanes(v):
        hi, lo = _split_bf16(v)
        return _dot(hi, hexp_ref[...]) + _dot(lo, hexp_ref[...])

    xdt = (xs * head_lanes(dt)).astype(BF16)
    xw = (xs * head_lanes(wend)).astype(BF16)
    y_intra, y_inter = [], []
    for hh in range(SSD_HPG):
        seg = acum[:, hh:hh + 1] - acum_t[hh:hh + 1, :]
        decay = jnp.exp(jnp.where(tril, seg, NEG))
        g = (cb * decay).astype(BF16)
        hprev = hs_ref[hh]
        y_intra.append(_dot(g, xdt[:, hh * P:(hh + 1) * P]))
        y_inter.append(_dot(cb16, hprev.astype(BF16)))
        hs_ref[hh] = hprev * eend[:, hh:hh + 1] + _dot(bt16, xw[:, hh * P:(hh + 1) * P])
    y = jnp.concatenate(y_intra, axis=1) + jnp.concatenate(y_inter, axis=1) * head_lanes(ea)
    y = y + dexp_ref[...] * xs
    y = y * _silu(z_ref[...].astype(F32))
    ms = jnp.mean(y * y, axis=-1, keepdims=True)
    y_ref[...] = (y * lax.rsqrt(ms + EPS) * nw_ref[...]).astype(BF16)


def _ssd(main3, tail3, conv_w, conv_b, dtb, alog, dexp, norm_w):
    B, S, _ = main3.shape
    L = math.gcd(S, SSD_CHUNK)
    nc = S // L
    G, N, GW = SSD_N_GROUPS, SSD_D_STATE, SSD_D_INNER // SSD_N_GROUPS
    xs0, b0, c0 = C_XS // GW, C_B // N, C_C // N
    cb0, cc0 = SSD_D_INNER // N, (SSD_D_INNER + G * N) // N
    hexp = jnp.asarray(np.arange(LANES)[:, None] == np.arange(GW)[None, :] // SSD_HEAD_DIM, BF16)

    def seq(w, off):
        return pl.BlockSpec((None, L, w), lambda b, g, c: (b, c, off + g))

    def par(r, w, off):
        return pl.BlockSpec((r, w), lambda b, g, c: (0, off + g))

    return pl.pallas_call(
        _ssd_kernel,
        grid=(B, G, nc),
        in_specs=[seq(GW, 0), seq(GW, xs0), seq(N, b0), seq(N, c0), seq(LANES, 0),
                  par(4, GW, 0), par(4, N, cb0), par(4, N, cc0),
                  par(1, GW, 0), par(1, N, cb0), par(1, N, cc0),
                  par(1, LANES, 0), par(1, LANES, 0), par(1, GW, 0), par(1, GW, 0),
                  pl.BlockSpec((LANES, GW), lambda b, g, c: (0, 0))],
        out_specs=pl.BlockSpec((None, L, GW), lambda b, g, c: (b, c, g)),
        out_shape=jax.ShapeDtypeStruct((B, S, SSD_D_INNER), BF16),
        scratch_shapes=[pltpu.VMEM((L + 8, GW), F32), pltpu.VMEM((L + 8, N), F32), pltpu.VMEM((L + 8, N), F32),
                        pltpu.VMEM((SSD_HPG, N, SSD_HEAD_DIM), F32)],
        compiler_params=_cparams(("arbitrary", "arbitrary", "arbitrary")),
        name="ssd",
    )(main3, main3, main3, main3, tail3, conv_w, conv_w, conv_w, conv_b, conv_b, conv_b,
      dtb, alog, dexp, norm_w, hexp)


def _nsa_norm_kernel(q_ref, ks_ref, kw_ref, vs_ref, vw_ref, bdq_ref, bdk_ref, wq_ref, wk_ref, srow_ref,
                     qt_ref, kso_ref, kwo_ref, vst_ref, vwt_ref):
    QB, DH, G, HPG = Q_BLOCK, NSA_HEAD_DIM, NSA_N_KV, NSA_HPG
    i = pl.program_id(1)

    def head_norm(x_ref, bd_ref, w_ref):
        x = x_ref[...].astype(F32)
        hi, lo = _split_bf16(x * x)
        ms = (_dot(hi, bd_ref[...]) + _dot(lo, bd_ref[...])) * (1.0 / DH)
        return x * lax.rsqrt(ms + EPS) * w_ref[...]

    qn_t = head_norm(q_ref, bdq_ref, wq_ref).T
    pos = i * QB + lax.broadcasted_iota(I32, (QB, K_AUG - DH), 0)
    lane = lax.broadcasted_iota(I32, (QB, K_AUG - DH), 1)
    lo = pos & 255
    pcols = jnp.where(lane >= 4, 0, jnp.where((lane & 1) == 0, lo, pos - lo)).astype(F32)
    ksn = head_norm(ks_ref, bdk_ref, wk_ref)
    kwn = head_norm(kw_ref, bdk_ref, wk_ref)
    vs_t = vs_ref[...].astype(F32).T
    vw_t = vw_ref[...].astype(F32).T
    ones_rows = jnp.where(lax.broadcasted_iota(I32, (V_AUG - DH, QB), 0) == 0, 1.0, 0.0).astype(BF16)
    for g in range(G):
        heads = [qn_t[(g * HPG + h) * DH:(g * HPG + h + 1) * DH, :] for h in range(HPG)]
        qt_ref[g] = jnp.concatenate([jnp.concatenate(heads, axis=1).astype(BF16), srow_ref[g]], axis=0)
        kso_ref[g] = jnp.concatenate([ksn[:, g * DH:(g + 1) * DH], pcols], axis=1).astype(BF16)
        kwo_ref[g] = jnp.concatenate([kwn[:, g * DH:(g + 1) * DH], pcols], axis=1).astype(BF16)
        vst_ref[g] = jnp.concatenate([vs_t[g * DH:(g + 1) * DH, :].astype(BF16), ones_rows], axis=0)
        vwt_ref[g] = jnp.concatenate([vw_t[g * DH:(g + 1) * DH, :].astype(BF16), ones_rows], axis=0)


def _nsa_norm(main3, bdq, bdk, wq, wk, srow):
    B, S, _ = main3.shape
    QB, DH, G = Q_BLOCK, NSA_HEAD_DIM, NSA_N_KV
    nb = S // QB
    QW, KW, W = NSA_N_HEADS * DH, G * DH, NSA_HPG * QB
    const = lambda shape: pl.BlockSpec(shape, lambda b, i: (0,) * len(shape))
    col = lambda w, c0: pl.BlockSpec((None, QB, w), lambda b, i: (b, i, c0 // w))
    return pl.pallas_call(
        _nsa_norm_kernel,
        grid=(B, nb),
        in_specs=[col(QW, C_Q), col(KW, C_KS), col(KW, C_KW), col(KW, C_VS), col(KW, C_VW),
                  const((QW, QW)), const((KW, KW)), const((1, QW)), const((1, KW)), const(srow.shape)],
        out_specs=[pl.BlockSpec((None, G, None, K_AUG, W), lambda b, i: (b, 0, i, 0, 0)),
                   pl.BlockSpec((None, G, QB, K_AUG), lambda b, i: (b, 0, i, 0)),
                   pl.BlockSpec((None, G, QB, K_AUG), lambda b, i: (b, 0, i, 0)),
                   pl.BlockSpec((None, G, None, V_AUG, QB), lambda b, i: (b, 0, i, 0, 0)),
                   pl.BlockSpec((None, G, None, V_AUG, QB), lambda b, i: (b, 0, i, 0, 0))],
        out_shape=[jax.ShapeDtypeStruct((B, G, nb, K_AUG, W), BF16),
                   jax.ShapeDtypeStruct((B, G, S, K_AUG), BF16), jax.ShapeDtypeStruct((B, G, S, K_AUG), BF16),
                   jax.ShapeDtypeStruct((B, G, nb, V_AUG, QB), BF16),
                   jax.ShapeDtypeStruct((B, G, nb, V_AUG, QB), BF16)],
        compiler_params=_cparams(("arbitrary", "arbitrary")),
        name="nsa_norm_layout",
    )(main3, main3, main3, main3, main3, bdq, bdk, wq, wk, srow)


def _compress_kernel(u_ref, pos_ref, w1_ref, w2_ref, nw_ref, o_ref, *, normalize):
    half = u_ref.shape[1]
    nrow = u_ref.shape[0]
    u = u_ref[...]
    a = _dot(u, w1_ref[0:half, :])
    b = _dot(u, w1_ref[half:2 * half, :])
    posc = _dot(pos_ref[...], w1_ref[...])[0:1, :]
    pre = a + pltpu.roll(b, nrow - 1, 0) + posc
    act = 0.5 * pre * (1.0 + jnp.tanh(math.sqrt(2.0 / math.pi) * (pre + 0.044715 * (pre * pre * pre))))
    o = _dot(act.astype(BF16), w2_ref[...])
    if normalize:
        ms = jnp.mean(o * o, axis=-1, keepdims=True)
        o = o * lax.rsqrt(ms + EPS) * nw_ref[...]
    o_ref[...] = o


def _compress(u, pos8, w1, w2, nw, normalize):
    BG, nrow, half = u.shape
    const = lambda shape: pl.BlockSpec(shape, lambda i: (0, 0))
    return pl.pallas_call(
        functools.partial(_compress_kernel, normalize=normalize),
        grid=(BG,),
        in_specs=[pl.BlockSpec((None, nrow, half), lambda i: (i, 0, 0)),
                  const(pos8.shape), const(w1.shape), const(w2.shape), const(nw.shape)],
        out_specs=pl.BlockSpec((None, nrow, NSA_HEAD_DIM), lambda i: (i, 0, 0)),
        out_shape=jax.ShapeDtypeStruct((BG, nrow, NSA_HEAD_DIM), F32),
        compiler_params=_cparams(("arbitrary",)),
        name="nsa_compress_norm" if normalize else "nsa_compress",
    )(u, pos8, w1, w2, nw)


def _nsa_cmp_kernel(qt_ref, kc_ref, vct_ref, ovt_ref, pm_ref, glt_ref, oct_ref, sb_ref, lst_ref, imp_scr):
    NT = qt_ref.shape[0]
    tiles = range(NT)
    idx = [pl.program_id(2) * NT + t for t in tiles]
    ncmp = kc_ref.shape[0]
    nsel = ovt_ref.shape[0]
    QB = Q_BLOCK
    W = NSA_HPG * QB
    CH = min(LANES, ncmp)
    span = CH * NSA_CMP_STRIDE
    nch = jnp.minimum((idx[-1] * QB + QB - NSA_CMP_LEN) // span + 1, ncmp // CH)

    def attend(rows):
        jrow = lax.broadcasted_iota(I32, (rows, W), 0)
        lane = lax.broadcasted_iota(I32, (rows, W), 1)
        for t in tiles:
            s = _dot(kc_ref[0:rows, :], qt_ref[t])
            mask = (NSA_CMP_STRIDE * jrow + (NSA_CMP_LEN - 1)) <= idx[t] * QB + (lane & (QB - 1))
            sm = jnp.where(mask, s, NEG)
            m = jnp.max(sm, axis=0, keepdims=True)
            p = jnp.where(mask, jnp.exp2(sm - m), 0.0)
            l = jnp.sum(p, axis=0, keepdims=True)
            pn = p * (1.0 / jnp.maximum(l, 1e-30))
            oct_ref[t] = _dot(vct_ref[:, 0:rows], pn.astype(BF16)) * jax.nn.sigmoid(glt_ref[t, 0:1, :])
            psum = pn[:, 0:QB]
            for h in range(1, NSA_HPG):
                psum = psum + pn[:, h * QB:(h + 1) * QB]
            imp_scr[t] = _dot(ovt_ref[:, 0:rows], psum.astype(BF16))

    for k in range(1, ncmp // CH + 1):
        pl.when(nch == k)(functools.partial(attend, k * CH))

    jf = lax.broadcasted_iota(I32, (nsel, QB), 0).astype(F32)
    lane_q = lax.broadcasted_iota(I32, (nsel, QB), 1)
    valid, v0 = [], []
    for t in tiles:
        cur = ((idx[t] * QB + lane_q) >> (NSA_SEL_LEN.bit_length() - 1)).astype(F32)
        forced = (jf == 0.0) | (jf == cur) | (jf == cur - 1.0)
        valid.append(jf <= cur)
        v0.append(jnp.where(forced, FORCED_SCORE, jnp.where(valid[t], imp_scr[t], -1.0)))

    def pick_one(_, carry):
        out = []
        for v, sel in carry:
            mx = jnp.max(v, axis=0, keepdims=True)
            first = jnp.min(jnp.where(v == mx, jf, float(nsel)), axis=0, keepdims=True)
            pick = jf == first
            out.append((jnp.where(pick, -2.0, v), jnp.where(pick, 1.0, sel)))
        return tuple(out)

    picked = lax.fori_loop(0, min(NSA_SEL_TOP, nsel), pick_one,
                           tuple((v0[t], jnp.zeros((nsel, QB), F32)) for t in tiles))

    npair = nsel // 2
    jp = lax.broadcasted_iota(I32, (npair, LANES), 0)
    r2 = lax.broadcasted_iota(I32, (npair, npair), 0)
    c2 = lax.broadcasted_iota(I32, (npair, npair), 1)
    tri = jnp.where(c2 <= r2, 1.0, 0.0).astype(BF16)
    slot = lax.broadcasted_iota(I32, (npair, LANES), 1).astype(F32)
    jpv = lax.broadcasted_iota(I32, (8, npair), 1).astype(F32).astype(BF16)
    r8 = lax.broadcasted_iota(I32, (8, LANES), 0)
    for t in tiles:
        sel = jnp.where(valid[t], picked[t][1], 0.0)
        sb_ref[t] = jnp.where(sel > 0.0, 0.0, NEG)
        pairsel = _dot(pm_ref[...], sel.astype(BF16))
        need = (jnp.sum(pairsel, axis=1, keepdims=True) > 0.0) & (jp < idx[t])
        needf = jnp.where(need, 1.0, 0.0)
        prefix = _dot(tri, needf.astype(BF16))
        onehot = jnp.where(need & (prefix == slot + 1.0), 1.0, 0.0).astype(BF16)
        lst = _dot(jpv, onehot)
        cnt = _dot(jnp.ones((8, npair), BF16), needf.astype(BF16))
        lst_ref[t] = jnp.where(r8 == 0, lst, cnt).astype(I32)


def _nsa_cmp(qt, kc, vct, ovt, pm, glt):
    B, G, nb, _, W = qt.shape
    ncmp = kc.shape[2]
    nsel = ovt.shape[0]
    NT = NSA_CMP_TILES_PER_STEP
    blk = lambda *shape: pl.BlockSpec((None, None, NT) + shape, lambda b, g, i: (b, g, i) + (0,) * len(shape))
    per_bg = lambda *shape: pl.BlockSpec((None, None) + shape, lambda b, g, i: (b, g) + (0,) * len(shape))
    const = lambda shape: pl.BlockSpec(shape, lambda b, g, i: (0,) * len(shape))
    return pl.pallas_call(
        _nsa_cmp_kernel,
        grid=(B, G, nb // NT),
        in_specs=[blk(K_AUG, W), per_bg(ncmp, K_AUG), per_bg(NSA_HEAD_DIM, ncmp),
                  const(ovt.shape), const(pm.shape), blk(8, W)],
        out_specs=[blk(NSA_HEAD_DIM, W), blk(nsel, Q_BLOCK), blk(8, LANES)],
        out_shape=[jax.ShapeDtypeStruct((B, G, nb, NSA_HEAD_DIM, W), F32),
                   jax.ShapeDtypeStruct((B, G, nb, nsel, Q_BLOCK), F32),
                   jax.ShapeDtypeStruct((B, G, nb, 8, LANES), I32)],
        scratch_shapes=[pltpu.VMEM((NT, nsel, Q_BLOCK), F32)],
        compiler_params=_cparams(("arbitrary", "arbitrary", "arbitrary")),
        name="nsa_cmp_select",
    )(qt, kc, vct, ovt, pm, glt)


def _tile4(r):
    return jnp.concatenate([r] * NSA_HPG, axis=1)


def _nsa_sel_kernel(lst_ref, qt_ref, ks_ref, vst_ref, kw_ref, vwt_ref, sb_ref, glt_ref, oct_ref, o_ref,
                    s_scr, d_scr, ow_scr):
    NT = qt_ref.shape[0]
    QB = Q_BLOCK
    W = NSA_HPG * QB
    HB = NSA_SEL_LEN
    DH = NSA_HEAD_DIM
    NPI = NSA_PAIRS_PER_ITER
    tiles = range(NT)
    qts = [qt_ref[t] for t in tiles]
    counts = [lst_ref[t, 1, 0] for t in tiles]
    idx = [pl.program_id(2) * NT + t for t in tiles]

    def flash(s, vt, carry):
        m, acc = carry
        mn = jnp.maximum(m, jnp.max(s, axis=0, keepdims=True))
        p = jnp.exp2(s - mn)
        return mn, jnp.exp2(m - mn) * acc + _dot(vt, p.astype(BF16))

    def pair_scores(t, k):
        jp = lst_ref[t, 0, k]
        live = k < counts[t]
        s = _dot(ks_ref[pl.ds(pl.multiple_of(jp * QB, QB), QB), :], qts[t])
        b0 = jnp.where(live, _tile4(sb_ref[t, pl.ds(2 * jp, 1), :]), NEG)
        b1 = jnp.where(live, _tile4(sb_ref[t, pl.ds(2 * jp + 1, 1), :]), NEG)
        return jnp.concatenate([s[0:HB] + b0, s[HB:2 * HB] + b1], axis=0)

    def scores(t, it):
        return jnp.concatenate([pair_scores(t, NPI * it + u) for u in range(NPI)], axis=0)

    r = lax.broadcasted_iota(I32, (QB, W), 0)
    c = lax.broadcasted_iota(I32, (QB, W), 1) & (QB - 1)
    for t in tiles:
        i = idx[t]
        s_scr[t, 0] = scores(t, 0)
        kd = ks_ref[pl.ds(pl.multiple_of(i * QB, QB), QB), :]
        d_scr[t] = jnp.where(r <= c, _dot(kd, qts[t]), NEG)

        ss, vts = [], []
        for w in range(NSA_WINDOW // QB + 1):
            pw = i - NSA_WINDOW // QB + w
            pc = jnp.maximum(pw, 0)
            kk = kw_ref[pl.ds(pl.multiple_of(pc * QB, QB), QB), :]
            s = _dot(kk, qts[t])
            if w == 0:
                s = jnp.where((r > c) & (pw >= 0), s, NEG)
            elif w == NSA_WINDOW // QB:
                s = jnp.where(r <= c, s, NEG)
            else:
                s = jnp.where(pw >= 0, s, NEG)
            ss.append(s)
            vts.append(vwt_ref[pc])
        s = jnp.concatenate(ss, axis=0)
        p = jnp.exp2(s - jnp.max(s, axis=0, keepdims=True))
        aw = _dot(jnp.concatenate(vts, axis=1), p.astype(BF16))
        o_win = aw[0:DH] * (1.0 / aw[DH:DH + 1])
        ow_scr[t] = oct_ref[t] + jax.nn.sigmoid(glt_ref[t, 2:3, :]) * o_win

    def body(it, carry):
        slot = it & 1
        cur = [s_scr[t, slot] for t in tiles]
        vts = [jnp.concatenate([vst_ref[lst_ref[t, 0, NPI * it + u]] for u in range(NPI)], axis=1) for t in tiles]
        nxt = [scores(t, it + 1) for t in tiles]
        out = tuple(flash(cur[t], vts[t], carry[t]) for t in tiles)
        for t in tiles:
            s_scr[t, 1 - slot] = nxt[t]
        return out

    n_iter = (functools.reduce(jnp.maximum, counts) + NPI - 1) // NPI
    init = tuple((jnp.full((1, W), NEG, F32), jnp.zeros((vst_ref.shape[1], W), F32)) for _ in tiles)
    carry = lax.fori_loop(0, n_iter, body, init)
    for t in tiles:
        m, acc = flash(d_scr[t], vst_ref[idx[t]], carry[t])
        o_t = ow_scr[t] + jax.nn.sigmoid(glt_ref[t, 1:2, :]) * (acc[0:DH] * (1.0 / acc[DH:DH + 1]))
        o_ref[t * QB:(t + 1) * QB, :] = jnp.concatenate(
            [o_t[:, h * QB:(h + 1) * QB].T for h in range(NSA_HPG)], axis=1).astype(BF16)


def _nsa_sel(lst, qt, ks, vst, kw, vwt, sb, glt, oct):
    B, G, nb, _, W = qt.shape
    S = ks.shape[2]
    npair = vst.shape[2]
    nsel = sb.shape[3]
    NT = NSA_TILES_PER_STEP
    blk = lambda *shape: pl.BlockSpec((None, None, NT) + shape, lambda b, g, i: (b, g, i) + (0,) * len(shape))
    per_bg = lambda *shape: pl.BlockSpec((None, None) + shape, lambda b, g, i: (b, g) + (0,) * len(shape))
    return pl.pallas_call(
        _nsa_sel_kernel,
        grid=(B, G, nb // NT),
        in_specs=[pl.BlockSpec((None, None, NT, 8, LANES), lambda b, g, i: (b, g, i, 0, 0),
                               memory_space=pltpu.SMEM),
                  blk(K_AUG, W), per_bg(S, K_AUG), per_bg(npair, V_AUG, Q_BLOCK),
                  per_bg(S, K_AUG), per_bg(npair, V_AUG, Q_BLOCK),
                  blk(nsel, Q_BLOCK), blk(8, W), blk(NSA_HEAD_DIM, W)],
        out_specs=pl.BlockSpec((None, NT * Q_BLOCK, NSA_HPG * NSA_HEAD_DIM), lambda b, g, i: (b, i, g)),
        out_shape=jax.ShapeDtypeStruct((B, S, NSA_N_HEADS * NSA_HEAD_DIM), BF16),
        scratch_shapes=[pltpu.VMEM((NT, 2, NSA_PAIRS_PER_ITER * Q_BLOCK, W), F32),
                        pltpu.VMEM((NT, Q_BLOCK, W), F32), pltpu.VMEM((NT, NSA_HEAD_DIM, W), F32)],
        compiler_params=_cparams(("arbitrary", "arbitrary", "arbitrary")),
        name="nsa_select_window",
    )(lst, qt, ks, vst, kw, vwt, sb, glt, oct)


ROW_TILE = 8


def _store_row_tiles(ref, x):
    n = x.shape[0]
    for c in range(ROW_TILE):
        ref[pl.ds(c, n, stride=ROW_TILE), :] = x[:, c * LANES:(c + 1) * LANES]


def _load_row_tiles(ref, start, n):
    return jnp.concatenate([ref[pl.ds(start + c, n, stride=ROW_TILE), :] for c in range(ROW_TILE)], axis=1)


def _merge_kernel(x_ref, ys_ref, on_ref, gs_ref, gn_ref, wssd_ref, wnsa_ref, wout_ref, ln2_ref,
                  rwh_ref, rwl_ref, rb_ref, x1_ref, xn2_ref, route_ref, cnt_ref, base_ref):
    tm = x_ref.shape[0]

    @pl.when(pl.program_id(0) == 0)
    def _():
        base_ref[...] = jnp.zeros(base_ref.shape, F32)

    y_ssd = _dot(ys_ref[...], wssd_ref[...])
    y_nsa = _dot(on_ref[...], wnsa_ref[...])
    merged = (jax.nn.sigmoid(gs_ref[...].astype(F32)) * y_ssd
              + jax.nn.sigmoid(gn_ref[...].astype(F32)) * y_nsa)
    x1 = x_ref[...] + _dot(merged.astype(BF16), wout_ref[...])
    x1_ref[...] = x1
    ms = jnp.mean(x1 * x1, axis=-1, keepdims=True)
    xn2 = x1 * lax.rsqrt(ms + EPS) * ln2_ref[...]
    _store_row_tiles(xn2_ref, xn2)

    xh, xl = _split_bf16(xn2)
    logits = _dot(xh, rwh_ref[...]) + _dot(xl, rwh_ref[...]) + _dot(xh, rwl_ref[...]) + rb_ref[...]
    lane = lax.broadcasted_iota(I32, (tm, LANES), 1)
    lanef = lane.astype(F32)
    v = logits
    onehot = jnp.zeros((tm, LANES), F32)
    vals, picks = [], []
    for _ in range(TOP_K):
        mx = jnp.max(v, axis=-1, keepdims=True)
        first = jnp.min(jnp.where(v == mx, lanef, float(LANES)), axis=-1, keepdims=True)
        pick = lanef == first
        v = jnp.where(pick, 2.0 * NEG, v)
        onehot = jnp.where(pick, 1.0, onehot)
        vals.append(mx)
        picks.append((pick, first))
    es = [jnp.exp(val - vals[0]) for val in vals]
    inv = 1.0 / (es[0] + es[1] + es[2] + es[3])

    r2 = lax.broadcasted_iota(I32, (tm, tm), 0)
    c2 = lax.broadcasted_iota(I32, (tm, tm), 1)
    stril = jnp.where(c2 < r2, 1.0, 0.0).astype(BF16)
    posmap = base_ref[...] + _dot(stril, onehot.astype(BF16))
    route = jnp.zeros((tm, LANES), F32)
    for k in range(TOP_K):
        pick, first = picks[k]
        pos = jnp.sum(jnp.where(pick, posmap, 0.0), axis=-1, keepdims=True)
        route = jnp.where(lane == k, first, route)
        route = jnp.where(lane == TOP_K + k, pos, route)
        route = jnp.where(lane == 2 * TOP_K + k, es[k] * inv, route)
    route_ref[...] = route
    base = base_ref[...] + jnp.sum(onehot, axis=0, keepdims=True)
    base_ref[...] = base
    cnt_ref[...] = jnp.broadcast_to(base, cnt_ref.shape)


def _merge(x2, yssd, onsa, main, wssd, wnsa, wout, ln2, rwh, rwl, rb, tm):
    T = x2.shape[0]
    const = lambda shape: pl.BlockSpec(shape, lambda i: (0, 0))
    row = lambda w, j=0: pl.BlockSpec((tm, w), lambda i: (i, j))
    return pl.pallas_call(
        _merge_kernel,
        grid=(T // tm,),
        in_specs=[row(D_MODEL), row(SSD_D_INNER), row(D_MODEL), row(D_MODEL, C_MS // D_MODEL),
                  row(D_MODEL, C_MN // D_MODEL),
                  const(wssd.shape), const(wnsa.shape), const(wout.shape), const(ln2.shape),
                  const(rwh.shape), const(rwl.shape), const(rb.shape)],
        out_specs=[row(D_MODEL), pl.BlockSpec((tm * ROW_TILE, LANES), lambda i: (i, 0)), row(LANES),
                   const((8, LANES))],
        out_shape=[jax.ShapeDtypeStruct((T, D_MODEL), F32), jax.ShapeDtypeStruct((T * ROW_TILE, LANES), F32),
                   jax.ShapeDtypeStruct((T, LANES), F32), jax.ShapeDtypeStruct((8, LANES), F32)],
        scratch_shapes=[pltpu.VMEM((1, LANES), F32)],
        compiler_params=_cparams(("arbitrary",)),
        name="merge_router",
    )(x2, yssd, onsa, main, main, wssd, wnsa, wout, ln2, rwh, rwl, rb)


def _dispatch_kernel(dest_ref, xn_ref, xs_init, xs_hbm, sem):
    del xs_init
    RT = ROW_TILE
    tc = xn_ref.shape[0] // RT

    def issue(r2, c):
        for pr in (0, 1):
            r = 2 * r2 + pr
            src = pl.multiple_of((r & (tc - 1)) * RT, RT)
            dst = pl.multiple_of(dest_ref[0, r] * RT, RT)
            pltpu.make_async_copy(xn_ref.at[pl.ds(src, RT)], xs_hbm.at[pl.ds(dst, RT)], sem).start(priority=pr)
        return c

    lax.fori_loop(0, TOP_K * tc // 2, issue, 0, unroll=4)
    for _ in range(TOP_K):
        pltpu.make_async_copy(xn_ref, xs_hbm.at[pl.ds(0, tc * RT)], sem).wait()


def _dispatch(dest3, xn2, n_rows, tc):
    T = xn2.shape[0] // ROW_TILE
    xs0 = jnp.zeros((n_rows * ROW_TILE, LANES), F32)
    return pl.pallas_call(
        _dispatch_kernel,
        grid=(T // tc,),
        in_specs=[pl.BlockSpec((None, 1, TOP_K * tc), lambda i: (i, 0, 0), memory_space=pltpu.SMEM),
                  pl.BlockSpec((tc * ROW_TILE, LANES), lambda i: (i, 0)),
                  pl.BlockSpec(memory_space=pl.ANY)],
        out_specs=pl.BlockSpec(memory_space=pl.ANY),
        out_shape=jax.ShapeDtypeStruct(xs0.shape, F32),
        scratch_shapes=[pltpu.SemaphoreType.DMA(())],
        input_output_aliases={2: 0},
        compiler_params=_cparams(("arbitrary",)),
        name="moe_dispatch",
    )(dest3, xn2, xs0)


def _expert_kernel(be_ref, nu_ref, xs_ref, w1g_ref, w1l_ref, b1g_ref, b1l_ref, w2_ref, b2_ref, ys_ref):
    bm = xs_ref.shape[0] // ROW_TILE
    blk = pl.program_id(0)

    @pl.when(blk < nu_ref[0])
    def _():
        x = _load_row_tiles(xs_ref, 0, bm).astype(BF16)
        glu = jnp.minimum(_dot(x, w1g_ref[...]) + b1g_ref[...], SWIGLU_LIMIT)
        lin = jnp.clip(_dot(x, w1l_ref[...]) + b1l_ref[...], -SWIGLU_LIMIT, SWIGLU_LIMIT)
        act = glu * jax.nn.sigmoid(SWIGLU_ALPHA * glu) * (lin + 1.0)
        _store_row_tiles(ys_ref, _dot(act.astype(BF16), w2_ref[...]) + b2_ref[...])

    @pl.when(blk >= nu_ref[0])
    def _():
        ys_ref[...] = jnp.zeros(ys_ref.shape, F32)


def _experts(block_e, n_used, xs, w1p, b1g, b1l, w2, b2, bm):
    n_blocks = xs.shape[0] // (bm * ROW_TILE)
    DE = w2.shape[1]
    wspec = lambda r, c, j=0: pl.BlockSpec((None, r, c), lambda b, be, nu: (be[b], 0, j))
    rows = pl.BlockSpec((bm * ROW_TILE, LANES), lambda b, be, nu: (b, 0))
    return pl.pallas_call(
        _expert_kernel,
        grid_spec=pltpu.PrefetchScalarGridSpec(
            num_scalar_prefetch=2,
            grid=(n_blocks,),
            in_specs=[rows, wspec(D_MODEL, DE, 0), wspec(D_MODEL, DE, 1), wspec(1, DE), wspec(1, DE),
                      wspec(DE, D_MODEL), wspec(1, D_MODEL)],
            out_specs=rows),
        out_shape=jax.ShapeDtypeStruct(xs.shape, F32),
        compiler_params=_cparams(("arbitrary",)),
        name="moe_experts",
    )(block_e, n_used, xs, w1p, w1p, b1g, b1l, w2, b2)


def _deinterleave_kernel(w_ref, p_ref, o_ref):
    CW = p_ref.shape[0]
    half = w_ref.shape[1] // 2
    for c in range(w_ref.shape[1] // CW):
        y = _dot(w_ref[:, c * CW:(c + 1) * CW].astype(BF16), p_ref[...]).astype(BF16)
        o_ref[:, c * LANES:(c + 1) * LANES] = y[:, 0:LANES]
        o_ref[:, half + c * LANES:half + (c + 1) * LANES] = y[:, LANES:CW]


def _deinterleave(w1):
    E, D, W = w1.shape
    CW = 2 * LANES
    src = np.concatenate([np.arange(0, CW, 2), np.arange(1, CW, 2)])
    perm = jnp.asarray(np.arange(CW)[:, None] == src[None, :], BF16)
    rows = D // 2
    return pl.pallas_call(
        _deinterleave_kernel,
        grid=(E, D // rows),
        in_specs=[pl.BlockSpec((None, rows, W), lambda e, r: (e, r, 0)),
                  pl.BlockSpec((CW, CW), lambda e, r: (0, 0))],
        out_specs=pl.BlockSpec((None, rows, W), lambda e, r: (e, r, 0)),
        out_shape=jax.ShapeDtypeStruct((E, D, W), BF16),
        compiler_params=_cparams(("arbitrary", "arbitrary")),
        name="moe_w1_deinterleave",
    )(w1, perm)


def _combine_kernel(dest_ref, destn_ref, route_ref, x1_ref, ys_hbm, o_ref, buf, sem):
    tc = x1_ref.shape[0]
    i = pl.program_id(0)
    slot = i & 1

    RT = ROW_TILE

    def gather(dest_smem, s):
        def issue(r2, c):
            for pr in (0, 1):
                r = 2 * r2 + pr
                src = pl.multiple_of(dest_smem[0, r] * RT, RT)
                dst = pl.multiple_of(r * RT, RT)
                pltpu.make_async_copy(ys_hbm.at[pl.ds(src, RT)], buf.at[s, pl.ds(dst, RT)],
                                      sem.at[s]).start(priority=pr)
            return c

        lax.fori_loop(0, TOP_K * tc // 2, issue, 0, unroll=4)

    @pl.when(i == 0)
    def _():
        gather(dest_ref, 0)

    for s in (0, 1):
        @pl.when((i + 1 < pl.num_programs(0)) & (slot == 1 - s))
        def _():
            gather(destn_ref, s)

    pltpu.make_async_copy(ys_hbm.at[pl.ds(0, TOP_K * tc * RT)], buf.at[slot], sem.at[slot]).wait()
    route = route_ref[...]
    acc = x1_ref[...]
    for k in range(TOP_K):
        acc = acc + route[:, 2 * TOP_K + k:2 * TOP_K + k + 1] * _load_row_tiles(buf.at[slot], k * tc * RT, tc)
    o_ref[...] = acc


def _combine(dest3, route, x1, ys, tc):
    T = x1.shape[0]
    n = T // tc
    return pl.pallas_call(
        _combine_kernel,
        grid=(n,),
        in_specs=[pl.BlockSpec((None, 1, TOP_K * tc), lambda i: (i, 0, 0), memory_space=pltpu.SMEM),
                  pl.BlockSpec((None, 1, TOP_K * tc), lambda i: (jnp.minimum(i + 1, n - 1), 0, 0),
                               memory_space=pltpu.SMEM),
                  pl.BlockSpec((tc, LANES), lambda i: (i, 0)),
                  pl.BlockSpec((tc, D_MODEL), lambda i: (i, 0)),
                  pl.BlockSpec(memory_space=pl.ANY)],
        out_specs=pl.BlockSpec((tc, D_MODEL), lambda i: (i, 0)),
        out_shape=jax.ShapeDtypeStruct((T, D_MODEL), F32),
        scratch_shapes=[pltpu.VMEM((2, TOP_K * tc * ROW_TILE, LANES), F32), pltpu.SemaphoreType.DMA((2,))],
        compiler_params=_cparams(("arbitrary",)),
        name="moe_combine",
    )(dest3, dest3, route, x1, ys)


def _split_pos(pos):
    lo = pos % 256
    return lo.astype(np.float32), (pos - lo).astype(np.float32)


def _aug_keys(k, pos):
    lo, hi = _split_pos(pos)
    cols = np.stack([lo, hi, lo, hi] + [np.zeros_like(lo)] * (K_AUG - NSA_HEAD_DIM - 4), axis=-1)
    cols = jnp.broadcast_to(jnp.asarray(cols, BF16), k.shape[:-1] + (cols.shape[-1],))
    return jnp.concatenate([k, cols], axis=-1)


def _slope_rows():
    slopes = (2.0 ** (-8.0 * np.arange(1, NSA_N_HEADS + 1) / NSA_N_HEADS)).astype(np.float32)
    sl = jnp.asarray(np.repeat((slopes * np.float32(LOG2E)).reshape(NSA_N_KV, NSA_HPG), Q_BLOCK, axis=1), F32)
    hi, lo = _split_bf16(sl)
    zero = jnp.zeros_like(hi)
    return jnp.stack([hi, hi, lo, lo] + [zero] * (K_AUG - NSA_HEAD_DIM - 4), axis=1)


def _nsa(main3, tail3, q_norm_w, k_norm_w, cmp_pos_k, cmp_pos_v, cmp_k_w1, cmp_k_w2, cmp_v_w1, cmp_v_w2):
    B, S, _ = main3.shape
    T = B * S
    G, HPG, DH, QB = NSA_N_KV, NSA_HPG, NSA_HEAD_DIM, Q_BLOCK
    nb, npair, nsel, ncmp = S // QB, S // QB, S // NSA_SEL_LEN, S // NSA_CMP_STRIDE
    main = main3.reshape(T, MAIN_W)

    hd = np.arange(NSA_N_HEADS * DH) // DH
    bdq = jnp.asarray(hd[:, None] == hd[None, :], BF16)
    bdk = bdq[:G * DH, :G * DH]
    wq = (jnp.tile(q_norm_w.astype(F32), NSA_N_HEADS) * (DH ** -0.5 * LOG2E)).reshape(1, -1)
    wk = jnp.tile(k_norm_w.astype(F32), G).reshape(1, -1)
    qt, ks, kw, vst, vwt = _nsa_norm(main3, bdq, bdk, wq, wk, _slope_rows())

    def kv_heads(t):
        return t.reshape(B, S, G, DH).transpose(0, 2, 1, 3)

    def halves(c0):
        return kv_heads(main[:, c0:c0 + G * DH]).reshape(B * G, ncmp, NSA_CMP_STRIDE * DH)

    def pos8(pos):
        return jnp.zeros((8, NSA_CMP_LEN * DH), F32).at[0].set(pos.reshape(-1)).astype(BF16)

    ones = jnp.ones((1, DH), F32)
    kc = _compress(halves(C_KC), pos8(cmp_pos_k), cmp_k_w1.astype(BF16), cmp_k_w2.astype(BF16),
                   k_norm_w.astype(F32).reshape(1, DH), True)
    vc = _compress(halves(C_VC), pos8(cmp_pos_v), cmp_v_w1.astype(BF16), cmp_v_w2.astype(BF16), ones, False)
    cmp_end = np.arange(ncmp) * NSA_CMP_STRIDE + NSA_CMP_LEN - 1
    kc = _aug_keys(kc.astype(BF16).reshape(B, G, ncmp, DH), cmp_end)
    vct = vc.astype(BF16).reshape(B, G, ncmp, DH).transpose(0, 1, 3, 2)

    c_start = (np.arange(ncmp) * NSA_CMP_STRIDE)[:, None]
    s_start = (np.arange(nsel) * NSA_SEL_LEN)[None, :]
    overlap = np.clip(np.minimum(c_start + NSA_CMP_LEN, s_start + NSA_SEL_LEN)
                      - np.maximum(c_start, s_start), 0, None) / NSA_CMP_LEN
    overlap[(S - NSA_CMP_LEN) // NSA_CMP_STRIDE + 1:] = 0.0
    ovt = jnp.asarray(overlap.T, BF16)
    pm = jnp.asarray(np.arange(nsel)[None, :] // 2 == np.arange(nsel // 2)[:, None], BF16)

    gl = tail3.reshape(B, nb, QB, G, LANES)[..., 8:8 + 3 * HPG].reshape(B, nb, QB, G, HPG, 3)
    glt = gl.transpose(0, 3, 1, 5, 4, 2).reshape(B, G, nb, 3, HPG * QB)
    glt = jnp.concatenate([glt, jnp.zeros((B, G, nb, 5, HPG * QB), F32)], axis=3)

    oct, sb, lst = _nsa_cmp(qt, kc, vct, ovt, pm, glt)
    return _nsa_sel(lst, qt, ks, vst, kw, vwt, sb, glt, oct).reshape(T, NSA_N_HEADS * DH)


def _moe_tables(route, cnt, T, bm):
    idx = route[:, 0:TOP_K].astype(I32)
    pos = route[:, TOP_K:2 * TOP_K].astype(I32)
    counts = cnt[0, :N_EXPERTS].astype(I32)
    padded = (counts + bm - 1) // bm * bm
    pend = jnp.cumsum(padded)
    pstart = pend - padded
    dest = pstart[idx] + pos
    n_blocks = -(-(T * TOP_K + N_EXPERTS * (bm - 1)) // bm)
    b_start = jnp.arange(n_blocks, dtype=I32) * bm
    block_e = jnp.minimum(jnp.sum(b_start[:, None] >= pend[None, :], axis=1), N_EXPERTS - 1).astype(I32)
    n_used = (pend[-1] // bm).astype(I32).reshape(1)
    return dest, n_blocks * bm, block_e, n_used


def kernel(x, ln1_w, w_in, ssd_conv_w, ssd_conv_b, ssd_dt_bias, ssd_a_log, ssd_d, ssd_norm_w, ssd_out_w,
           nsa_q_norm_w, nsa_k_norm_w, cmp_pos_k, cmp_pos_v, cmp_k_w1, cmp_k_w2, cmp_v_w1, cmp_v_w2,
           nsa_out_w, w_out, ln2_w, router_w, router_b, exp_w1, exp_b1, exp_w2, exp_b2):
    B, S, D = x.shape
    T = B * S
    depth = ln1_w.shape[0]
    x2 = x.reshape(T, D)
    G = SSD_N_GROUPS

    main_parts = ((0, O_DT), (O_MERGE, O_MERGE + 2 * D_MODEL), (O_Q, O_GATE))

    def group_lanes(v):
        out = jnp.zeros((G, LANES), F32).at[:, :SSD_HPG].set(v.astype(F32).reshape(G, SSD_HPG))
        return out.reshape(1, TAIL_W)

    tm_rows = min(1024, T)
    bm = 512
    for l in range(depth):
        w_main = jnp.concatenate([w_in[l][:, a:b].astype(BF16) for a, b in main_parts], axis=1)
        zpad = jnp.zeros((D, LANES - SSD_HPG - 3 * NSA_HPG), BF16)
        w_tail = jnp.concatenate(
            [part for g in range(G) for part in (
                w_in[l][:, O_DT + SSD_HPG * g:O_DT + SSD_HPG * (g + 1)].astype(BF16),
                w_in[l][:, O_GATE + 3 * NSA_HPG * g:O_GATE + 3 * NSA_HPG * (g + 1)].astype(BF16), zpad)], axis=1)
        main, tail = _inproj(x2, ln1_w[l].reshape(1, D), w_main, w_tail, tm=tm_rows, tn=MAIN_W // 4)
        main3 = main.reshape(B, S, MAIN_W)
        tail3 = tail.reshape(B, S, TAIL_W)

        yssd = _ssd(main3, tail3, ssd_conv_w[l], ssd_conv_b[l].reshape(1, -1),
                    group_lanes(ssd_dt_bias[l]), group_lanes(ssd_a_log[l]),
                    jnp.repeat(ssd_d[l].astype(F32), SSD_HEAD_DIM).reshape(1, -1),
                    ssd_norm_w[l].reshape(1, -1))
        onsa = _nsa(main3, tail3, nsa_q_norm_w[l], nsa_k_norm_w[l], cmp_pos_k[l], cmp_pos_v[l],
                    cmp_k_w1[l], cmp_k_w2[l], cmp_v_w1[l], cmp_v_w2[l])

        rw = jnp.zeros((D, LANES), F32).at[:, :N_EXPERTS].set(router_w[l])
        rwh, rwl = _split_bf16(rw)
        rb = jnp.full((1, LANES), NEG, F32).at[0, :N_EXPERTS].set(router_b[l])
        x1, xn2, route, cnt = _merge(x2, yssd.reshape(T, -1), onsa, main,
                                     ssd_out_w[l].astype(BF16), nsa_out_w[l].astype(BF16), w_out[l].astype(BF16),
                                     ln2_w[l].reshape(1, D), rwh, rwl, rb, tm=min(512, T))

        dest, n_rows, block_e, n_used = _moe_tables(route, cnt, T, bm)
        tc = min(256, T)
        dest3 = dest.reshape(T // tc, tc, TOP_K).transpose(0, 2, 1).reshape(T // tc, 1, TOP_K * tc)
        xs = _dispatch(dest3, xn2, n_rows, tc)
        ys = _experts(block_e, n_used, xs, _deinterleave(exp_w1[l]),
                      exp_b1[l][:, None, 0::2], exp_b1[l][:, None, 1::2],
                      exp_w2[l].astype(BF16), exp_b2[l][:, None, :], bm)
        x2 = _combine(dest3, route, x1, ys, tc)
    return x2.reshape(B, S, D)
```

```python
import functools
import math

import numpy as np
import jax
import jax.numpy as jnp
from jax import lax
from jax.experimental import pallas as pl
from jax.experimental.pallas import tpu as pltpu

F32 = jnp.float32
BF16 = jnp.bfloat16
I32 = jnp.int32

D_MODEL = 1024
SSD_D_INNER = 2048
SSD_HEAD_DIM = 64
SSD_N_HEADS = 32
SSD_N_GROUPS = 4
SSD_D_STATE = 128
SSD_CHUNK = 256
SSD_HPG = SSD_N_HEADS // SSD_N_GROUPS
NSA_N_HEADS = 16
NSA_N_KV = 4
NSA_HPG = NSA_N_HEADS // NSA_N_KV
NSA_HEAD_DIM = 64
NSA_CMP_LEN = 32
NSA_CMP_STRIDE = 16
NSA_CMP_HIDDEN = 256
NSA_SEL_LEN = 64
NSA_SEL_TOP = 16
NSA_WINDOW = 512
Q_BLOCK = 128
FORCED_SCORE = 1.0e4
N_EXPERTS = 32
TOP_K = 4
SWIGLU_LIMIT = 7.0
SWIGLU_ALPHA = 1.702
EPS = 1e-6

NEG = -1.0e30
LOG2E = 1.4426950408889634
LANES = 128
K_AUG = 80
V_AUG = 80
NSA_PAIRS_PER_ITER = 2
NSA_TILES_PER_STEP = 2
NSA_CMP_TILES_PER_STEP = 2
VMEM_LIMIT = 56 * 1024 * 1024

C_Z, C_XS, C_B, C_C, C_MS, C_MN, C_Q, C_KC, C_VC, C_KS, C_VS, C_KW, C_VW = (
    0, 2048, 4096, 4608, 5120, 6144, 7168, 8192, 8448, 8704, 8960, 9216, 9472)
MAIN_W = 9728
TAIL_W = 512
O_DT, O_Q, O_GATE, O_MERGE = 5120, 5152, 7712, 7760


def _cparams(sem):
    return pltpu.CompilerParams(dimension_semantics=sem, vmem_limit_bytes=VMEM_LIMIT)


def _dot(a, b):
    return jnp.dot(a, b, preferred_element_type=F32)


def _split_bf16(x):
    hi = x.astype(BF16)
    lo = (x - hi.astype(F32)).astype(BF16)
    return hi, lo


def _inproj_kernel(x_ref, lnw_ref, w_ref, wt_ref, main_ref, tail_ref, xn_ref):
    @pl.when(pl.program_id(1) == 0)
    def _():
        x = x_ref[...]
        ms = jnp.mean(x * x, axis=-1, keepdims=True)
        xn = (x * lax.rsqrt(ms + EPS) * lnw_ref[...]).astype(BF16)
        xn_ref[...] = xn
        tail_ref[...] = _dot(xn, wt_ref[...])

    main_ref[...] = _dot(xn_ref[...], w_ref[...]).astype(BF16)


def _inproj(x2, ln_w, w_main, w_tail, tm, tn):
    T = x2.shape[0]
    return pl.pallas_call(
        _inproj_kernel,
        grid=(T // tm, MAIN_W // tn),
        in_specs=[pl.BlockSpec((tm, D_MODEL), lambda i, j: (i, 0)),
                  pl.BlockSpec((1, D_MODEL), lambda i, j: (0, 0)),
                  pl.BlockSpec((D_MODEL, tn), lambda i, j: (0, j)),
                  pl.BlockSpec((D_MODEL, TAIL_W), lambda i, j: (0, 0))],
        out_specs=[pl.BlockSpec((tm, tn), lambda i, j: (i, j)),
                   pl.BlockSpec((tm, TAIL_W), lambda i, j: (i, 0))],
        out_shape=[jax.ShapeDtypeStruct((T, MAIN_W), BF16),
                   jax.ShapeDtypeStruct((T, TAIL_W), F32)],
        scratch_shapes=[pltpu.VMEM((tm, D_MODEL), BF16)],
        compiler_params=_cparams(("arbitrary", "arbitrary")),
        name="inproj",
    )(x2, ln_w, w_main, w_tail)


def _softplus(x):
    return jnp.maximum(x, 0.0) + jnp.log1p(jnp.exp(-jnp.abs(x)))


def _silu(x):
    return x * (0.5 * jnp.tanh(0.5 * x) + 0.5)


def _ssd_kernel(z_ref, xs_ref, b_ref, c_ref, dt_ref, cwx_ref, cwb_ref, cwc_ref, cbx_ref, cbb_ref, cbc_ref,
                dtb_ref, alog_ref, dexp_ref, nw_ref, hexp_ref, y_ref, extx, extb, extc, hs_ref):
    L = xs_ref.shape[0]
    P = SSD_HEAD_DIM

    @pl.when(pl.program_id(2) == 0)
    def _():
        extx[0:8, :] = jnp.zeros((8, extx.shape[1]), F32)
        extb[0:8, :] = jnp.zeros((8, extb.shape[1]), F32)
        extc[0:8, :] = jnp.zeros((8, extc.shape[1]), F32)
        hs_ref[...] = jnp.zeros(hs_ref.shape, F32)

    def conv_act(x_ref, w_ref, bias_ref, ext):
        xf = x_ref[...].astype(F32)
        ext[8:8 + L, :] = xf
        acc = bias_ref[...] + w_ref[3:4, :] * xf
        for k in (1, 2, 3):
            acc = acc + w_ref[3 - k:4 - k, :] * ext[8 - k:8 - k + L, :]
        ext[0:8, :] = ext[L:L + 8, :]
        return _silu(acc)

    xs = conv_act(xs_ref, cwx_ref, cbx_ref, extx)
    bm = conv_act(b_ref, cwb_ref, cbb_ref, extb)
    cm = conv_act(c_ref, cwc_ref, cbc_ref, extc)

    dt = _softplus(dt_ref[...] + dtb_ref[...])
    da = dt * (-jnp.exp(alog_ref[...]))
    row = lax.broadcasted_iota(I32, (L, L), 0)
    col = lax.broadcasted_iota(I32, (L, L), 1)
    tril = row >= col
    trif = jnp.where(tril, 1.0, 0.0).astype(BF16)
    da_hi, da_lo = _split_bf16(da)
    acum = _dot(trif, da_hi) + _dot(trif, da_lo)
    acum_t = acum.T
    a_end = acum[L - 1:L, :]
    ea = jnp.exp(acum)
    wend = jnp.exp(a_end - acum) * dt
    eend = jnp.exp(a_end)

    cb16 = cm.astype(BF16)
    bt16 = bm.T.astype(BF16)
    cb = _dot(cb16, bt16)

    def head_lanes(v):
        hi, lo = _split_bf16(v)
        return _dot(hi, hexp_ref[...]) + _dot(lo, hexp_ref[...])

    xdt = (xs * head_lanes(dt)).astype(BF16)
    xw = (xs * head_lanes(wend)).astype(BF16)
    y_intra, y_inter = [], []
    for hh in range(SSD_HPG):
        seg = acum[:, hh:hh + 1] - acum_t[hh:hh + 1, :]
        decay = jnp.exp(jnp.where(tril, seg, NEG))
        g = (cb * decay).astype(BF16)
        hprev = hs_ref[hh]
        y_intra.append(_dot(g, xdt[:, hh * P:(hh + 1) * P]))
        y_inter.append(_dot(cb16, hprev.astype(BF16)))
        hs_ref[hh] = hprev * eend[:, hh:hh + 1] + _dot(bt16, xw[:, hh * P:(hh + 1) * P])
    y = jnp.concatenate(y_intra, axis=1) + jnp.concatenate(y_inter, axis=1) * head_lanes(ea)
    y = y + dexp_ref[...] * xs
    y = y * _silu(z_ref[...].astype(F32))
    ms = jnp.mean(y * y, axis=-1, keepdims=True)
    y_ref[...] = (y * lax.rsqrt(ms + EPS) * nw_ref[...]).astype(BF16)


def _ssd(main3, tail3, conv_w, conv_b, dtb, alog, dexp, norm_w):
    B, S, _ = main3.shape
    L = math.gcd(S, SSD_CHUNK)
    nc = S // L
    G, N, GW = SSD_N_GROUPS, SSD_D_STATE, SSD_D_INNER // SSD_N_GROUPS
    xs0, b0, c0 = C_XS // GW, C_B // N, C_C // N
    cb0, cc0 = SSD_D_INNER // N, (SSD_D_INNER + G * N) // N
    hexp = jnp.asarray(np.arange(LANES)[:, None] == np.arange(GW)[None, :] // SSD_HEAD_DIM, BF16)

    def seq(w, off):
        return pl.BlockSpec((None, L, w), lambda b, g, c: (b, c, off + g))

    def par(r, w, off):
        return pl.BlockSpec((r, w), lambda b, g, c: (0, off + g))

    return pl.pallas_call(
        _ssd_kernel,
        grid=(B, G, nc),
        in_specs=[seq(GW, 0), seq(GW, xs0), seq(N, b0), seq(N, c0), seq(LANES, 0),
                  par(4, GW, 0), par(4, N, cb0), par(4, N, cc0),
                  par(1, GW, 0), par(1, N, cb0), par(1, N, cc0),
                  par(1, LANES, 0), par(1, LANES, 0), par(1, GW, 0), par(1, GW, 0),
                  pl.BlockSpec((LANES, GW), lambda b, g, c: (0, 0))],
        out_specs=pl.BlockSpec((None, L, GW), lambda b, g, c: (b, c, g)),
        out_shape=jax.ShapeDtypeStruct((B, S, SSD_D_INNER), BF16),
        scratch_shapes=[pltpu.VMEM((L + 8, GW), F32), pltpu.VMEM((L + 8, N), F32), pltpu.VMEM((L + 8, N), F32),
                        pltpu.VMEM((SSD_HPG, N, SSD_HEAD_DIM), F32)],
        compiler_params=_cparams(("arbitrary", "arbitrary", "arbitrary")),
        name="ssd",
    )(main3, main3, main3, main3, tail3, conv_w, conv_w, conv_w, conv_b, conv_b, conv_b,
      dtb, alog, dexp, norm_w, hexp)


def _nsa_norm_kernel(q_ref, ks_ref, kw_ref, vs_ref, vw_ref, bdq_ref, bdk_ref, wq_ref, wk_ref, srow_ref,
                     qt_ref, kso_ref, kwo_ref, vst_ref, vwt_ref):
    QB, DH, G, HPG = Q_BLOCK, NSA_HEAD_DIM, NSA_N_KV, NSA_HPG
    i = pl.program_id(1)

    def head_norm(x_ref, bd_ref, w_ref):
        x = x_ref[...].astype(F32)
        hi, lo = _split_bf16(x * x)
        ms = (_dot(hi, bd_ref[...]) + _dot(lo, bd_ref[...])) * (1.0 / DH)
        return x * lax.rsqrt(ms + EPS) * w_ref[...]

    qn_t = head_norm(q_ref, bdq_ref, wq_ref).T
    pos = i * QB + lax.broadcasted_iota(I32, (QB, K_AUG - DH), 0)
    lane = lax.broadcasted_iota(I32, (QB, K_AUG - DH), 1)
    lo = pos & 255
    pcols = jnp.where(lane >= 4, 0, jnp.where((lane & 1) == 0, lo, pos - lo)).astype(F32)
    ksn = head_norm(ks_ref, bdk_ref, wk_ref)
    kwn = head_norm(kw_ref, bdk_ref, wk_ref)
    vs_t = vs_ref[...].astype(F32).T
    vw_t = vw_ref[...].astype(F32).T
    ones_rows = jnp.where(lax.broadcasted_iota(I32, (V_AUG - DH, QB), 0) == 0, 1.0, 0.0).astype(BF16)
    for g in range(G):
        heads = [qn_t[(g * HPG + h) * DH:(g * HPG + h + 1) * DH, :] for h in range(HPG)]
        qt_ref[g] = jnp.concatenate([jnp.concatenate(heads, axis=1).astype(BF16), srow_ref[g]], axis=0)
        kso_ref[g] = jnp.concatenate([ksn[:, g * DH:(g + 1) * DH], pcols], axis=1).astype(BF16)
        kwo_ref[g] = jnp.concatenate([kwn[:, g * DH:(g + 1) * DH], pcols], axis=1).astype(BF16)
        vst_ref[g] = jnp.concatenate([vs_t[g * DH:(g + 1) * DH, :].astype(BF16), ones_rows], axis=0)
        vwt_ref[g] = jnp.concatenate([vw_t[g * DH:(g + 1) * DH, :].astype(BF16), ones_rows], axis=0)


def _nsa_norm(main3, bdq, bdk, wq, wk, srow):
    B, S, _ = main3.shape
    QB, DH, G = Q_BLOCK, NSA_HEAD_DIM, NSA_N_KV
    nb = S // QB
    QW, KW, W = NSA_N_HEADS * DH, G * DH, NSA_HPG * QB
    const = lambda shape: pl.BlockSpec(shape, lambda b, i: (0,) * len(shape))
    col = lambda w, c0: pl.BlockSpec((None, QB, w), lambda b, i: (b, i, c0 // w))
    return pl.pallas_call(
        _nsa_norm_kernel,
        grid=(B, nb),
        in_specs=[col(QW, C_Q), col(KW, C_KS), col(KW, C_KW), col(KW, C_VS), col(KW, C_VW),
                  const((QW, QW)), const((KW, KW)), const((1, QW)), const((1, KW)), const(srow.shape)],
        out_specs=[pl.BlockSpec((None, G, None, K_AUG, W), lambda b, i: (b, 0, i, 0, 0)),
                   pl.BlockSpec((None, G, QB, K_AUG), lambda b, i: (b, 0, i, 0)),
                   pl.BlockSpec((None, G, QB, K_AUG), lambda b, i: (b, 0, i, 0)),
                   pl.BlockSpec((None, G, None, V_AUG, QB), lambda b, i: (b, 0, i, 0, 0)),
                   pl.BlockSpec((None, G, None, V_AUG, QB), lambda b, i: (b, 0, i, 0, 0))],
        out_shape=[jax.ShapeDtypeStruct((B, G, nb, K_AUG, W), BF16),
                   jax.ShapeDtypeStruct((B, G, S, K_AUG), BF16), jax.ShapeDtypeStruct((B, G, S, K_AUG), BF16),
                   jax.ShapeDtypeStruct((B, G, nb, V_AUG, QB), BF16),
                   jax.ShapeDtypeStruct((B, G, nb, V_AUG, QB), BF16)],
        compiler_params=_cparams(("arbitrary", "arbitrary")),
        name="nsa_norm_layout",
    )(main3, main3, main3, main3, main3, bdq, bdk, wq, wk, srow)


def _compress_kernel(u_ref, pos_ref, w1_ref, w2_ref, nw_ref, o_ref, *, normalize):
    half = u_ref.shape[1]
    nrow = u_ref.shape[0]
    u = u_ref[...]
    a = _dot(u, w1_ref[0:half, :])
    b = _dot(u, w1_ref[half:2 * half, :])
    posc = _dot(pos_ref[...], w1_ref[...])[0:1, :]
    pre = a + pltpu.roll(b, nrow - 1, 0) + posc
    act = 0.5 * pre * (1.0 + jnp.tanh(math.sqrt(2.0 / math.pi) * (pre + 0.044715 * (pre * pre * pre))))
    o = _dot(act.astype(BF16), w2_ref[...])
    if normalize:
        ms = jnp.mean(o * o, axis=-1, keepdims=True)
        o = o * lax.rsqrt(ms + EPS) * nw_ref[...]
    o_ref[...] = o


def _compress(u, pos8, w1, w2, nw, normalize):
    BG, nrow, half = u.shape
    const = lambda shape: pl.BlockSpec(shape, lambda i: (0, 0))
    return pl.pallas_call(
        functools.partial(_compress_kernel, normalize=normalize),
        grid=(BG,),
        in_specs=[pl.BlockSpec((None, nrow, half), lambda i: (i, 0, 0)),
                  const(pos8.shape), const(w1.shape), const(w2.shape), const(nw.shape)],
        out_specs=pl.BlockSpec((None, nrow, NSA_HEAD_DIM), lambda i: (i, 0, 0)),
        out_shape=jax.ShapeDtypeStruct((BG, nrow, NSA_HEAD_DIM), F32),
        compiler_params=_cparams(("arbitrary",)),
        name="nsa_compress_norm" if normalize else "nsa_compress",
    )(u, pos8, w1, w2, nw)


def _nsa_cmp_kernel(qt_ref, kc_ref, vct_ref, ovt_ref, pm_ref, glt_ref, oct_ref, sb_ref, lst_ref, imp_scr):
    NT = qt_ref.shape[0]
    tiles = range(NT)
    idx = [pl.program_id(2) * NT + t for t in tiles]
    ncmp = kc_ref.shape[0]
    nsel = ovt_ref.shape[0]
    QB = Q_BLOCK
    W = NSA_HPG * QB
    CH = min(LANES, ncmp)
    span = CH * NSA_CMP_STRIDE
    nch = jnp.minimum((idx[-1] * QB + QB - NSA_CMP_LEN) // span + 1, ncmp // CH)

    def attend(rows):
        jrow = lax.broadcasted_iota(I32, (rows, W), 0)
        lane = lax.broadcasted_iota(I32, (rows, W), 1)
        for t in tiles:
            s = _dot(kc_ref[0:rows, :], qt_ref[t])
            mask = (NSA_CMP_STRIDE * jrow + (NSA_CMP_LEN - 1)) <= idx[t] * QB + (lane & (QB - 1))
            sm = jnp.where(mask, s, NEG)
            m = jnp.max(sm, axis=0, keepdims=True)
            p = jnp.where(mask, jnp.exp2(sm - m), 0.0)
            l = jnp.sum(p, axis=0, keepdims=True)
            pn = p * (1.0 / jnp.maximum(l, 1e-30))
            oct_ref[t] = _dot(vct_ref[:, 0:rows], pn.astype(BF16)) * jax.nn.sigmoid(glt_ref[t, 0:1, :])
            psum = pn[:, 0:QB]
            for h in range(1, NSA_HPG):
                psum = psum + pn[:, h * QB:(h + 1) * QB]
            imp_scr[t] = _dot(ovt_ref[:, 0:rows], psum.astype(BF16))

    for k in range(1, ncmp // CH + 1):
        pl.when(nch == k)(functools.partial(attend, k * CH))

    jf = lax.broadcasted_iota(I32, (nsel, QB), 0).astype(F32)
    lane_q = lax.broadcasted_iota(I32, (nsel, QB), 1)
    valid, v0 = [], []
    for t in tiles:
        cur = ((idx[t] * QB + lane_q) >> (NSA_SEL_LEN.bit_length() - 1)).astype(F32)
        forced = (jf == 0.0) | (jf == cur) | (jf == cur - 1.0)
        valid.append(jf <= cur)
        v0.append(jnp.where(forced, FORCED_SCORE, jnp.where(valid[t], imp_scr[t], -1.0)))

    def pick_one(_, carry):
        out = []
        for v, sel in carry:
            mx = jnp.max(v, axis=0, keepdims=True)
            first = jnp.min(jnp.where(v == mx, jf, float(nsel)), axis=0, keepdims=True)
            pick = jf == first
            out.append((jnp.where(pick, -2.0, v), jnp.where(pick, 1.0, sel)))
        return tuple(out)

    picked = lax.fori_loop(0, min(NSA_SEL_TOP, nsel), pick_one,
                           tuple((v0[t], jnp.zeros((nsel, QB), F32)) for t in tiles))

    npair = nsel // 2
    jp = lax.broadcasted_iota(I32, (npair, LANES), 0)
    r2 = lax.broadcasted_iota(I32, (npair, npair), 0)
    c2 = lax.broadcasted_iota(I32, (npair, npair), 1)
    tri = jnp.where(c2 <= r2, 1.0, 0.0).astype(BF16)
    slot = lax.broadcasted_iota(I32, (npair, LANES), 1).astype(F32)
    jpv = lax.broadcasted_iota(I32, (8, npair), 1).astype(F32).astype(BF16)
    r8 = lax.broadcasted_iota(I32, (8, LANES), 0)
    for t in tiles:
        sel = jnp.where(valid[t], picked[t][1], 0.0)
        sb_ref[t] = jnp.where(sel > 0.0, 0.0, NEG)
        pairsel = _dot(pm_ref[...], sel.astype(BF16))
        need = (jnp.sum(pairsel, axis=1, keepdims=True) > 0.0) & (jp < idx[t])
        needf = jnp.where(need, 1.0, 0.0)
        prefix = _dot(tri, needf.astype(BF16))
        onehot = jnp.where(need & (prefix == slot + 1.0), 1.0, 0.0).astype(BF16)
        lst = _dot(jpv, onehot)
        cnt = _dot(jnp.ones((8, npair), BF16), needf.astype(BF16))
        lst_ref[t] = jnp.where(r8 == 0, lst, cnt).astype(I32)


def _nsa_cmp(qt, kc, vct, ovt, pm, glt):
    B, G, nb, _, W = qt.shape
    ncmp = kc.shape[2]
    nsel = ovt.shape[0]
    NT = NSA_CMP_TILES_PER_STEP
    blk = lambda *shape: pl.BlockSpec((None, None, NT) + shape, lambda b, g, i: (b, g, i) + (0,) * len(shape))
    per_bg = lambda *shape: pl.BlockSpec((None, None) + shape, lambda b, g, i: (b, g) + (0,) * len(shape))
    const = lambda shape: pl.BlockSpec(shape, lambda b, g, i: (0,) * len(shape))
    return pl.pallas_call(
        _nsa_cmp_kernel,
        grid=(B, G, nb // NT),
        in_specs=[blk(K_AUG, W), per_bg(ncmp, K_AUG), per_bg(NSA_HEAD_DIM, ncmp),
                  const(ovt.shape), const(pm.shape), blk(8, W)],
        out_specs=[blk(NSA_HEAD_DIM, W), blk(nsel, Q_BLOCK), blk(8, LANES)],
        out_shape=[jax.ShapeDtypeStruct((B, G, nb, NSA_HEAD_DIM, W), F32),
                   jax.ShapeDtypeStruct((B, G, nb, nsel, Q_BLOCK), F32),
                   jax.ShapeDtypeStruct((B, G, nb, 8, LANES), I32)],
        scratch_shapes=[pltpu.VMEM((NT, nsel, Q_BLOCK), F32)],
        compiler_params=_cparams(("arbitrary", "arbitrary", "arbitrary")),
        name="nsa_cmp_select",
    )(qt, kc, vct, ovt, pm, glt)


def _tile4(r):
    return jnp.concatenate([r] * NSA_HPG, axis=1)


def _nsa_sel_kernel(lst_ref, qt_ref, ks_ref, vst_ref, kw_ref, vwt_ref, sb_ref, glt_ref, oct_ref, o_ref,
                    s_scr, d_scr, ow_scr):
    NT = qt_ref.shape[0]
    QB = Q_BLOCK
    W = NSA_HPG * QB
    HB = NSA_SEL_LEN
    DH = NSA_HEAD_DIM
    NPI = NSA_PAIRS_PER_ITER
    tiles = range(NT)
    qts = [qt_ref[t] for t in tiles]
    counts = [lst_ref[t, 1, 0] for t in tiles]
    idx = [pl.program_id(2) * NT + t for t in tiles]

    def flash(s, vt, carry):
        m, acc = carry
        mn = jnp.maximum(m, jnp.max(s, axis=0, keepdims=True))
        p = jnp.exp2(s - mn)
        return mn, jnp.exp2(m - mn) * acc + _dot(vt, p.astype(BF16))

    def pair_scores(t, k):
        jp = lst_ref[t, 0, k]
        live = k < counts[t]
        s = _dot(ks_ref[pl.ds(pl.multiple_of(jp * QB, QB), QB), :], qts[t])
        b0 = jnp.where(live, _tile4(sb_ref[t, pl.ds(2 * jp, 1), :]), NEG)
        b1 = jnp.where(live, _tile4(sb_ref[t, pl.ds(2 * jp + 1, 1), :]), NEG)
        return jnp.concatenate([s[0:HB] + b0, s[HB:2 * HB] + b1], axis=0)

    def scores(t, it):
        return jnp.concatenate([pair_scores(t, NPI * it + u) for u in range(NPI)], axis=0)

    r = lax.broadcasted_iota(I32, (QB, W), 0)
    c = lax.broadcasted_iota(I32, (QB, W), 1) & (QB - 1)
    for t in tiles:
        i = idx[t]
        s_scr[t, 0] = scores(t, 0)
        kd = ks_ref[pl.ds(pl.multiple_of(i * QB, QB), QB), :]
        d_scr[t] = jnp.where(r <= c, _dot(kd, qts[t]), NEG)

        ss, vts = [], []
        for w in range(NSA_WINDOW // QB + 1):
            pw = i - NSA_WINDOW // QB + w
            pc = jnp.maximum(pw, 0)
            kk = kw_ref[pl.ds(pl.multiple_of(pc * QB, QB), QB), :]
            s = _dot(kk, qts[t])
            if w == 0:
                s = jnp.where((r > c) & (pw >= 0), s, NEG)
            elif w == NSA_WINDOW // QB:
                s = jnp.where(r <= c, s, NEG)
            else:
                s = jnp.where(pw >= 0, s, NEG)
            ss.append(s)
            vts.append(vwt_ref[pc])
        s = jnp.concatenate(ss, axis=0)
        p = jnp.exp2(s - jnp.max(s, axis=0, keepdims=True))
        aw = _dot(jnp.concatenate(vts, axis=1), p.astype(BF16))
        o_win = aw[0:DH] * (1.0 / aw[DH:DH + 1])
        ow_scr[t] = oct_ref[t] + jax.nn.sigmoid(glt_ref[t, 2:3, :]) * o_win

    def body(it, carry):
        slot = it & 1
        cur = [s_scr[t, slot] for t in tiles]
        vts = [jnp.concatenate([vst_ref[lst_ref[t, 0, NPI * it + u]] for u in range(NPI)], axis=1) for t in tiles]
        nxt = [scores(t, it + 1) for t in tiles]
        out = tuple(flash(cur[t], vts[t], carry[t]) for t in tiles)
        for t in tiles:
            s_scr[t, 1 - slot] = nxt[t]
        return out

    n_iter = (functools.reduce(jnp.maximum, counts) + NPI - 1) // NPI
    init = tuple((jnp.full((1, W), NEG, F32), jnp.zeros((vst_ref.shape[1], W), F32)) for _ in tiles)
    carry = lax.fori_loop(0, n_iter, body, init)
    for t in tiles:
        m, acc = flash(d_scr[t], vst_ref[idx[t]], carry[t])
        o_t = ow_scr[t] + jax.nn.sigmoid(glt_ref[t, 1:2, :]) * (acc[0:DH] * (1.0 / acc[DH:DH + 1]))
        o_ref[t * QB:(t + 1) * QB, :] = jnp.concatenate(
            [o_t[:, h * QB:(h + 1) * QB].T for h in range(NSA_HPG)], axis=1).astype(BF16)


def _nsa_sel(lst, qt, ks, vst, kw, vwt, sb, glt, oct):
    B, G, nb, _, W = qt.shape
    S = ks.shape[2]
    npair = vst.shape[2]
    nsel = sb.shape[3]
    NT = NSA_TILES_PER_STEP
    blk = lambda *shape: pl.BlockSpec((None, None, NT) + shape, lambda b, g, i: (b, g, i) + (0,) * len(shape))
    per_bg = lambda *shape: pl.BlockSpec((None, None) + shape, lambda b, g, i: (b, g) + (0,) * len(shape))
    return pl.pallas_call(
        _nsa_sel_kernel,
        grid=(B, G, nb // NT),
        in_specs=[pl.BlockSpec((None, None, NT, 8, LANES), lambda b, g, i: (b, g, i, 0, 0),
                               memory_space=pltpu.SMEM),
                  blk(K_AUG, W), per_bg(S, K_AUG), per_bg(npair, V_AUG, Q_BLOCK),
                  per_bg(S, K_AUG), per_bg(npair, V_AUG, Q_BLOCK),
                  blk(nsel, Q_BLOCK), blk(8, W), blk(NSA_HEAD_DIM, W)],
        out_specs=pl.BlockSpec((None, NT * Q_BLOCK, NSA_HPG * NSA_HEAD_DIM), lambda b, g, i: (b, i, g)),
        out_shape=jax.ShapeDtypeStruct((B, S, NSA_N_HEADS * NSA_HEAD_DIM), BF16),
        scratch_shapes=[pltpu.VMEM((NT, 2, NSA_PAIRS_PER_ITER * Q_BLOCK, W), F32),
                        pltpu.VMEM((NT, Q_BLOCK, W), F32), pltpu.VMEM((NT, NSA_HEAD_DIM, W), F32)],
        compiler_params=_cparams(("arbitrary", "arbitrary", "arbitrary")),
        name="nsa_select_window",
    )(lst, qt, ks, vst, kw, vwt, sb, glt, oct)


ROW_TILE = 8


def _store_row_tiles(ref, x):
    n = x.shape[0]
    for c in range(ROW_TILE):
        ref[pl.ds(c, n, stride=ROW_TILE), :] = x[:, c * LANES:(c + 1) * LANES]


def _load_row_tiles(ref, start, n):
    return jnp.concatenate([ref[pl.ds(start + c, n, stride=ROW_TILE), :] for c in range(ROW_TILE)], axis=1)


def _merge_kernel(x_ref, ys_ref, on_ref, gs_ref, gn_ref, wssd_ref, wnsa_ref, wout_ref, ln2_ref,
                  rwh_ref, rwl_ref, rb_ref, x1_ref, xn2_ref, route_ref, cnt_ref, base_ref):
    tm = x_ref.shape[0]

    @pl.when(pl.program_id(0) == 0)
    def _():
        base_ref[...] = jnp.zeros(base_ref.shape, F32)

    y_ssd = _dot(ys_ref[...], wssd_ref[...])
    y_nsa = _dot(on_ref[...], wnsa_ref[...])
    merged = (jax.nn.sigmoid(gs_ref[...].astype(F32)) * y_ssd
              + jax.nn.sigmoid(gn_ref[...].astype(F32)) * y_nsa)
    x1 = x_ref[...] + _dot(merged.astype(BF16), wout_ref[...])
    x1_ref[...] = x1
    ms = jnp.mean(x1 * x1, axis=-1, keepdims=True)
    xn2 = x1 * lax.rsqrt(ms + EPS) * ln2_ref[...]
    _store_row_tiles(xn2_ref, xn2)

    xh, xl = _split_bf16(xn2)
    logits = _dot(xh, rwh_ref[...]) + _dot(xl, rwh_ref[...]) + _dot(xh, rwl_ref[...]) + rb_ref[...]
    lane = lax.broadcasted_iota(I32, (tm, LANES), 1)
    lanef = lane.astype(F32)
    v = logits
    onehot = jnp.zeros((tm, LANES), F32)
    vals, picks = [], []
    for _ in range(TOP_K):
        mx = jnp.max(v, axis=-1, keepdims=True)
        first = jnp.min(jnp.where(v == mx, lanef, float(LANES)), axis=-1, keepdims=True)
        pick = lanef == first
        v = jnp.where(pick, 2.0 * NEG, v)
        onehot = jnp.where(pick, 1.0, onehot)
        vals.append(mx)
        picks.append((pick, first))
    es = [jnp.exp(val - vals[0]) for val in vals]
    inv = 1.0 / (es[0] + es[1] + es[2] + es[3])

    r2 = lax.broadcasted_iota(I32, (tm, tm), 0)
    c2 = lax.broadcasted_iota(I32, (tm, tm), 1)
    stril = jnp.where(c2 < r2, 1.0, 0.0).astype(BF16)
    posmap = base_ref[...] + _dot(stril, onehot.astype(BF16))
    route = jnp.zeros((tm, LANES), F32)
    for k in range(TOP_K):
        pick, first = picks[k]
        pos = jnp.sum(jnp.where(pick, posmap, 0.0), axis=-1, keepdims=True)
        route = jnp.where(lane == k, first, route)
        route = jnp.where(lane == TOP_K + k, pos, route)
        route = jnp.where(lane == 2 * TOP_K + k, es[k] * inv, route)
    route_ref[...] = route
    base = base_ref[...] + jnp.sum(onehot, axis=0, keepdims=True)
    base_ref[...] = base
    cnt_ref[...] = jnp.broadcast_to(base, cnt_ref.shape)


def _merge(x2, yssd, onsa, main, wssd, wnsa, wout, ln2, rwh, rwl, rb, tm):
    T = x2.shape[0]
    const = lambda shape: pl.BlockSpec(shape, lambda i: (0, 0))
    row = lambda w, j=0: pl.BlockSpec((tm, w), lambda i: (i, j))
    return pl.pallas_call(
        _merge_kernel,
        grid=(T // tm,),
        in_specs=[row(D_MODEL), row(SSD_D_INNER), row(D_MODEL), row(D_MODEL, C_MS // D_MODEL),
                  row(D_MODEL, C_MN // D_MODEL),
                  const(wssd.shape), const(wnsa.shape), const(wout.shape), const(ln2.shape),
                  const(rwh.shape), const(rwl.shape), const(rb.shape)],
        out_specs=[row(D_MODEL), pl.BlockSpec((tm * ROW_TILE, LANES), lambda i: (i, 0)), row(LANES),
                   const((8, LANES))],
        out_shape=[jax.ShapeDtypeStruct((T, D_MODEL), F32), jax.ShapeDtypeStruct((T * ROW_TILE, LANES), F32),
                   jax.ShapeDtypeStruct((T, LANES), F32), jax.ShapeDtypeStruct((8, LANES), F32)],
        scratch_shapes=[pltpu.VMEM((1, LANES), F32)],
        compiler_params=_cparams(("arbitrary",)),
        name="merge_router",
    )(x2, yssd, onsa, main, main, wssd, wnsa, wout, ln2, rwh, rwl, rb)


def _dispatch_kernel(dest_ref, xn_ref, xs_init, xs_hbm, sem):
    del xs_init
    RT = ROW_TILE
    tc = xn_ref.shape[0] // RT

    def issue(r2, c):
        for pr in (0, 1):
            r = 2 * r2 + pr
            src = pl.multiple_of((r & (tc - 1)) * RT, RT)
            dst = pl.multiple_of(dest_ref[0, r] * RT, RT)
            pltpu.make_async_copy(xn_ref.at[pl.ds(src, RT)], xs_hbm.at[pl.ds(dst, RT)], sem).start(priority=pr)
        return c

    lax.fori_loop(0, TOP_K * tc // 2, issue, 0, unroll=4)
    for _ in range(TOP_K):
        pltpu.make_async_copy(xn_ref, xs_hbm.at[pl.ds(0, tc * RT)], sem).wait()


def _dispatch(dest3, xn2, n_rows, tc):
    T = xn2.shape[0] // ROW_TILE
    xs0 = jnp.zeros((n_rows * ROW_TILE, LANES), F32)
    return pl.pallas_call(
        _dispatch_kernel,
        grid=(T // tc,),
        in_specs=[pl.BlockSpec((None, 1, TOP_K * tc), lambda i: (i, 0, 0), memory_space=pltpu.SMEM),
                  pl.BlockSpec((tc * ROW_TILE, LANES), lambda i: (i, 0)),
                  pl.BlockSpec(memory_space=pl.ANY)],
        out_specs=pl.BlockSpec(memory_space=pl.ANY),
        out_shape=jax.ShapeDtypeStruct(xs0.shape, F32),
        scratch_shapes=[pltpu.SemaphoreType.DMA(())],
        input_output_aliases={2: 0},
        compiler_params=_cparams(("arbitrary",)),
        name="moe_dispatch",
    )(dest3, xn2, xs0)


def _expert_kernel(be_ref, nu_ref, xs_ref, w1g_ref, w1l_ref, b1g_ref, b1l_ref, w2_ref, b2_ref, ys_ref):
    bm = xs_ref.shape[0] // ROW_TILE
    blk = pl.program_id(0)

    @pl.when(blk < nu_ref[0])
    def _():
        x = _load_row_tiles(xs_ref, 0, bm).astype(BF16)
        glu = jnp.minimum(_dot(x, w1g_ref[...]) + b1g_ref[...], SWIGLU_LIMIT)
        lin = jnp.clip(_dot(x, w1l_ref[...]) + b1l_ref[...], -SWIGLU_LIMIT, SWIGLU_LIMIT)
        act = glu * jax.nn.sigmoid(SWIGLU_ALPHA * glu) * (lin + 1.0)
        _store_row_tiles(ys_ref, _dot(act.astype(BF16), w2_ref[...]) + b2_ref[...])

    @pl.when(blk >= nu_ref[0])
    def _():
        ys_ref[...] = jnp.zeros(ys_ref.shape, F32)


def _experts(block_e, n_used, xs, w1p, b1g, b1l, w2, b2, bm):
    n_blocks = xs.shape[0] // (bm * ROW_TILE)
    DE = w2.shape[1]
    wspec = lambda r, c, j=0: pl.BlockSpec((None, r, c), lambda b, be, nu: (be[b], 0, j))
    rows = pl.BlockSpec((bm * ROW_TILE, LANES), lambda b, be, nu: (b, 0))
    return pl.pallas_call(
        _expert_kernel,
        grid_spec=pltpu.PrefetchScalarGridSpec(
            num_scalar_prefetch=2,
            grid=(n_blocks,),
            in_specs=[rows, wspec(D_MODEL, DE, 0), wspec(D_MODEL, DE, 1), wspec(1, DE), wspec(1, DE),
                      wspec(DE, D_MODEL), wspec(1, D_MODEL)],
            out_specs=rows),
        out_shape=jax.ShapeDtypeStruct(xs.shape, F32),
        compiler_params=_cparams(("arbitrary",)),
        name="moe_experts",
    )(block_e, n_used, xs, w1p, w1p, b1g, b1l, w2, b2)


def _deinterleave_kernel(w_ref, p_ref, o_ref):
    CW = p_ref.shape[0]
    half = w_ref.shape[1] // 2
    for c in range(w_ref.shape[1] // CW):
        y = _dot(w_ref[:, c * CW:(c + 1) * CW].astype(BF16), p_ref[...]).astype(BF16)
        o_ref[:, c * LANES:(c + 1) * LANES] = y[:, 0:LANES]
        o_ref[:, half + c * LANES:half + (c + 1) * LANES] = y[:, LANES:CW]


def _deinterleave(w1):
    E, D, W = w1.shape
    CW = 2 * LANES
    src = np.concatenate([np.arange(0, CW, 2), np.arange(1, CW, 2)])
    perm = jnp.asarray(np.arange(CW)[:, None] == src[None, :], BF16)
    rows = D // 2
    return pl.pallas_call(
        _deinterleave_kernel,
        grid=(E, D // rows),
        in_specs=[pl.BlockSpec((None, rows, W), lambda e, r: (e, r, 0)),
                  pl.BlockSpec((CW, CW), lambda e, r: (0, 0))],
        out_specs=pl.BlockSpec((None, rows, W), lambda e, r: (e, r, 0)),
        out_shape=jax.ShapeDtypeStruct((E, D, W), BF16),
        compiler_params=_cparams(("arbitrary", "arbitrary")),
        name="moe_w1_deinterleave",
    )(w1, perm)


def _combine_kernel(dest_ref, destn_ref, route_ref, x1_ref, ys_hbm, o_ref, buf, sem):
    tc = x1_ref.shape[0]
    i = pl.program_id(0)
    slot = i & 1

    RT = ROW_TILE

    def gather(dest_smem, s):
        def issue(r2, c):
            for pr in (0, 1):
                r = 2 * r2 + pr
                src = pl.multiple_of(dest_smem[0, r] * RT, RT)
                dst = pl.multiple_of(r * RT, RT)
                pltpu.make_async_copy(ys_hbm.at[pl.ds(src, RT)], buf.at[s, pl.ds(dst, RT)],
                                      sem.at[s]).start(priority=pr)
            return c

        lax.fori_loop(0, TOP_K * tc // 2, issue, 0, unroll=4)

    @pl.when(i == 0)
    def _():
        gather(dest_ref, 0)

    for s in (0, 1):
        @pl.when((i + 1 < pl.num_programs(0)) & (slot == 1 - s))
        def _():
            gather(destn_ref, s)

    pltpu.make_async_copy(ys_hbm.at[pl.ds(0, TOP_K * tc * RT)], buf.at[slot], sem.at[slot]).wait()
    route = route_ref[...]
    acc = x1_ref[...]
    for k in range(TOP_K):
        acc = acc + route[:, 2 * TOP_K + k:2 * TOP_K + k + 1] * _load_row_tiles(buf.at[slot], k * tc * RT, tc)
    o_ref[...] = acc


def _combine(dest3, route, x1, ys, tc):
    T = x1.shape[0]
    n = T // tc
    return pl.pallas_call(
        _combine_kernel,
        grid=(n,),
        in_specs=[pl.BlockSpec((None, 1, TOP_K * tc), lambda i: (i, 0, 0), memory_space=pltpu.SMEM),
                  pl.BlockSpec((None, 1, TOP_K * tc), lambda i: (jnp.minimum(i + 1, n - 1), 0, 0),
                               memory_space=pltpu.SMEM),
                  pl.BlockSpec((tc, LANES), lambda i: (i, 0)),
                  pl.BlockSpec((tc, D_MODEL), lambda i: (i, 0)),
                  pl.BlockSpec(memory_space=pl.ANY)],
        out_specs=pl.BlockSpec((tc, D_MODEL), lambda i: (i, 0)),
        out_shape=jax.ShapeDtypeStruct((T, D_MODEL), F32),
        scratch_shapes=[pltpu.VMEM((2, TOP_K * tc * ROW_TILE, LANES), F32), pltpu.SemaphoreType.DMA((2,))],
        compiler_params=_cparams(("arbitrary",)),
        name="moe_combine",
    )(dest3, dest3, route, x1, ys)


def _split_pos(pos):
    lo = pos % 256
    return lo.astype(np.float32), (pos - lo).astype(np.float32)


def _aug_keys(k, pos):
    lo, hi = _split_pos(pos)
    cols = np.stack([lo, hi, lo, hi] + [np.zeros_like(lo)] * (K_AUG - NSA_HEAD_DIM - 4), axis=-1)
    cols = jnp.broadcast_to(jnp.asarray(cols, BF16), k.shape[:-1] + (cols.shape[-1],))
    return jnp.concatenate([k, cols], axis=-1)


def _slope_rows():
    slopes = (2.0 ** (-8.0 * np.arange(1, NSA_N_HEADS + 1) / NSA_N_HEADS)).astype(np.float32)
    sl = jnp.asarray(np.repeat((slopes * np.float32(LOG2E)).reshape(NSA_N_KV, NSA_HPG), Q_BLOCK, axis=1), F32)
    hi, lo = _split_bf16(sl)
    zero = jnp.zeros_like(hi)
    return jnp.stack([hi, hi, lo, lo] + [zero] * (K_AUG - NSA_HEAD_DIM - 4), axis=1)


def _nsa(main3, tail3, q_norm_w, k_norm_w, cmp_pos_k, cmp_pos_v, cmp_k_w1, cmp_k_w2, cmp_v_w1, cmp_v_w2):
    B, S, _ = main3.shape
    T = B * S
    G, HPG, DH, QB = NSA_N_KV, NSA_HPG, NSA_HEAD_DIM, Q_BLOCK
    nb, npair, nsel, ncmp = S // QB, S // QB, S // NSA_SEL_LEN, S // NSA_CMP_STRIDE
    main = main3.reshape(T, MAIN_W)

    hd = np.arange(NSA_N_HEADS * DH) // DH
    bdq = jnp.asarray(hd[:, None] == hd[None, :], BF16)
    bdk = bdq[:G * DH, :G * DH]
    wq = (jnp.tile(q_norm_w.astype(F32), NSA_N_HEADS) * (DH ** -0.5 * LOG2E)).reshape(1, -1)
    wk = jnp.tile(k_norm_w.astype(F32), G).reshape(1, -1)
    qt, ks, kw, vst, vwt = _nsa_norm(main3, bdq, bdk, wq, wk, _slope_rows())

    def kv_heads(t):
        return t.reshape(B, S, G, DH).transpose(0, 2, 1, 3)

    def halves(c0):
        return kv_heads(main[:, c0:c0 + G * DH]).reshape(B * G, ncmp, NSA_CMP_STRIDE * DH)

    def pos8(pos):
        return jnp.zeros((8, NSA_CMP_LEN * DH), F32).at[0].set(pos.reshape(-1)).astype(BF16)

    ones = jnp.ones((1, DH), F32)
    kc = _compress(halves(C_KC), pos8(cmp_pos_k), cmp_k_w1.astype(BF16), cmp_k_w2.astype(BF16),
                   k_norm_w.astype(F32).reshape(1, DH), True)
    vc = _compress(halves(C_VC), pos8(cmp_pos_v), cmp_v_w1.astype(BF16), cmp_v_w2.astype(BF16), ones, False)
    cmp_end = np.arange(ncmp) * NSA_CMP_STRIDE + NSA_CMP_LEN - 1
    kc = _aug_keys(kc.astype(BF16).reshape(B, G, ncmp, DH), cmp_end)
    vct = vc.astype(BF16).reshape(B, G, ncmp, DH).transpose(0, 1, 3, 2)

    c_start = (np.arange(ncmp) * NSA_CMP_STRIDE)[:, None]
    s_start = (np.arange(nsel) * NSA_SEL_LEN)[None, :]
    overlap = np.clip(np.minimum(c_start + NSA_CMP_LEN, s_start + NSA_SEL_LEN)
                      - np.maximum(c_start, s_start), 0, None) / NSA_CMP_LEN
    overlap[(S - NSA_CMP_LEN) // NSA_CMP_STRIDE + 1:] = 0.0
    ovt = jnp.asarray(overlap.T, BF16)
    pm = jnp.asarray(np.arange(nsel)[None, :] // 2 == np.arange(nsel // 2)[:, None], BF16)

    gl = tail3.reshape(B, nb, QB, G, LANES)[..., 8:8 + 3 * HPG].reshape(B, nb, QB, G, HPG, 3)
    glt = gl.transpose(0, 3, 1, 5, 4, 2).reshape(B, G, nb, 3, HPG * QB)
    glt = jnp.concatenate([glt, jnp.zeros((B, G, nb, 5, HPG * QB), F32)], axis=3)

    oct, sb, lst = _nsa_cmp(qt, kc, vct, ovt, pm, glt)
    return _nsa_sel(lst, qt, ks, vst, kw, vwt, sb, glt, oct).reshape(T, NSA_N_HEADS * DH)


def _moe_tables(route, cnt, T, bm):
    idx = route[:, 0:TOP_K].astype(I32)
    pos = route[:, TOP_K:2 * TOP_K].astype(I32)
    counts = cnt[0, :N_EXPERTS].astype(I32)
    padded = (counts + bm - 1) // bm * bm
    pend = jnp.cumsum(padded)
    pstart = pend - padded
    dest = pstart[idx] + pos
    n_blocks = -(-(T * TOP_K + N_EXPERTS * (bm - 1)) // bm)
    b_start = jnp.arange(n_blocks, dtype=I32) * bm
    block_e = jnp.minimum(jnp.sum(b_start[:, None] >= pend[None, :], axis=1), N_EXPERTS - 1).astype(I32)
    n_used = (pend[-1] // bm).astype(I32).reshape(1)
    return dest, n_blocks * bm, block_e, n_used


def kernel(x, ln1_w, w_in, ssd_conv_w, ssd_conv_b, ssd_dt_bias, ssd_a_log, ssd_d, ssd_norm_w, ssd_out_w,
           nsa_q_norm_w, nsa_k_norm_w, cmp_pos_k, cmp_pos_v, cmp_k_w1, cmp_k_w2, cmp_v_w1, cmp_v_w2,
           nsa_out_w, w_out, ln2_w, router_w, router_b, exp_w1, exp_b1, exp_w2, exp_b2):
    B, S, D = x.shape
    T = B * S
    depth = ln1_w.shape[0]
    x2 = x.reshape(T, D)
    G = SSD_N_GROUPS

    main_parts = ((0, O_DT), (O_MERGE, O_MERGE + 2 * D_MODEL), (O_Q, O_GATE))

    def group_lanes(v):
        out = jnp.zeros((G, LANES), F32).at[:, :SSD_HPG].set(v.astype(F32).reshape(G, SSD_HPG))
        return out.reshape(1, TAIL_W)

    tm_rows = min(1024, T)
    bm = 512
    for l in range(depth):
        w_main = jnp.concatenate([w_in[l][:, a:b].astype(BF16) for a, b in main_parts], axis=1)
        zpad = jnp.zeros((D, LANES - SSD_HPG - 3 * NSA_HPG), BF16)
        w_tail = jnp.concatenate(
            [part for g in range(G) for part in (
                w_in[l][:, O_DT + SSD_HPG * g:O_DT + SSD_HPG * (g + 1)].astype(BF16),
                w_in[l][:, O_GATE + 3 * NSA_HPG * g:O_GATE + 3 * NSA_HPG * (g + 1)].astype(BF16), zpad)], axis=1)
        main, tail = _inproj(x2, ln1_w[l].reshape(1, D), w_main, w_tail, tm=tm_rows, tn=MAIN_W // 4)
        main3 = main.reshape(B, S, MAIN_W)
        tail3 = tail.reshape(B, S, TAIL_W)

        yssd = _ssd(main3, tail3, ssd_conv_w[l], ssd_conv_b[l].reshape(1, -1),
                    group_lanes(ssd_dt_bias[l]), group_lanes(ssd_a_log[l]),
                    jnp.repeat(ssd_d[l].astype(F32), SSD_HEAD_DIM).reshape(1, -1),
                    ssd_norm_w[l].reshape(1, -1))
        onsa = _nsa(main3, tail3, nsa_q_norm_w[l], nsa_k_norm_w[l], cmp_pos_k[l], cmp_pos_v[l],
                    cmp_k_w1[l], cmp_k_w2[l], cmp_v_w1[l], cmp_v_w2[l])

        rw = jnp.zeros((D, LANES), F32).at[:, :N_EXPERTS].set(router_w[l])
        rwh, rwl = _split_bf16(rw)
        rb = jnp.full((1, LANES), NEG, F32).at[0, :N_EXPERTS].set(router_b[l])
        x1, xn2, route, cnt = _merge(x2, yssd.reshape(T, -1), onsa, main,
                                     ssd_out_w[l].astype(BF16), nsa_out_w[l].astype(BF16), w_out[l].astype(BF16),
                                     ln2_w[l].reshape(1, D), rwh, rwl, rb, tm=min(512, T))

        dest, n_rows, block_e, n_used = _moe_tables(route, cnt, T, bm)
        tc = min(512, T)
        dest3 = dest.reshape(T // tc, tc, TOP_K).transpose(0, 2, 1).reshape(T // tc, 1, TOP_K * tc)
        xs = _dispatch(dest3, xn2, n_rows, tc)
        ys = _experts(block_e, n_used, xs, _deinterleave(exp_w1[l]),
                      exp_b1[l][:, None, 0::2], exp_b1[l][:, None, 1::2],
                      exp_w2[l].astype(BF16), exp_b2[l][:, None, :], bm)
        x2 = _combine(dest3, route, x1, ys, tc)
    return x2.reshape(B, S, D)
```
